```python
import math
import jax, jax.numpy as jnp
from jax import lax
import numpy as np

D_MODEL = 4096
BATCH = 4
SEQ = 4096
DEPTH = 1

CHUNK = 64
Q_BLOCK = 128
HEAD_DIM = 128
MIX_WIDTH = D_MODEL
H_FOX = (MIX_WIDTH // 2) // HEAD_DIM
H_DIFF = (MIX_WIDTH // 2) // (2 * HEAD_DIM)
N_BUCKETS = 32
MAX_DISTANCE = 128
N_MEM = 256
H_MEM = 4
MEM_HEAD_DIM = D_MODEL // H_MEM
D_FF = 4 * D_MODEL
NORM_EPS = 1e-6
SUBLN_EPS = 1e-5
NEG_INF = -1e30

W_FOX_Q = H_FOX * HEAD_DIM
W_FOX_K = H_FOX * HEAD_DIM
W_FOX_V = H_FOX * HEAD_DIM
W_FOX_F = H_FOX
W_DIFF_Q = H_DIFF * 2 * HEAD_DIM
W_DIFF_K = H_DIFF * 2 * HEAD_DIM
W_DIFF_V = H_DIFF * 2 * HEAD_DIM
IN_WIDTH = W_FOX_Q + W_FOX_K + W_FOX_V + W_FOX_F + W_DIFF_Q + W_DIFF_K + W_DIFF_V

kernel_name = "hybrid_fox_diffattn_stream_encoder"


def rms_norm(x, g, eps=NORM_EPS):
    xf = x.astype(jnp.float32)
    y = xf * lax.rsqrt(jnp.mean(xf * xf, axis=-1, keepdims=True) + eps)
    return (y * g.astype(jnp.float32)).astype(x.dtype)


def split_offsets():
    widths = [W_FOX_Q, W_FOX_K, W_FOX_V, W_FOX_F, W_DIFF_Q, W_DIFF_K, W_DIFF_V]
    offs, acc = [], 0
    for w in widths[:-1]:
        acc += w
        offs.append(acc)
    return offs


def t5_bucket(rel):
    half = N_BUCKETS // 2
    max_exact = half // 2
    ret = jnp.where(rel > 0, half, 0)
    n = jnp.abs(rel)
    nf = jnp.maximum(n, 1).astype(jnp.float32)
    large = max_exact + (jnp.log(nf / max_exact) / math.log(MAX_DISTANCE / max_exact)
                         * (half - max_exact)).astype(jnp.int32)
    large = jnp.minimum(large, half - 1)
    return ret + jnp.where(n < max_exact, n, large)


def forgetting_attention(q, k, v, log_f):
    B, S, H, Dh = q.shape
    nb = S // Q_BLOCK
    cum = jnp.cumsum(log_f, axis=1).transpose(0, 2, 1)
    cum_b = cum.reshape(B, H, nb, Q_BLOCK).transpose(2, 0, 1, 3)
    q_b = q.reshape(B, nb, Q_BLOCK, H, Dh).transpose(1, 0, 2, 3, 4)
    kf = k.astype(jnp.float32)
    kpos = jnp.arange(S)
    scale = Dh ** -0.5

    def block(args):
        i, qi, ci = args
        qpos = i * Q_BLOCK + jnp.arange(Q_BLOCK)
        logits = jnp.einsum('bqhd,bkhd->bhqk', qi.astype(jnp.float32), kf) * scale
        logits = logits + (ci[..., :, None] - cum[:, :, None, :])
        mask = kpos[None, :] <= qpos[:, None]
        logits = jnp.where(mask, logits, NEG_INF)
        p = jax.nn.softmax(logits, axis=-1).astype(v.dtype)
        return jnp.einsum('bhqk,bkhd->bqhd', p, v)

    out = lax.map(block, (jnp.arange(nb), q_b, cum_b))
    return out.transpose(1, 0, 2, 3, 4).reshape(B, S, H, Dh)


def differential_attention(q, k, v, lam, rel_bias, g_subln, lambda_init):
    B, S, H, _, Dh = q.shape
    nb = S // Q_BLOCK
    q_b = q.reshape(B, nb, Q_BLOCK, H, 2, Dh).transpose(1, 0, 2, 3, 4, 5)
    kf = k.astype(jnp.float32)
    kpos = jnp.arange(S)
    scale = Dh ** -0.5

    def block(args):
        i, qi = args
        qpos = i * Q_BLOCK + jnp.arange(Q_BLOCK)
        bias = rel_bias[t5_bucket(kpos[None, :] - qpos[:, None])]
        bias = bias.transpose(2, 0, 1).astype(jnp.float32)
        logits = jnp.einsum('bqhcd,bkhcd->bchqk', qi.astype(jnp.float32), kf) * scale + bias
        mask = (kpos // CHUNK)[None, :] <= (qpos // CHUNK)[:, None]
        logits = jnp.where(mask, logits, NEG_INF)
        p = jax.nn.softmax(logits, axis=-1)
        a = p[:, 0] - lam * p[:, 1]
        return jnp.einsum('bhqk,bkhe->bqhe', a.astype(v.dtype), v)

    out = lax.map(block, (jnp.arange(nb), q_b))
    out = out.transpose(1, 0, 2, 3, 4).reshape(B, S, H, 2 * Dh)
    out = rms_norm(out, g_subln, eps=SUBLN_EPS)
    return out * (1.0 - lambda_init)


def memory_cross_attention(c, m, wq, wk, wv, wo):
    B, S, _ = c.shape
    M = m.shape[1]
    q = (c @ wq).reshape(B, S, H_MEM, MEM_HEAD_DIM)
    k = (m @ wk).reshape(B, M, H_MEM, MEM_HEAD_DIM)
    v = (m @ wv).reshape(B, M, H_MEM, MEM_HEAD_DIM)
    logits = jnp.einsum('bqhd,bmhd->bhqm', q.astype(jnp.float32), k.astype(jnp.float32)) * MEM_HEAD_DIM ** -0.5
    p = jax.nn.softmax(logits, axis=-1).astype(v.dtype)
    o = jnp.einsum('bhqm,bmhd->bqhd', p, v).reshape(B, S, H_MEM * MEM_HEAD_DIM)
    return o @ wo


def setup_inputs(seed: int = 0) -> dict:
    key = jax.random.key(seed)
    ks = jax.random.split(key, 24)
    D, L = D_MODEL, DEPTH

    def nrm(k, shape, scale):
        return jax.random.normal(k, shape, jnp.float32) * scale

    def gain(k, shape):
        return 1.0 + 0.02 * jax.random.normal(k, shape, jnp.float32)

    return {
        "x": nrm(ks[0], (BATCH, SEQ, D), 1.0),
        "mem": nrm(ks[1], (BATCH, N_MEM, D), 1.0),
        "g_mix": gain(ks[2], (L, D)),
        "w_in": nrm(ks[3], (L, D, IN_WIDTH), D ** -0.5),
        "b_forget": 2.0 + 0.1 * jax.random.normal(ks[4], (L, H_FOX), jnp.float32),
        "lambda_q1": nrm(ks[5], (L, HEAD_DIM), 0.1),
        "lambda_k1": nrm(ks[6], (L, HEAD_DIM), 0.1),
        "lambda_q2": nrm(ks[7], (L, HEAD_DIM), 0.1),
        "lambda_k2": nrm(ks[8], (L, HEAD_DIM), 0.1),
        "g_subln": gain(ks[9], (L, 2 * HEAD_DIM)),
        "rel_bias": nrm(ks[10], (N_BUCKETS, H_DIFF), 0.5),
        "w_out": nrm(ks[11], (L, MIX_WIDTH, D), MIX_WIDTH ** -0.5),
        "g_cross": gain(ks[12], (L, D)),
        "g_mem": gain(ks[13], (L, D)),
        "wq_mem": nrm(ks[14], (L, D, H_MEM * MEM_HEAD_DIM), D ** -0.5),
        "wk_mem": nrm(ks[15], (L, D, H_MEM * MEM_HEAD_DIM), D ** -0.5),
        "wv_mem": nrm(ks[16], (L, D, H_MEM * MEM_HEAD_DIM), D ** -0.5),
        "wo_mem": nrm(ks[17], (L, H_MEM * MEM_HEAD_DIM, D), (H_MEM * MEM_HEAD_DIM) ** -0.5),
        "g_mlp": gain(ks[18], (L, D)),
        "w_up": nrm(ks[19], (L, D, D_FF), D ** -0.5),
        "w_down": nrm(ks[20], (L, D_FF, D), D_FF ** -0.5),
        "g_final": gain(ks[21], (D,)),
    }


def reference(x, mem, g_mix, w_in, b_forget, lambda_q1, lambda_k1, lambda_q2, lambda_k2,
              g_subln, rel_bias, w_out, g_cross, g_mem, wq_mem, wk_mem, wv_mem, wo_mem,
              g_mlp, w_up, w_down, g_final):
    B, S, _ = x.shape
    offs = split_offsets()
    h = x
    for l in range(DEPTH):
        a = rms_norm(h, g_mix[l])
        proj = a @ w_in[l]
        q_f, k_f, v_f, f_logit, q_d, k_d, v_d = jnp.split(proj, offs, axis=-1)

        log_f = jax.nn.log_sigmoid((f_logit + b_forget[l]).astype(jnp.float32))
        fox = forgetting_attention(q_f.reshape(B, S, H_FOX, HEAD_DIM),
                                   k_f.reshape(B, S, H_FOX, HEAD_DIM),
                                   v_f.reshape(B, S, H_FOX, HEAD_DIM), log_f)

        lambda_init = 0.8 - 0.6 * math.exp(-0.3 * l)
        lam = (jnp.exp(jnp.sum(lambda_q1[l].astype(jnp.float32) * lambda_k1[l].astype(jnp.float32)))
               - jnp.exp(jnp.sum(lambda_q2[l].astype(jnp.float32) * lambda_k2[l].astype(jnp.float32)))
               + lambda_init)
        diff = differential_attention(q_d.reshape(B, S, H_DIFF, 2, HEAD_DIM),
                                      k_d.reshape(B, S, H_DIFF, 2, HEAD_DIM),
                                      v_d.reshape(B, S, H_DIFF, 2 * HEAD_DIM),
                                      lam, rel_bias, g_subln[l], lambda_init)

        mixed = jnp.concatenate([fox.reshape(B, S, W_FOX_V), diff.reshape(B, S, W_DIFF_V)], axis=-1)
        h = h + mixed @ w_out[l]

        h = h + memory_cross_attention(rms_norm(h, g_cross[l]), rms_norm(mem, g_mem[l]),
                                       wq_mem[l], wk_mem[l], wv_mem[l], wo_mem[l])

        u = rms_norm(h, g_mlp[l]) @ w_up[l]
        h = h + jnp.square(jax.nn.relu(u)) @ w_down[l]
    return rms_norm(h, g_final)
```

```python
import functools
import math

import jax
import jax.numpy as jnp
from jax import lax
from jax.experimental import pallas as pl
from jax.experimental.pallas import tpu as pltpu

HEAD_DIM = 128
CHUNK = 64
N_BUCKETS = 32
MAX_DISTANCE = 128
H_MEM = 4
NORM_EPS = 1e-6
SUBLN_EPS = 1e-5
NEG_INF = -1e30

V7X_LANES = 128
V7X_SUBLANES = 8
V7X_VMEM_BYTES = 64 * 1024 * 1024
V7X_VMEM_REQUEST_CAP = 56 * 1024 * 1024

F32 = jnp.float32
BF16 = jnp.bfloat16


def _vmem_limit(block_bytes, scratch_bytes=0, temp_bytes=0):
    need = 2 * block_bytes + scratch_bytes + temp_bytes + (4 << 20)
    return int(min(max(need, 16 << 20), V7X_VMEM_REQUEST_CAP))


def _nbytes(shape, dtype):
    return math.prod(shape) * jnp.dtype(dtype).itemsize


def _pick_tile(n, target):
    if n <= target:
        return n
    t = target
    while t >= V7X_LANES:
        if n % t == 0:
            return t
        t -= V7X_LANES
    return n


def _t5_thresholds():
    half = N_BUCKETS // 2
    max_exact = half // 2
    steps = half - max_exact
    ratio = MAX_DISTANCE // max_exact
    thr = []
    for k in range(1, steps):
        n = max_exact
        while n ** steps < (ratio ** k) * (max_exact ** steps):
            n += 1
        thr.append(n)
    return half, max_exact, tuple(thr)


def _norm_matmul_kernel(x_ref, g_ref, w_ref, *rest, eps, with_gate):
    if with_gate:
        wf_ref, o_ref, f_ref, a_ref = rest
    else:
        o_ref, a_ref = rest

    @pl.when(pl.program_id(1) == 0)
    def _():
        x = x_ref[...]
        ms = jnp.mean(x * x, axis=-1, keepdims=True)
        a = (x * lax.rsqrt(ms + eps) * g_ref[...]).astype(BF16)
        a_ref[...] = a
        if with_gate:
            f_ref[...] = jnp.dot(a, wf_ref[...], preferred_element_type=F32)

    o_ref[...] = jnp.dot(a_ref[...], w_ref[...], preferred_element_type=F32).astype(o_ref.dtype)


def _norm_matmul(x, g, w, wf=None, *, eps=NORM_EPS, tm=512, tn=1024, name):
    m, k = x.shape
    n = w.shape[1]
    tm = _pick_tile(m, tm)
    tn = _pick_tile(n, tn)
    with_gate = wf is not None
    in_specs = [
        pl.BlockSpec((tm, k), lambda i, j: (i, 0), pipeline_mode=pl.Buffered(1)),
        pl.BlockSpec((1, k), lambda i, j: (0, 0)),
        pl.BlockSpec((k, tn), lambda i, j: (0, j)),
    ]
    args = [x, g.reshape(1, k), w]
    out_shape = [jax.ShapeDtypeStruct((m, n), BF16)]
    out_specs = [pl.BlockSpec((tm, tn), lambda i, j: (i, j))]
    blocks = _nbytes((k, tn), BF16) + _nbytes((tm, tn), BF16)
    if with_gate:
        nf = wf.shape[1]
        in_specs.append(pl.BlockSpec((k, nf), lambda i, j: (0, 0)))
        args.append(wf)
        out_shape.append(jax.ShapeDtypeStruct((m, nf), F32))
        out_specs.append(pl.BlockSpec((tm, nf), lambda i, j: (i, 0)))
        blocks += _nbytes((k, nf), BF16) + _nbytes((tm, nf), F32)
    res = pl.pallas_call(
        functools.partial(_norm_matmul_kernel, eps=eps, with_gate=with_gate),
        grid=(m // tm, n // tn),
        in_specs=in_specs,
        out_specs=out_specs,
        out_shape=out_shape,
        scratch_shapes=[pltpu.VMEM((tm, k), BF16)],
        compiler_params=pltpu.CompilerParams(
            dimension_semantics=("parallel", "arbitrary"),
            vmem_limit_bytes=_vmem_limit(
                blocks, _nbytes((tm, k), F32) + _nbytes((tm, k), BF16), 2 * _nbytes((tm, k), F32)),
        ),
        name=name,
    )(*args)
    return res if with_gate else res[0]


def _matmul_resid_kernel(a1_ref, a2_ref, w1_ref, w2_ref, r_ref, o_ref):
    acc = jnp.dot(a1_ref[...], w1_ref[...], preferred_element_type=F32)
    acc = acc + jnp.dot(a2_ref[...], w2_ref[...], preferred_element_type=F32)
    o_ref[...] = r_ref[...] + acc


def _matmul_resid(a1, a2, a_blk, w1, w2, w_blk, resid, *, tm=512, tn=1024, name):
    (a1, a1c), (a2, a2c) = a1, a2
    (w1, w1r), (w2, w2r) = w1, w2
    m, n = resid.shape
    tm = _pick_tile(m, tm)
    tn = _pick_tile(n, tn)
    blocks = (2 * _nbytes((tm, a_blk), BF16) + 2 * _nbytes((w_blk, tn), BF16)
              + 2 * _nbytes((tm, tn), F32))
    return pl.pallas_call(
        _matmul_resid_kernel,
        grid=(m // tm, n // tn),
        in_specs=[
            pl.BlockSpec((tm, a_blk), lambda i, j: (i, a1c)),
            pl.BlockSpec((tm, a_blk), lambda i, j: (i, a2c)),
            pl.BlockSpec((w_blk, tn), lambda i, j: (w1r, j)),
            pl.BlockSpec((w_blk, tn), lambda i, j: (w2r, j)),
            pl.BlockSpec((tm, tn), lambda i, j: (i, j)),
        ],
        out_specs=pl.BlockSpec((tm, tn), lambda i, j: (i, j)),
        out_shape=jax.ShapeDtypeStruct((m, n), F32),
        compiler_params=pltpu.CompilerParams(
            dimension_semantics=("parallel", "arbitrary"),
            vmem_limit_bytes=_vmem_limit(blocks, 0, 2 * _nbytes((tm, tn), F32)),
        ),
        name=name,
    )(a1, a2, w1, w2, resid)


def _gate_cumsum_kernel(f_ref, b_ref, o_ref, *, n_rows):
    z = f_ref[...] + b_ref[...]
    x = jnp.minimum(z, 0.0) - jnp.log1p(jnp.exp(-jnp.abs(z)))
    s_len = x.shape[0]
    row = lax.broadcasted_iota(jnp.int32, x.shape, 0)
    d = 1
    while d < s_len:
        x = x + jnp.where(row >= d, pltpu.roll(x, d, axis=0), 0.0)
        d *= 2
    o_ref[0] = x.T[:n_rows, :]


def _gate_cumsum(f_logit, b_pad, batch, seq, n_rows):
    nf = f_logit.shape[1]
    return pl.pallas_call(
        functools.partial(_gate_cumsum_kernel, n_rows=n_rows),
        grid=(batch,),
        in_specs=[
            pl.BlockSpec((seq, nf), lambda b: (b, 0)),
            pl.BlockSpec((1, nf), lambda b: (0, 0)),
        ],
        out_specs=pl.BlockSpec((1, n_rows, seq), lambda b: (b, 0, 0)),
        out_shape=jax.ShapeDtypeStruct((batch, n_rows, seq), F32),
        compiler_params=pltpu.CompilerParams(
            dimension_semantics=("parallel",),
            vmem_limit_bytes=_vmem_limit(2 * _nbytes((seq, nf), F32), 0, 6 * _nbytes((seq, nf), F32)),
        ),
        name="gate_cumsum",
    )(f_logit, b_pad)


def _bias_tile_kernel(rb_ref, o_ref, *, tile, n_heads):
    h = pl.program_id(0)
    d = pl.program_id(1)
    half, max_exact, thresholds = _t5_thresholds()
    t = lax.broadcasted_iota(jnp.int32, (tile, tile), 0)
    s = lax.broadcasted_iota(jnp.int32, (tile, tile), 1)
    rel = s - t - d * tile
    n = jnp.abs(rel)
    large = jnp.full((tile, tile), max_exact, jnp.int32)
    for thr in thresholds:
        large = large + (n >= thr).astype(jnp.int32)
    idx = jnp.where(n < max_exact, n, large) + jnp.where(rel > 0, half, 0)
    val = jnp.zeros((tile, tile), F32)
    for b in range(N_BUCKETS):
        val = jnp.where(idx == b, rb_ref[b * n_heads + h], val)
    shift = int(math.log2(CHUNK))
    allowed = (s >> shift) <= ((t >> shift) + d * tile)
    o_ref[0, 0] = jnp.where(allowed, val, NEG_INF)


def _bias_tiles(rel_bias, tile):
    n_heads = rel_bias.shape[1]
    return pl.pallas_call(
        functools.partial(_bias_tile_kernel, tile=tile, n_heads=n_heads),
        grid=(n_heads, 3),
        in_specs=[pl.BlockSpec(memory_space=pltpu.SMEM)],
        out_specs=pl.BlockSpec((1, 1, tile, tile), lambda h, d: (h, d, 0, 0)),
        out_shape=jax.ShapeDtypeStruct((n_heads, 3, tile, tile), F32),
        compiler_params=pltpu.CompilerParams(
            dimension_semantics=("parallel", "parallel"),
            vmem_limit_bytes=_vmem_limit(_nbytes((tile, tile), F32), 0, 8 * _nbytes((tile, tile), F32)),
        ),
        name="t5_bias_tiles",
    )(rel_bias.reshape(-1))


def _lane_tile(x, reps):
    return x if reps == 1 else jnp.concatenate([x] * reps, axis=1)


def _online_softmax_step(s, v, m_ref, l_ref, acc_ref):
    tk = s.shape[1]
    m_prev = m_ref[...]
    m_new = jnp.maximum(m_prev, jnp.max(s, axis=1, keepdims=True))
    alpha = jnp.exp(m_prev - m_new)
    p = jnp.exp(s - _lane_tile(m_new, tk // V7X_LANES))
    l_ref[...] = alpha * l_ref[...] + jnp.sum(p, axis=1, keepdims=True)
    pv = jnp.dot(p.astype(BF16), v, preferred_element_type=F32)
    acc_ref[...] = _lane_tile(alpha, acc_ref.shape[1] // V7X_LANES) * acc_ref[...] + pv
    m_ref[...] = m_new


def _qk(q, k):
    return lax.dot_general(q, k, (((1,), (1,)), ((), ())), preferred_element_type=F32)


def _fox_kernel(q_ref, k_ref, v_ref, c_ref, o_ref, m_ref, l_ref, acc_ref, *, tile, scale):
    i = pl.program_id(2)
    q = (q_ref[...].astype(F32) * scale).astype(BF16)
    m_ref[...] = jnp.full(m_ref.shape, NEG_INF, F32)
    l_ref[...] = jnp.zeros(l_ref.shape, F32)
    acc_ref[...] = jnp.zeros(acc_ref.shape, F32)
    c_q0 = c_ref[0, :, pl.ds(pl.multiple_of(i * tile, tile), 1)]

    def scores(j):
        k0 = pl.multiple_of(j * tile, tile)
        s = _qk(q, k_ref[pl.ds(k0, tile), :])
        return s + (c_q0 - c_ref[0, :, pl.ds(k0, tile)]), v_ref[pl.ds(k0, tile), :]

    def body(j, carry):
        s, v = scores(j)
        _online_softmax_step(s, v, m_ref, l_ref, acc_ref)
        return carry

    lax.fori_loop(0, i, body, 0)
    s, v = scores(i)
    row = lax.broadcasted_iota(jnp.int32, s.shape, 0)
    col = lax.broadcasted_iota(jnp.int32, s.shape, 1)
    _online_softmax_step(jnp.where(col <= row, s, NEG_INF), v, m_ref, l_ref, acc_ref)
    o_ref[...] = (acc_ref[...] / l_ref[...]).astype(o_ref.dtype)


def _fox_attention(proj, c_rows, batch, seq, n_heads, tile):
    nq = seq // tile
    hd = HEAD_DIM
    blocks = (2 * _nbytes((tile, hd), BF16) + 2 * _nbytes((seq, hd), BF16)
              + _nbytes((V7X_SUBLANES, seq), F32))
    return pl.pallas_call(
        functools.partial(_fox_kernel, tile=tile, scale=hd ** -0.5),
        grid=(batch, n_heads, nq),
        in_specs=[
            pl.BlockSpec((tile, hd), lambda b, h, i: (b * nq + i, h)),
            pl.BlockSpec((seq, hd), lambda b, h, i: (b, n_heads + h)),
            pl.BlockSpec((seq, hd), lambda b, h, i: (b, 2 * n_heads + h)),
            pl.BlockSpec((1, 1, seq), lambda b, h, i: (b * n_heads + h, 0, 0)),
        ],
        out_specs=pl.BlockSpec((tile, hd), lambda b, h, i: (b * nq + i, h)),
        out_shape=jax.ShapeDtypeStruct((batch * seq, n_heads * hd), BF16),
        scratch_shapes=[
            pltpu.VMEM((tile, V7X_LANES), F32),
            pltpu.VMEM((tile, V7X_LANES), F32),
            pltpu.VMEM((tile, hd), F32),
        ],
        compiler_params=pltpu.CompilerParams(
            dimension_semantics=("parallel", "parallel", "arbitrary"),
            vmem_limit_bytes=_vmem_limit(blocks, 3 * _nbytes((tile, hd), F32), 6 * _nbytes((tile, tile), F32)),
        ),
        name="fox_attention",
    )(proj, proj, proj, c_rows)


def _diff_kernel(q_ref, k_ref, v_ref, bias_ref, lq1_ref, lk1_ref, lq2_ref, lk2_ref, g_ref, o_ref,
                 m1_ref, l1_ref, acc1_ref, m2_ref, l2_ref, acc2_ref, *, tile, scale, lambda_init):
    i = pl.program_id(2)
    hd = HEAD_DIM
    q = (q_ref[...].astype(F32) * scale).astype(BF16)
    q1, q2 = q[:, :hd], q[:, hd:]
    for m_ref, l_ref, acc_ref in ((m1_ref, l1_ref, acc1_ref), (m2_ref, l2_ref, acc2_ref)):
        m_ref[...] = jnp.full(m_ref.shape, NEG_INF, F32)
        l_ref[...] = jnp.zeros(l_ref.shape, F32)
        acc_ref[...] = jnp.zeros(acc_ref.shape, F32)

    def body(j, carry):
        k0 = pl.multiple_of(j * tile, tile)
        k = k_ref[pl.ds(k0, tile), :]
        v = v_ref[pl.ds(k0, tile), :]
        bias = bias_ref[0, jnp.minimum(i - j, 2)]
        _online_softmax_step(_qk(q1, k[:, :hd]) + bias, v, m1_ref, l1_ref, acc1_ref)
        _online_softmax_step(_qk(q2, k[:, hd:]) + bias, v, m2_ref, l2_ref, acc2_ref)
        return carry

    lax.fori_loop(0, i + 1, body, 0)

    lam = (jnp.exp(jnp.sum(lq1_ref[...] * lk1_ref[...], axis=1, keepdims=True))
           - jnp.exp(jnp.sum(lq2_ref[...] * lk2_ref[...], axis=1, keepdims=True))
           + lambda_init)
    reps = acc1_ref.shape[1] // V7X_LANES
    out = (acc1_ref[...] / _lane_tile(l1_ref[...], reps)
           - lam * (acc2_ref[...] / _lane_tile(l2_ref[...], reps)))
    ms = jnp.mean(out * out, axis=-1, keepdims=True)
    y = out * lax.rsqrt(ms + SUBLN_EPS) * g_ref[...]
    o_ref[...] = (y * (1.0 - lambda_init)).astype(o_ref.dtype)


def _diff_attention(proj, bias, lq1, lk1, lq2, lk2, g_subln, batch, seq, n_heads, col0, tile, lambda_init):
    nq = seq // tile
    hd2 = 2 * HEAD_DIM
    c0 = col0 // hd2
    vec = lambda: pl.BlockSpec((1, HEAD_DIM), lambda b, h, i: (0, 0))
    blocks = (2 * _nbytes((tile, hd2), BF16) + 2 * _nbytes((seq, hd2), BF16)
              + _nbytes((3, tile, tile), F32))
    return pl.pallas_call(
        functools.partial(_diff_kernel, tile=tile, scale=HEAD_DIM ** -0.5, lambda_init=lambda_init),
        grid=(batch, n_heads, nq),
        in_specs=[
            pl.BlockSpec((tile, hd2), lambda b, h, i: (b * nq + i, c0 + h)),
            pl.BlockSpec((seq, hd2), lambda b, h, i: (b, c0 + n_heads + h)),
            pl.BlockSpec((seq, hd2), lambda b, h, i: (b, c0 + 2 * n_heads + h)),
            pl.BlockSpec((1, 3, tile, tile), lambda b, h, i: (h, 0, 0, 0)),
            vec(), vec(), vec(), vec(),
            pl.BlockSpec((1, hd2), lambda b, h, i: (0, 0)),
        ],
        out_specs=pl.BlockSpec((tile, hd2), lambda b, h, i: (b * nq + i, h)),
        out_shape=jax.ShapeDtypeStruct((batch * seq, n_heads * hd2), BF16),
        scratch_shapes=[
            pltpu.VMEM((tile, V7X_LANES), F32), pltpu.VMEM((tile, V7X_LANES), F32),
            pltpu.VMEM((tile, hd2), F32),
            pltpu.VMEM((tile, V7X_LANES), F32), pltpu.VMEM((tile, V7X_LANES), F32),
            pltpu.VMEM((tile, hd2), F32),
        ],
        compiler_params=pltpu.CompilerParams(
            dimension_semantics=("parallel", "parallel", "arbitrary"),
            vmem_limit_bytes=_vmem_limit(blocks, 4 * _nbytes((tile, hd2), F32), 8 * _nbytes((tile, tile), F32)),
        ),
        name="diff_attention",
    )(proj, proj, proj, bias, lq1, lk1, lq2, lk2, g_subln)


def _cross_kernel(q_ref, k_ref, v_ref, o_ref, *, n_heads, scale):
    dh = q_ref.shape[1] // n_heads
    for h in range(n_heads):
        cols = slice(h * dh, (h + 1) * dh)
        s = _qk(q_ref[:, cols], k_ref[:, cols]) * scale
        m = jnp.max(s, axis=1, keepdims=True)
        p = jnp.exp(s - m)
        p = p / jnp.sum(p, axis=1, keepdims=True)
        o_ref[:, cols] = jnp.dot(p.astype(BF16), v_ref[:, cols], preferred_element_type=F32).astype(o_ref.dtype)


def _cross_attention(q, kv, batch, seq, n_mem, tile):
    d = q.shape[1]
    nq = seq // tile
    blocks = 2 * _nbytes((tile, d), BF16) + 2 * _nbytes((n_mem, d), BF16)
    return pl.pallas_call(
        functools.partial(_cross_kernel, n_heads=H_MEM, scale=(d // H_MEM) ** -0.5),
        grid=(batch, nq),
        in_specs=[
            pl.BlockSpec((tile, d), lambda b, i: (b * nq + i, 0)),
            pl.BlockSpec((n_mem, d), lambda b, i: (b, 0)),
            pl.BlockSpec((n_mem, d), lambda b, i: (b, 1)),
        ],
        out_specs=pl.BlockSpec((tile, d), lambda b, i: (b * nq + i, 0)),
        out_shape=jax.ShapeDtypeStruct((batch * seq, d), BF16),
        compiler_params=pltpu.CompilerParams(
            dimension_semantics=("parallel", "arbitrary"),
            vmem_limit_bytes=_vmem_limit(blocks, 0, 8 * _nbytes((tile, n_mem), F32) + _nbytes((tile, d), F32)),
        ),
        name="cross_attention",
    )(q, kv, kv)


def _mlp_kernel(h_ref, g_ref, wu_ref, wd_ref, gf_ref, o_ref, a_ref):
    f = pl.program_id(1)

    @pl.when(f == 0)
    def _():
        x = h_ref[...]
        ms = jnp.mean(x * x, axis=-1, keepdims=True)
        a_ref[...] = (x * lax.rsqrt(ms + NORM_EPS) * g_ref[...]).astype(BF16)
        o_ref[...] = x

    u = jnp.dot(a_ref[...], wu_ref[...], preferred_element_type=F32)
    act = jnp.square(jnp.maximum(u, 0.0)).astype(BF16)
    o_ref[...] += jnp.dot(act, wd_ref[...], preferred_element_type=F32)

    @pl.when(f == pl.num_programs(1) - 1)
    def _():
        y = o_ref[...]
        ms = jnp.mean(y * y, axis=-1, keepdims=True)
        o_ref[...] = y * lax.rsqrt(ms + NORM_EPS) * gf_ref[...]


def _mlp(h, g_mlp, w_up, w_down, g_final, *, tm=512, tf=512):
    m, d = h.shape
    dff = w_up.shape[1]
    tm = _pick_tile(m, tm)
    tf = _pick_tile(dff, tf)
    blocks = 2 * _nbytes((d, tf), BF16) + _nbytes((tm, d), F32)
    return pl.pallas_call(
        _mlp_kernel,
        grid=(m // tm, dff // tf),
        in_specs=[
            pl.BlockSpec((tm, d), lambda i, f: (i, 0), pipeline_mode=pl.Buffered(1)),
            pl.BlockSpec((1, d), lambda i, f: (0, 0)),
            pl.BlockSpec((d, tf), lambda i, f: (0, f)),
            pl.BlockSpec((tf, d), lambda i, f: (f, 0)),
            pl.BlockSpec((1, d), lambda i, f: (0, 0)),
        ],
        out_specs=pl.BlockSpec((tm, d), lambda i, f: (i, 0)),
        out_shape=jax.ShapeDtypeStruct((m, d), F32),
        scratch_shapes=[pltpu.VMEM((tm, d), BF16)],
        compiler_params=pltpu.CompilerParams(
            dimension_semantics=("parallel", "arbitrary"),
            vmem_limit_bytes=_vmem_limit(
                blocks, _nbytes((tm, d), F32) + _nbytes((tm, d), BF16),
                2 * _nbytes((tm, tf), F32) + _nbytes((tm, d), F32)),
        ),
        name="mlp_final_norm",
    )(h, g_mlp.reshape(1, d), w_up, w_down, g_final.reshape(1, d))


def _attn_tile(seq):
    return _pick_tile(seq, 512)


def _layer(h, mem, l, g_mix, w_in, b_forget, lambda_q1, lambda_k1, lambda_q2, lambda_k2, g_subln, rel_bias,
           w_out, g_cross, g_mem, wq_mem, wk_mem, wv_mem, wo_mem, g_mlp, w_up, w_down, g_final, final):
    batch, seq, d = h.shape
    n_mem = mem.shape[1]
    h_fox = b_forget.shape[-1]
    h_diff = rel_bias.shape[1]
    w_fox = h_fox * HEAD_DIM
    w_diff = h_diff * 2 * HEAD_DIM
    tile = _attn_tile(seq)
    x2 = h.reshape(batch * seq, d)

    wi = w_in[l]
    f0 = 3 * w_fox
    w_qkv = jnp.concatenate([wi[:, :f0], wi[:, f0 + h_fox:]], axis=1).astype(BF16)
    n_gate = -(-h_fox // V7X_LANES) * V7X_LANES
    w_gate = jnp.pad(wi[:, f0:f0 + h_fox], ((0, 0), (0, n_gate - h_fox))).astype(BF16)
    b_gate = jnp.pad(b_forget[l].astype(F32), (0, n_gate - h_fox)).reshape(1, n_gate)

    proj, f_logit = _norm_matmul(x2, g_mix[l], w_qkv, w_gate, name="in_proj")

    gate_rows = -(-h_fox // V7X_SUBLANES) * V7X_SUBLANES
    c = _gate_cumsum(f_logit, b_gate, batch, seq, gate_rows)
    c_rows = c[:, :h_fox, :].reshape(batch * h_fox, 1, seq)
    fox = _fox_attention(proj, c_rows, batch, seq, h_fox, tile)

    lambda_init = 0.8 - 0.6 * math.exp(-0.3 * l)
    bias = _bias_tiles(rel_bias.astype(F32), tile)
    row = lambda v: v[l].astype(F32).reshape(1, -1)
    diff = _diff_attention(proj, bias, row(lambda_q1), row(lambda_k1), row(lambda_q2), row(lambda_k2),
                           row(g_subln), batch, seq, h_diff, 3 * w_fox, tile, lambda_init)

    w_o = w_out[l].astype(BF16)
    h1 = _matmul_resid((fox, 0), (diff, 0), w_fox, (w_o, 0), (w_o, 1), w_fox, x2, name="out_proj")

    q = _norm_matmul(h1, g_cross[l], wq_mem[l].astype(BF16), name="cross_q_proj")
    w_kv = jnp.concatenate([wk_mem[l], wv_mem[l]], axis=1).astype(BF16)
    kv = _norm_matmul(mem.reshape(batch * n_mem, d), g_mem[l], w_kv, name="cross_kv_proj")
    o = _cross_attention(q, kv, batch, seq, n_mem, tile)
    w_om = wo_mem[l].astype(BF16)
    half = d // 2
    h2 = _matmul_resid((o, 0), (o, 1), half, (w_om, 0), (w_om, 1), half, h1, name="cross_o_proj")

    gf = g_final if final else jnp.ones((d,), F32)
    out = _mlp(h2, g_mlp[l], w_up[l].astype(BF16), w_down[l].astype(BF16), gf)
    return out.reshape(batch, seq, d)


def kernel(x, mem, g_mix, w_in, b_forget, lambda_q1, lambda_k1, lambda_q2, lambda_k2, g_subln, rel_bias, w_out,
           g_cross, g_mem, wq_mem, wk_mem, wv_mem, wo_mem, g_mlp, w_up, w_down, g_final):
    depth = g_mix.shape[0]
    assert depth == 1, "the fused MLP epilogue applies the final norm; only depth 1 is supported"
    return _layer(x, mem, 0, g_mix, w_in, b_forget, lambda_q1, lambda_k1, lambda_q2, lambda_k2, g_subln,
                  rel_bias, w_out, g_cross, g_mem, wq_mem, wk_mem, wv_mem, wo_mem, g_mlp, w_up, w_down,
                  g_final, True)
```

```python
import functools
import math

import jax
import jax.numpy as jnp
from jax import lax
from jax.experimental import pallas as pl
from jax.experimental.pallas import tpu as pltpu

HEAD_DIM = 128
CHUNK = 64
N_BUCKETS = 32
MAX_DISTANCE = 128
H_MEM = 4
NORM_EPS = 1e-6
SUBLN_EPS = 1e-5
NEG_INF = -1e30
LOG2E = math.log2(math.e)

V7X_LANES = 128
V7X_VMEM_BYTES = 64 * 1024 * 1024
V7X_VMEM_REQUEST_CAP = V7X_VMEM_BYTES - 8 * 1024 * 1024

F32 = jnp.float32
BF16 = jnp.bfloat16


def _vmem_limit(block_bytes, scratch_bytes=0, temp_bytes=0):
    need = 2 * block_bytes + scratch_bytes + temp_bytes + (4 << 20)
    return int(min(max(need, 16 << 20), V7X_VMEM_REQUEST_CAP))


def _nbytes(shape, dtype):
    return math.prod(shape) * jnp.dtype(dtype).itemsize


def _pick_tile(n, target):
    if n <= target:
        return n
    t = target
    while t >= V7X_LANES:
        if n % t == 0:
            return t
        t -= V7X_LANES
    return n


def _t5_thresholds():
    half = N_BUCKETS // 2
    max_exact = half // 2
    steps = half - max_exact
    ratio = MAX_DISTANCE // max_exact
    thr = []
    for k in range(1, steps):
        n = max_exact
        while n ** steps < (ratio ** k) * (max_exact ** steps):
            n += 1
        thr.append(n)
    return half, max_exact, tuple(thr)


def _norm_matmul_kernel(x_ref, g_ref, w_ref, *rest, eps, with_gate):
    if with_gate:
        wf_ref, o_ref, f_ref, a_ref = rest
    else:
        o_ref, a_ref = rest

    @pl.when(pl.program_id(1) == 0)
    def _():
        x = x_ref[...]
        ms = jnp.mean(x * x, axis=-1, keepdims=True)
        a = (x * lax.rsqrt(ms + eps) * g_ref[...]).astype(BF16)
        a_ref[...] = a
        if with_gate:
            f_ref[...] = jnp.dot(a, wf_ref[...], preferred_element_type=F32)

    o_ref[...] = jnp.dot(a_ref[...], w_ref[...], preferred_element_type=F32).astype(o_ref.dtype)


def _norm_matmul(x, g, w, wf=None, *, eps=NORM_EPS, tm=512, tn=1024, name):
    m, k = x.shape
    n = w.shape[1]
    tm = _pick_tile(m, tm)
    tn = _pick_tile(n, tn)
    with_gate = wf is not None
    in_specs = [
        pl.BlockSpec((tm, k), lambda i, j: (i, 0), pipeline_mode=pl.Buffered(1)),
        pl.BlockSpec((1, k), lambda i, j: (0, 0)),
        pl.BlockSpec((k, tn), lambda i, j: (0, j)),
    ]
    args = [x, g.reshape(1, k), w]
    out_shape = [jax.ShapeDtypeStruct((m, n), BF16)]
    out_specs = [pl.BlockSpec((tm, tn), lambda i, j: (i, j))]
    blocks = _nbytes((k, tn), BF16) + _nbytes((tm, tn), BF16)
    if with_gate:
        nf = wf.shape[1]
        in_specs.append(pl.BlockSpec((k, nf), lambda i, j: (0, 0)))
        args.append(wf)
        out_shape.append(jax.ShapeDtypeStruct((m, nf), F32))
        out_specs.append(pl.BlockSpec((tm, nf), lambda i, j: (i, 0)))
        blocks += _nbytes((k, nf), BF16) + _nbytes((tm, nf), F32)
    res = pl.pallas_call(
        functools.partial(_norm_matmul_kernel, eps=eps, with_gate=with_gate),
        grid=(m // tm, n // tn),
        in_specs=in_specs,
        out_specs=out_specs,
        out_shape=out_shape,
        scratch_shapes=[pltpu.VMEM((tm, k), BF16)],
        compiler_params=pltpu.CompilerParams(
            dimension_semantics=("parallel", "arbitrary"),
            vmem_limit_bytes=_vmem_limit(
                blocks, _nbytes((tm, k), F32) + _nbytes((tm, k), BF16), 2 * _nbytes((tm, k), F32)),
        ),
        name=name,
    )(*args)
    return res if with_gate else res[0]


def _matmul_resid_kernel(a1_ref, a2_ref, w1_ref, w2_ref, r_ref, o_ref):
    acc = jnp.dot(a1_ref[...], w1_ref[...], preferred_element_type=F32)
    acc = acc + jnp.dot(a2_ref[...], w2_ref[...], preferred_element_type=F32)
    o_ref[...] = r_ref[...] + acc


def _matmul_resid(a1, a2, a_blk, w1, w2, w_blk, resid, *, tm=512, tn=1024, name):
    (a1, a1c), (a2, a2c) = a1, a2
    (w1, w1r), (w2, w2r) = w1, w2
    m, n = resid.shape
    tm = _pick_tile(m, tm)
    tn = _pick_tile(n, tn)
    blocks = (2 * _nbytes((tm, a_blk), BF16) + 2 * _nbytes((w_blk, tn), BF16)
              + 2 * _nbytes((tm, tn), F32))
    return pl.pallas_call(
        _matmul_resid_kernel,
        grid=(m // tm, n // tn),
        in_specs=[
            pl.BlockSpec((tm, a_blk), lambda i, j: (i, a1c)),
            pl.BlockSpec((tm, a_blk), lambda i, j: (i, a2c)),
            pl.BlockSpec((w_blk, tn), lambda i, j: (w1r, j)),
            pl.BlockSpec((w_blk, tn), lambda i, j: (w2r, j)),
            pl.BlockSpec((tm, tn), lambda i, j: (i, j)),
        ],
        out_specs=pl.BlockSpec((tm, tn), lambda i, j: (i, j)),
        out_shape=jax.ShapeDtypeStruct((m, n), F32),
        compiler_params=pltpu.CompilerParams(
            dimension_semantics=("parallel", "arbitrary"),
            vmem_limit_bytes=_vmem_limit(blocks, 0, 2 * _nbytes((tm, tn), F32)),
        ),
        name=name,
    )(a1, a2, w1, w2, resid)


def _gate_cumsum_kernel(f_ref, b_ref, o_ref, *, n_heads):
    z = f_ref[...] + b_ref[...]
    x = jnp.minimum(z, 0.0) - jnp.log1p(jnp.exp(-jnp.abs(z)))
    s_len = x.shape[0]
    row = lax.broadcasted_iota(jnp.int32, x.shape, 0)
    d = 1
    while d < s_len:
        x = x + jnp.where(row >= d, pltpu.roll(x, d, axis=0), 0.0)
        d *= 2
    c2 = x * (-LOG2E)
    hi = c2.astype(BF16)
    r1 = c2 - hi.astype(F32)
    mid = r1.astype(BF16)
    lo = (r1 - mid.astype(F32)).astype(BF16)
    lane = lax.broadcasted_iota(jnp.int32, x.shape, 1)
    mid_s = pltpu.roll(mid.astype(F32), n_heads, axis=1)
    lo_s = pltpu.roll(lo.astype(F32), 2 * n_heads, axis=1)
    out = jnp.where(lane < n_heads, hi.astype(F32),
                    jnp.where(lane < 2 * n_heads, mid_s, jnp.where(lane < 3 * n_heads, lo_s, 0.0)))
    o_ref[...] = out.astype(BF16)


def _gate_cumsum(f_logit, b_pad, batch, seq, n_heads):
    nf = f_logit.shape[1]
    assert 3 * n_heads <= nf
    return pl.pallas_call(
        functools.partial(_gate_cumsum_kernel, n_heads=n_heads),
        grid=(batch,),
        in_specs=[
            pl.BlockSpec((seq, nf), lambda b: (b, 0)),
            pl.BlockSpec((1, nf), lambda b: (0, 0)),
        ],
        out_specs=pl.BlockSpec((seq, nf), lambda b: (b, 0)),
        out_shape=jax.ShapeDtypeStruct((batch * seq, nf), BF16),
        compiler_params=pltpu.CompilerParams(
            dimension_semantics=("parallel",),
            vmem_limit_bytes=_vmem_limit(2 * _nbytes((seq, nf), F32), 0, 8 * _nbytes((seq, nf), F32)),
        ),
        name="gate_cumsum",
    )(f_logit, b_pad)


def _bias_tile_kernel(rb_ref, o_ref, *, tile, n_heads):
    h = pl.program_id(0)
    d = pl.program_id(1)
    half, max_exact, thresholds = _t5_thresholds()
    t = lax.broadcasted_iota(jnp.int32, (tile, tile), 0)
    s = lax.broadcasted_iota(jnp.int32, (tile, tile), 1)
    rel = s - t - d * tile
    n = jnp.abs(rel)
    large = jnp.full((tile, tile), max_exact, jnp.int32)
    for thr in thresholds:
        large = large + (n >= thr).astype(jnp.int32)
    idx = jnp.where(n < max_exact, n, large) + jnp.where(rel > 0, half, 0)
    val = jnp.zeros((tile, tile), F32)
    for b in range(N_BUCKETS):
        val = jnp.where(idx == b, rb_ref[b * n_heads + h], val)
    shift = int(math.log2(CHUNK))
    allowed = (s >> shift) <= ((t >> shift) + d * tile)
    o_ref[0, 0] = jnp.where(allowed, val * LOG2E, NEG_INF)


def _bias_tiles(rel_bias, tile):
    n_heads = rel_bias.shape[1]
    return pl.pallas_call(
        functools.partial(_bias_tile_kernel, tile=tile, n_heads=n_heads),
        grid=(n_heads, 3),
        in_specs=[pl.BlockSpec(memory_space=pltpu.SMEM)],
        out_specs=pl.BlockSpec((1, 1, tile, tile), lambda h, d: (h, d, 0, 0)),
        out_shape=jax.ShapeDtypeStruct((n_heads, 3, tile, tile), F32),
        compiler_params=pltpu.CompilerParams(
            dimension_semantics=("parallel", "parallel"),
            vmem_limit_bytes=_vmem_limit(_nbytes((tile, tile), F32), 0, 8 * _nbytes((tile, tile), F32)),
        ),
        name="t5_bias_tiles",
    )(rel_bias.reshape(-1))


def _lane_tile(x, reps):
    return x if reps == 1 else jnp.concatenate([x] * reps, axis=1)


def _qk(q, k):
    return lax.dot_general(q, k, (((1,), (1,)), ((), ())), preferred_element_type=F32)


def _pipelined_key_blocks(i, score, softmax, softmax_diag, pv):
    score(0, 0)

    @pl.when(i == 0)
    def _():
        softmax_diag(0, 0)
        pv(0, 0)

    @pl.when(i > 0)
    def _():
        score(1, 1)
        softmax(0, 0)

        def pair(t, carry):
            j = 2 * t
            score(j + 2, 0)
            softmax(j + 1, 1)
            pv(j, 0)
            score(j + 3, 1)
            softmax(j + 2, 0)
            pv(j + 1, 1)
            return carry

        lax.fori_loop(0, (i - 1) >> 1, pair, 0)

        @pl.when((i & 1) == 1)
        def _():
            softmax_diag(i, 1)
            pv(i - 1, 0)
            pv(i, 1)

        @pl.when((i & 1) == 0)
        def _():
            score(i, 0)
            softmax(i - 1, 1)
            pv(i - 2, 0)
            softmax_diag(i, 0)
            pv(i - 1, 1)
            pv(i, 0)


def _fox_kernel(q_ref, k_ref, v_ref, cx_ref, o_ref, kx_ref, vx_ref, s0_ref, s1_ref, p0_ref, p1_ref,
                a0_ref, a1_ref, m_ref, acc_ref, *, tile, scale, n_heads):
    s_bufs, p_bufs, a_bufs = (s0_ref, s1_ref), (p0_ref, p1_ref), (a0_ref, a1_ref)
    h = pl.program_id(1)
    i = pl.program_id(2)
    hd = HEAD_DIM

    @pl.when(i == 0)
    def _():
        kx_ref[:, :hd] = k_ref[...]
        kx_ref[:, hd:] = cx_ref[...]
        vx_ref[:, :hd] = v_ref[...]
        vx_ref[:, hd:] = jnp.ones((vx_ref.shape[0], vx_ref.shape[1] - hd), BF16)

    lane = lax.broadcasted_iota(jnp.int32, (tile, cx_ref.shape[1]), 1)
    sel = (lane == h) | (lane == h + n_heads) | (lane == h + 2 * n_heads)
    qx = jnp.concatenate([(q_ref[...].astype(F32) * (scale * LOG2E)).astype(BF16),
                          jnp.where(sel, 1.0, 0.0).astype(BF16)], axis=1)
    m_ref[...] = jnp.full(m_ref.shape, NEG_INF, F32)
    acc_ref[...] = jnp.zeros(acc_ref.shape, F32)

    def score_stage(j, slot):
        k0 = pl.multiple_of(j * tile, tile)
        s_bufs[slot][...] = _qk(qx, kx_ref[pl.ds(k0, tile), :])

    def softmax_stage(j, slot, masked):
        s = s_bufs[slot][...]
        if masked:
            row = lax.broadcasted_iota(jnp.int32, s.shape, 0)
            col = lax.broadcasted_iota(jnp.int32, s.shape, 1)
            s = jnp.where(col <= row, s, NEG_INF)
        m_prev = m_ref[...]
        m_new = jnp.maximum(m_prev, jnp.max(s, axis=1, keepdims=True))
        a_bufs[slot][...] = jnp.exp2(m_prev - m_new)
        p_bufs[slot][...] = jnp.exp2(s - _lane_tile(m_new, tile // V7X_LANES)).astype(BF16)
        m_ref[...] = m_new

    def pv_stage(j, slot):
        k0 = pl.multiple_of(j * tile, tile)
        pv = jnp.dot(p_bufs[slot][...], vx_ref[pl.ds(k0, tile), :], preferred_element_type=F32)
        acc_ref[...] = _lane_tile(a_bufs[slot][...], acc_ref.shape[1] // V7X_LANES) * acc_ref[...] + pv

    _pipelined_key_blocks(i, score_stage, functools.partial(softmax_stage, masked=False),
                          functools.partial(softmax_stage, masked=True), pv_stage)
    acc = acc_ref[...]
    o_ref[...] = (acc[:, :hd] / acc[:, hd:2 * hd]).astype(o_ref.dtype)


def _fox_attention(proj, cx, batch, seq, n_heads, tile):
    nq = seq // tile
    hd = HEAD_DIM
    nx = cx.shape[1]
    blocks = 2 * _nbytes((tile, hd), BF16) + 2 * _nbytes((seq, hd), BF16) + _nbytes((seq, nx), BF16)
    scratch = (_nbytes((seq, hd + nx), BF16) + _nbytes((seq, 2 * hd), BF16) + 2 * _nbytes((tile, tile), F32)
               + 2 * _nbytes((tile, tile), BF16) + 4 * _nbytes((tile, 2 * hd), F32))
    return pl.pallas_call(
        functools.partial(_fox_kernel, tile=tile, scale=hd ** -0.5, n_heads=n_heads),
        grid=(batch, n_heads, nq),
        in_specs=[
            pl.BlockSpec((tile, hd), lambda b, h, i: (b * nq + i, h)),
            pl.BlockSpec((seq, hd), lambda b, h, i: (b, n_heads + h)),
            pl.BlockSpec((seq, hd), lambda b, h, i: (b, 2 * n_heads + h)),
            pl.BlockSpec((seq, nx), lambda b, h, i: (b, 0)),
        ],
        out_specs=pl.BlockSpec((tile, hd), lambda b, h, i: (b * nq + i, h)),
        out_shape=jax.ShapeDtypeStruct((batch * seq, n_heads * hd), BF16),
        scratch_shapes=[
            pltpu.VMEM((seq, hd + nx), BF16),
            pltpu.VMEM((seq, 2 * hd), BF16),
            pltpu.VMEM((tile, tile), F32), pltpu.VMEM((tile, tile), F32),
            pltpu.VMEM((tile, tile), BF16), pltpu.VMEM((tile, tile), BF16),
            pltpu.VMEM((tile, V7X_LANES), F32), pltpu.VMEM((tile, V7X_LANES), F32),
            pltpu.VMEM((tile, V7X_LANES), F32),
            pltpu.VMEM((tile, 2 * hd), F32),
        ],
        compiler_params=pltpu.CompilerParams(
            dimension_semantics=("parallel", "parallel", "arbitrary"),
            vmem_limit_bytes=_vmem_limit(blocks, scratch, 4 * _nbytes((tile, tile), F32)),
        ),
        name="fox_attention",
    )(proj, proj, proj, cx)


def _diff_kernel(q_ref, k_ref, v_ref, bias_ref, lq1_ref, lk1_ref, lq2_ref, lk2_ref, g_ref, o_ref,
                 s0_ref, s1_ref, p0_ref, p1_ref, a0_ref, a1_ref, m_ref, l_ref, acc_ref,
                 *, tile, scale, lambda_init):
    s_bufs, p_bufs, a_bufs = (s0_ref, s1_ref), (p0_ref, p1_ref), (a0_ref, a1_ref)
    i = pl.program_id(2)
    hd = HEAD_DIM
    qf = q_ref[...].astype(F32) * (scale * LOG2E)
    lane = lax.broadcasted_iota(jnp.int32, qf.shape, 1)
    qq = jnp.concatenate([jnp.where(lane < hd, qf, 0.0).astype(BF16),
                          jnp.where(lane >= hd, qf, 0.0).astype(BF16)], axis=0)
    m_ref[...] = jnp.full(m_ref.shape, NEG_INF, F32)
    l_ref[...] = jnp.zeros(l_ref.shape, F32)
    acc_ref[...] = jnp.zeros(acc_ref.shape, F32)

    def score_stage(j, slot):
        k0 = pl.multiple_of(j * tile, tile)
        s_bufs[slot][...] = _qk(qq, k_ref[pl.ds(k0, tile), :])

    def softmax_stage(j, slot):
        bias = bias_ref[0, jnp.minimum(i - j, 2)]
        for c in range(2):
            rows = slice(c * tile, (c + 1) * tile)
            s = s_bufs[slot][rows, :] + bias
            m_prev = m_ref[rows, :]
            m_new = jnp.maximum(m_prev, jnp.max(s, axis=1, keepdims=True))
            alpha = jnp.exp2(m_prev - m_new)
            p = jnp.exp2(s - _lane_tile(m_new, tile // V7X_LANES))
            l_ref[rows, :] = alpha * l_ref[rows, :] + jnp.sum(p, axis=1, keepdims=True)
            a_bufs[slot][rows, :] = alpha
            p_bufs[slot][rows, :] = p.astype(BF16)
            m_ref[rows, :] = m_new

    def pv_stage(j, slot):
        k0 = pl.multiple_of(j * tile, tile)
        pv = jnp.dot(p_bufs[slot][...], v_ref[pl.ds(k0, tile), :], preferred_element_type=F32)
        acc_ref[...] = _lane_tile(a_bufs[slot][...], acc_ref.shape[1] // V7X_LANES) * acc_ref[...] + pv

    _pipelined_key_blocks(i, score_stage, softmax_stage, softmax_stage, pv_stage)

    lam = (jnp.exp(jnp.sum(lq1_ref[...] * lk1_ref[...], axis=1, keepdims=True))
           - jnp.exp(jnp.sum(lq2_ref[...] * lk2_ref[...], axis=1, keepdims=True))
           + lambda_init)
    reps = acc_ref.shape[1] // V7X_LANES
    out = (acc_ref[:tile, :] / _lane_tile(l_ref[:tile, :], reps)
           - lam * (acc_ref[tile:, :] / _lane_tile(l_ref[tile:, :], reps)))
    ms = jnp.mean(out * out, axis=-1, keepdims=True)
    y = out * lax.rsqrt(ms + SUBLN_EPS) * g_ref[...]
    o_ref[...] = (y * (1.0 - lambda_init)).astype(o_ref.dtype)


def _diff_attention(proj, bias, lq1, lk1, lq2, lk2, g_subln, batch, seq, n_heads, col0, tile, lambda_init):
    nq = seq // tile
    hd2 = 2 * HEAD_DIM
    c0 = col0 // hd2
    vec = lambda: pl.BlockSpec((1, HEAD_DIM), lambda b, h, i: (0, 0))
    blocks = (2 * _nbytes((tile, hd2), BF16) + 2 * _nbytes((seq, hd2), BF16)
              + _nbytes((3, tile, tile), F32))
    return pl.pallas_call(
        functools.partial(_diff_kernel, tile=tile, scale=HEAD_DIM ** -0.5, lambda_init=lambda_init),
        grid=(batch, n_heads, nq),
        in_specs=[
            pl.BlockSpec((tile, hd2), lambda b, h, i: (b * nq + i, c0 + h)),
            pl.BlockSpec((seq, hd2), lambda b, h, i: (b, c0 + n_heads + h)),
            pl.BlockSpec((seq, hd2), lambda b, h, i: (b, c0 + 2 * n_heads + h)),
            pl.BlockSpec((1, 3, tile, tile), lambda b, h, i: (h, 0, 0, 0)),
            vec(), vec(), vec(), vec(),
            pl.BlockSpec((1, hd2), lambda b, h, i: (0, 0)),
        ],
        out_specs=pl.BlockSpec((tile, hd2), lambda b, h, i: (b * nq + i, h)),
        out_shape=jax.ShapeDtypeStruct((batch * seq, n_heads * hd2), BF16),
        scratch_shapes=[
            pltpu.VMEM((2 * tile, tile), F32), pltpu.VMEM((2 * tile, tile), F32),
            pltpu.VMEM((2 * tile, tile), BF16), pltpu.VMEM((2 * tile, tile), BF16),
            pltpu.VMEM((2 * tile, V7X_LANES), F32), pltpu.VMEM((2 * tile, V7X_LANES), F32),
            pltpu.VMEM((2 * tile, V7X_LANES), F32),
            pltpu.VMEM((2 * tile, V7X_LANES), F32),
            pltpu.VMEM((2 * tile, hd2), F32),
        ],
        compiler_params=pltpu.CompilerParams(
            dimension_semantics=("parallel", "parallel", "arbitrary"),
            vmem_limit_bytes=_vmem_limit(
                blocks, 6 * _nbytes((tile, tile), F32) + 6 * _nbytes((tile, hd2), F32),
                6 * _nbytes((tile, tile), F32)),
        ),
        name="diff_attention",
    )(proj, proj, proj, bias, lq1, lk1, lq2, lk2, g_subln)


def _cross_kernel(q_ref, k_ref, v_ref, o_ref, *, n_heads, scale):
    dh = q_ref.shape[1] // n_heads
    for h in range(n_heads):
        cols = slice(h * dh, (h + 1) * dh)
        s = _qk(q_ref[:, cols], k_ref[:, cols]) * scale
        m = jnp.max(s, axis=1, keepdims=True)
        p = jnp.exp(s - m)
        p = p / jnp.sum(p, axis=1, keepdims=True)
        o_ref[:, cols] = jnp.dot(p.astype(BF16), v_ref[:, cols], preferred_element_type=F32).astype(o_ref.dtype)


def _cross_attention(q, kv, batch, seq, n_mem, tile):
    d = q.shape[1]
    nq = seq // tile
    blocks = 2 * _nbytes((tile, d), BF16) + 2 * _nbytes((n_mem, d), BF16)
    return pl.pallas_call(
        functools.partial(_cross_kernel, n_heads=H_MEM, scale=(d // H_MEM) ** -0.5),
        grid=(batch, nq),
        in_specs=[
            pl.BlockSpec((tile, d), lambda b, i: (b * nq + i, 0)),
            pl.BlockSpec((n_mem, d), lambda b, i: (b, 0)),
            pl.BlockSpec((n_mem, d), lambda b, i: (b, 1)),
        ],
        out_specs=pl.BlockSpec((tile, d), lambda b, i: (b * nq + i, 0)),
        out_shape=jax.ShapeDtypeStruct((batch * seq, d), BF16),
        compiler_params=pltpu.CompilerParams(
            dimension_semantics=("parallel", "arbitrary"),
            vmem_limit_bytes=_vmem_limit(blocks, 0, 8 * _nbytes((tile, n_mem), F32) + _nbytes((tile, d), F32)),
        ),
        name="cross_attention",
    )(q, kv, kv)


def _mlp_kernel(h_ref, g_ref, wu_ref, wd_ref, gf_ref, o_ref, a_ref):
    f = pl.program_id(1)

    @pl.when(f == 0)
    def _():
        x = h_ref[...]
        ms = jnp.mean(x * x, axis=-1, keepdims=True)
        a_ref[...] = (x * lax.rsqrt(ms + NORM_EPS) * g_ref[...]).astype(BF16)
        o_ref[...] = x

    u = jnp.dot(a_ref[...], wu_ref[...], preferred_element_type=F32)
    act = jnp.square(jnp.maximum(u, 0.0)).astype(BF16)
    o_ref[...] += jnp.dot(act, wd_ref[...], preferred_element_type=F32)

    @pl.when(f == pl.num_programs(1) - 1)
    def _():
        y = o_ref[...]
        ms = jnp.mean(y * y, axis=-1, keepdims=True)
        o_ref[...] = y * lax.rsqrt(ms + NORM_EPS) * gf_ref[...]


def _mlp(h, g_mlp, w_up, w_down, g_final, *, tm=512, tf=512):
    m, d = h.shape
    dff = w_up.shape[1]
    tm = _pick_tile(m, tm)
    tf = _pick_tile(dff, tf)
    blocks = 2 * _nbytes((d, tf), BF16) + _nbytes((tm, d), F32)
    return pl.pallas_call(
        _mlp_kernel,
        grid=(m // tm, dff // tf),
        in_specs=[
            pl.BlockSpec((tm, d), lambda i, f: (i, 0), pipeline_mode=pl.Buffered(1)),
            pl.BlockSpec((1, d), lambda i, f: (0, 0)),
            pl.BlockSpec((d, tf), lambda i, f: (0, f)),
            pl.BlockSpec((tf, d), lambda i, f: (f, 0)),
            pl.BlockSpec((1, d), lambda i, f: (0, 0)),
        ],
        out_specs=pl.BlockSpec((tm, d), lambda i, f: (i, 0)),
        out_shape=jax.ShapeDtypeStruct((m, d), F32),
        scratch_shapes=[pltpu.VMEM((tm, d), BF16)],
        compiler_params=pltpu.CompilerParams(
            dimension_semantics=("parallel", "arbitrary"),
            vmem_limit_bytes=_vmem_limit(
                blocks, _nbytes((tm, d), F32) + _nbytes((tm, d), BF16),
                2 * _nbytes((tm, tf), F32) + _nbytes((tm, d), F32)),
        ),
        name="mlp_final_norm",
    )(h, g_mlp.reshape(1, d), w_up, w_down, g_final.reshape(1, d))


def _attn_tile(seq):
    return _pick_tile(seq, 512)


def _layer(h, mem, l, g_mix, w_in, b_forget, lambda_q1, lambda_k1, lambda_q2, lambda_k2, g_subln, rel_bias,
           w_out, g_cross, g_mem, wq_mem, wk_mem, wv_mem, wo_mem, g_mlp, w_up, w_down, g_final, final):
    batch, seq, d = h.shape
    n_mem = mem.shape[1]
    h_fox = b_forget.shape[-1]
    h_diff = rel_bias.shape[1]
    w_fox = h_fox * HEAD_DIM
    w_diff = h_diff * 2 * HEAD_DIM
    tile = _attn_tile(seq)
    x2 = h.reshape(batch * seq, d)

    wi = w_in[l]
    f0 = 3 * w_fox
    w_qkv = jnp.concatenate([wi[:, :f0], wi[:, f0 + h_fox:]], axis=1).astype(BF16)
    n_gate = -(-h_fox // V7X_LANES) * V7X_LANES
    w_gate = jnp.pad(wi[:, f0:f0 + h_fox], ((0, 0), (0, n_gate - h_fox))).astype(BF16)
    b_gate = jnp.pad(b_forget[l].astype(F32), (0, n_gate - h_fox)).reshape(1, n_gate)

    proj, f_logit = _norm_matmul(x2, g_mix[l], w_qkv, w_gate, name="in_proj")

    cx = _gate_cumsum(f_logit, b_gate, batch, seq, h_fox)
    fox = _fox_attention(proj, cx, batch, seq, h_fox, tile)

    lambda_init = 0.8 - 0.6 * math.exp(-0.3 * l)
    bias = _bias_tiles(rel_bias.astype(F32), tile)
    row = lambda v: v[l].astype(F32).reshape(1, -1)
    diff = _diff_attention(proj, bias, row(lambda_q1), row(lambda_k1), row(lambda_q2), row(lambda_k2),
                           row(g_subln), batch, seq, h_diff, 3 * w_fox, tile, lambda_init)

    w_o = w_out[l].astype(BF16)
    h1 = _matmul_resid((fox, 0), (diff, 0), w_fox, (w_o, 0), (w_o, 1), w_fox, x2, name="out_proj")

    q = _norm_matmul(h1, g_cross[l], wq_mem[l].astype(BF16), name="cross_q_proj")
    w_kv = jnp.concatenate([wk_mem[l], wv_mem[l]], axis=1).astype(BF16)
    kv = _norm_matmul(mem.reshape(batch * n_mem, d), g_mem[l], w_kv, name="cross_kv_proj")
    o = _cross_attention(q, kv, batch, seq, n_mem, tile)
    w_om = wo_mem[l].astype(BF16)
    half = d // 2
    h2 = _matmul_resid((o, 0), (o, 1), half, (w_om, 0), (w_om, 1), half, h1, name="cross_o_proj")

    gf = g_final if final else jnp.ones((d,), F32)
    out = _mlp(h2, g_mlp[l], w_up[l].astype(BF16), w_down[l].astype(BF16), gf)
    return out.reshape(batch, seq, d)


def kernel(x, mem, g_mix, w_in, b_forget, lambda_q1, lambda_k1, lambda_q2, lambda_k2, g_subln, rel_bias, w_out,
           g_cross, g_mem, wq_mem, wk_mem, wv_mem, wo_mem, g_mlp, w_up, w_down, g_final):
    depth = g_mix.shape[0]
    assert depth == 1, "the fused MLP epilogue applies the final norm; only depth 1 is supported"
    return _layer(x, mem, 0, g_mix, w_in, b_forget, lambda_q1, lambda_k1, lambda_q2, lambda_k2, g_subln,
                  rel_bias, w_out, g_cross, g_mem, wq_mem, wk_mem, wv_mem, wo_mem, g_mlp, w_up, w_down,
                  g_final, True)
```

```python
import functools
import math

import jax
import jax.numpy as jnp
from jax import lax
from jax.experimental import pallas as pl
from jax.experimental.pallas import tpu as pltpu

HEAD_DIM = 128
CHUNK = 64
N_BUCKETS = 32
MAX_DISTANCE = 128
H_MEM = 4
NORM_EPS = 1e-6
SUBLN_EPS = 1e-5
NEG_INF = -1e30
LOG2E = math.log2(math.e)

V7X_LANES = 128
V7X_VMEM_BYTES = 64 * 1024 * 1024
V7X_VMEM_REQUEST_CAP = V7X_VMEM_BYTES - 8 * 1024 * 1024

F32 = jnp.float32
BF16 = jnp.bfloat16


def _vmem_limit(block_bytes, scratch_bytes=0, temp_bytes=0):
    need = 2 * block_bytes + scratch_bytes + temp_bytes + (4 << 20)
    return int(min(max(need, 16 << 20), V7X_VMEM_REQUEST_CAP))


def _nbytes(shape, dtype):
    return math.prod(shape) * jnp.dtype(dtype).itemsize


def _pick_tile(n, target):
    if n <= target:
        return n
    t = target
    while t >= V7X_LANES:
        if n % t == 0:
            return t
        t -= V7X_LANES
    return n


def _t5_thresholds():
    half = N_BUCKETS // 2
    max_exact = half // 2
    steps = half - max_exact
    ratio = MAX_DISTANCE // max_exact
    thr = []
    for k in range(1, steps):
        n = max_exact
        while n ** steps < (ratio ** k) * (max_exact ** steps):
            n += 1
        thr.append(n)
    return half, max_exact, tuple(thr)


def _norm_matmul_kernel(x_ref, g_ref, w_ref, *rest, eps, with_gate):
    if with_gate:
        wf_ref, o_ref, f_ref, a_ref = rest
    else:
        o_ref, a_ref = rest

    @pl.when(pl.program_id(1) == 0)
    def _():
        x = x_ref[...]
        ms = jnp.mean(x * x, axis=-1, keepdims=True)
        a = (x * lax.rsqrt(ms + eps) * g_ref[...]).astype(BF16)
        a_ref[...] = a
        if with_gate:
            f_ref[...] = jnp.dot(a, wf_ref[...], preferred_element_type=F32)

    o_ref[...] = jnp.dot(a_ref[...], w_ref[...], preferred_element_type=F32).astype(o_ref.dtype)


def _norm_matmul(x, g, w, wf=None, *, eps=NORM_EPS, tm=512, tn=1024, name):
    m, k = x.shape
    n = w.shape[1]
    tm = _pick_tile(m, tm)
    tn = _pick_tile(n, tn)
    with_gate = wf is not None
    in_specs = [
        pl.BlockSpec((tm, k), lambda i, j: (i, 0), pipeline_mode=pl.Buffered(1)),
        pl.BlockSpec((1, k), lambda i, j: (0, 0)),
        pl.BlockSpec((k, tn), lambda i, j: (0, j)),
    ]
    args = [x, g.reshape(1, k), w]
    out_shape = [jax.ShapeDtypeStruct((m, n), BF16)]
    out_specs = [pl.BlockSpec((tm, tn), lambda i, j: (i, j))]
    blocks = _nbytes((k, tn), BF16) + _nbytes((tm, tn), BF16)
    if with_gate:
        nf = wf.shape[1]
        in_specs.append(pl.BlockSpec((k, nf), lambda i, j: (0, 0)))
        args.append(wf)
        out_shape.append(jax.ShapeDtypeStruct((m, nf), F32))
        out_specs.append(pl.BlockSpec((tm, nf), lambda i, j: (i, 0)))
        blocks += _nbytes((k, nf), BF16) + _nbytes((tm, nf), F32)
    res = pl.pallas_call(
        functools.partial(_norm_matmul_kernel, eps=eps, with_gate=with_gate),
        grid=(m // tm, n // tn),
        in_specs=in_specs,
        out_specs=out_specs,
        out_shape=out_shape,
        scratch_shapes=[pltpu.VMEM((tm, k), BF16)],
        compiler_params=pltpu.CompilerParams(
            dimension_semantics=("parallel", "arbitrary"),
            vmem_limit_bytes=_vmem_limit(
                blocks, _nbytes((tm, k), F32) + _nbytes((tm, k), BF16), 2 * _nbytes((tm, k), F32)),
        ),
        name=name,
    )(*args)
    return res if with_gate else res[0]


def _row_scale(ss_ref, k, eps):
    ss = ss_ref[:, :V7X_LANES]
    for c in range(1, ss_ref.shape[1] // V7X_LANES):
        ss = ss + ss_ref[:, c * V7X_LANES:(c + 1) * V7X_LANES]
    return lax.rsqrt(ss * (1.0 / k) + eps)


def _in_proj_kernel(x_ref, w1_ref, w2_ref, wf_ref, o_ref, f_ref, xb_ref, r_ref, *, eps, n1):
    j = pl.program_id(1)

    @pl.when(j == 0)
    def _():
        x = x_ref[...]
        r = lax.rsqrt(jnp.mean(x * x, axis=-1, keepdims=True) + eps)
        r_ref[...] = jnp.broadcast_to(r, r_ref.shape)
        xb = x.astype(BF16)
        xb_ref[...] = xb
        f_ref[...] = r * jnp.dot(xb, wf_ref[...], preferred_element_type=F32)

    def project(w_ref):
        acc = jnp.dot(xb_ref[...], w_ref[...], preferred_element_type=F32)
        o_ref[...] = (_lane_tile(r_ref[...], o_ref.shape[1] // V7X_LANES) * acc).astype(o_ref.dtype)

    @pl.when(j < n1)
    def _():
        project(w1_ref)

    @pl.when(j >= n1)
    def _():
        project(w2_ref)


def _in_proj(x, w1, w2, wf, *, eps=NORM_EPS, tm=512, tn=1024):
    m, k = x.shape
    tm = _pick_tile(m, tm)
    tn = _pick_tile(math.gcd(w1.shape[1], w2.shape[1]), tn)
    n1, n2 = w1.shape[1] // tn, w2.shape[1] // tn
    nf = wf.shape[1]
    blocks = (2 * _nbytes((k, tn), BF16) + _nbytes((tm, tn), BF16) + _nbytes((k, nf), BF16)
              + _nbytes((tm, nf), F32))
    return pl.pallas_call(
        functools.partial(_in_proj_kernel, eps=eps, n1=n1),
        grid=(m // tm, n1 + n2),
        in_specs=[
            pl.BlockSpec((tm, k), lambda i, j: (i, 0), pipeline_mode=pl.Buffered(1)),
            pl.BlockSpec((k, tn), lambda i, j: (0, jnp.minimum(j, n1 - 1))),
            pl.BlockSpec((k, tn), lambda i, j: (0, jnp.maximum(j - n1, 0))),
            pl.BlockSpec((k, nf), lambda i, j: (0, 0)),
        ],
        out_specs=[
            pl.BlockSpec((tm, tn), lambda i, j: (i, j)),
            pl.BlockSpec((tm, nf), lambda i, j: (i, 0)),
        ],
        out_shape=[
            jax.ShapeDtypeStruct((m, (n1 + n2) * tn), BF16),
            jax.ShapeDtypeStruct((m, nf), F32),
        ],
        scratch_shapes=[pltpu.VMEM((tm, k), BF16), pltpu.VMEM((tm, V7X_LANES), F32)],
        compiler_params=pltpu.CompilerParams(
            dimension_semantics=("parallel", "arbitrary"),
            vmem_limit_bytes=_vmem_limit(
                blocks, _nbytes((tm, k), F32) + _nbytes((tm, k), BF16), _nbytes((tm, k), F32)),
        ),
        name="in_proj",
    )(x, w1, w2, wf)


def _scaled_matmul_kernel(a_ref, ss_ref, w_ref, o_ref, *, eps):
    r = _row_scale(ss_ref, a_ref.shape[1], eps)
    acc = jnp.dot(a_ref[...], w_ref[...], preferred_element_type=F32)
    o_ref[...] = (_lane_tile(r, o_ref.shape[1] // V7X_LANES) * acc).astype(o_ref.dtype)


def _scaled_matmul(a, ss, w, *, eps=NORM_EPS, tm=512, tn=1024, name):
    m, k = a.shape
    n = w.shape[1]
    tm = _pick_tile(m, tm)
    tn = _pick_tile(n, tn)
    nss = ss.shape[1]
    blocks = (_nbytes((tm, k), BF16) + _nbytes((tm, nss), F32) + _nbytes((k, tn), BF16)
              + _nbytes((tm, tn), BF16))
    return pl.pallas_call(
        functools.partial(_scaled_matmul_kernel, eps=eps),
        grid=(m // tm, n // tn),
        in_specs=[
            pl.BlockSpec((tm, k), lambda i, j: (i, 0)),
            pl.BlockSpec((tm, nss), lambda i, j: (i, 0)),
            pl.BlockSpec((k, tn), lambda i, j: (0, j)),
        ],
        out_specs=pl.BlockSpec((tm, tn), lambda i, j: (i, j)),
        out_shape=jax.ShapeDtypeStruct((m, n), BF16),
        compiler_params=pltpu.CompilerParams(
            dimension_semantics=("parallel", "arbitrary"),
            vmem_limit_bytes=_vmem_limit(blocks, 0, 2 * _nbytes((tm, tn), F32)),
        ),
        name=name,
    )(a, ss, w)


def _matmul_resid_kernel(a1_ref, a2_ref, w1_ref, w2_ref, r_ref, o_ref, ob_ref, ss_ref):
    acc = jnp.dot(a1_ref[...], w1_ref[...], preferred_element_type=F32)
    acc = acc + jnp.dot(a2_ref[...], w2_ref[...], preferred_element_type=F32)
    h = r_ref[...] + acc
    o_ref[...] = h
    ob_ref[...] = h.astype(BF16)
    ss_ref[...] = jnp.broadcast_to(jnp.sum(h * h, axis=1, keepdims=True), ss_ref.shape)


def _matmul_resid(a1, a2, a_blk, w1, w2, w_blk, resid, *, tm=512, tn=1024, name):
    (a1, a1c), (a2, a2c) = a1, a2
    (w1, w1r), (w2, w2r) = w1, w2
    m, n = resid.shape
    tm = _pick_tile(m, tm)
    tn = _pick_tile(n, tn)
    blocks = (2 * _nbytes((tm, a_blk), BF16) + 2 * _nbytes((w_blk, tn), BF16)
              + 2 * _nbytes((tm, tn), F32) + _nbytes((tm, tn), BF16) + _nbytes((tm, V7X_LANES), F32))
    return pl.pallas_call(
        _matmul_resid_kernel,
        grid=(m // tm, n // tn),
        in_specs=[
            pl.BlockSpec((tm, a_blk), lambda i, j: (i, a1c)),
            pl.BlockSpec((tm, a_blk), lambda i, j: (i, a2c)),
            pl.BlockSpec((w_blk, tn), lambda i, j: (w1r, j)),
            pl.BlockSpec((w_blk, tn), lambda i, j: (w2r, j)),
            pl.BlockSpec((tm, tn), lambda i, j: (i, j)),
        ],
        out_specs=[
            pl.BlockSpec((tm, tn), lambda i, j: (i, j)),
            pl.BlockSpec((tm, tn), lambda i, j: (i, j)),
            pl.BlockSpec((tm, V7X_LANES), lambda i, j: (i, j)),
        ],
        out_shape=[
            jax.ShapeDtypeStruct((m, n), F32),
            jax.ShapeDtypeStruct((m, n), BF16),
            jax.ShapeDtypeStruct((m, (n // tn) * V7X_LANES), F32),
        ],
        compiler_params=pltpu.CompilerParams(
            dimension_semantics=("parallel", "arbitrary"),
            vmem_limit_bytes=_vmem_limit(blocks, 0, 3 * _nbytes((tm, tn), F32)),
        ),
        name=name,
    )(a1, a2, w1, w2, resid)


def _gate_cumsum_kernel(f_ref, b_ref, o_ref, *, n_heads):
    z = f_ref[...] + b_ref[...]
    x = jnp.minimum(z, 0.0) - jnp.log1p(jnp.exp(-jnp.abs(z)))
    s_len = x.shape[0]
    row = lax.broadcasted_iota(jnp.int32, x.shape, 0)
    d = 1
    while d < s_len:
        x = x + jnp.where(row >= d, pltpu.roll(x, d, axis=0), 0.0)
        d *= 2
    c2 = x * (-LOG2E)
    hi = c2.astype(BF16)
    r1 = c2 - hi.astype(F32)
    mid = r1.astype(BF16)
    lo = (r1 - mid.astype(F32)).astype(BF16)
    lane = lax.broadcasted_iota(jnp.int32, x.shape, 1)
    mid_s = pltpu.roll(mid.astype(F32), n_heads, axis=1)
    lo_s = pltpu.roll(lo.astype(F32), 2 * n_heads, axis=1)
    out = jnp.where(lane < n_heads, hi.astype(F32),
                    jnp.where(lane < 2 * n_heads, mid_s, jnp.where(lane < 3 * n_heads, lo_s, 0.0)))
    o_ref[...] = out.astype(BF16)


def _gate_cumsum(f_logit, b_pad, batch, seq, n_heads):
    nf = f_logit.shape[1]
    assert 3 * n_heads <= nf
    return pl.pallas_call(
        functools.partial(_gate_cumsum_kernel, n_heads=n_heads),
        grid=(batch,),
        in_specs=[
            pl.BlockSpec((seq, nf), lambda b: (b, 0)),
            pl.BlockSpec((1, nf), lambda b: (0, 0)),
        ],
        out_specs=pl.BlockSpec((seq, nf), lambda b: (b, 0)),
        out_shape=jax.ShapeDtypeStruct((batch * seq, nf), BF16),
        compiler_params=pltpu.CompilerParams(
            dimension_semantics=("parallel",),
            vmem_limit_bytes=_vmem_limit(2 * _nbytes((seq, nf), F32), 0, 8 * _nbytes((seq, nf), F32)),
        ),
        name="gate_cumsum",
    )(f_logit, b_pad)


def _bias_tile_kernel(rb_ref, o_ref, *, tile, n_heads):
    h = pl.program_id(0)
    d = pl.program_id(1)
    half, max_exact, thresholds = _t5_thresholds()
    t = lax.broadcasted_iota(jnp.int32, (tile, tile), 0)
    s = lax.broadcasted_iota(jnp.int32, (tile, tile), 1)
    rel = s - t - d * tile
    n = jnp.abs(rel)
    large = jnp.full((tile, tile), max_exact, jnp.int32)
    for thr in thresholds:
        large = large + (n >= thr).astype(jnp.int32)
    idx = jnp.where(n < max_exact, n, large) + jnp.where(rel > 0, half, 0)
    val = jnp.zeros((tile, tile), F32)
    for b in range(N_BUCKETS):
        val = jnp.where(idx == b, rb_ref[b * n_heads + h], val)
    shift = int(math.log2(CHUNK))
    allowed = (s >> shift) <= ((t >> shift) + d * tile)
    o_ref[0, 0] = jnp.where(allowed, val * LOG2E, NEG_INF)


def _bias_tiles(rel_bias, tile):
    n_heads = rel_bias.shape[1]
    return pl.pallas_call(
        functools.partial(_bias_tile_kernel, tile=tile, n_heads=n_heads),
        grid=(n_heads, 3),
        in_specs=[pl.BlockSpec(memory_space=pltpu.SMEM)],
        out_specs=pl.BlockSpec((1, 1, tile, tile), lambda h, d: (h, d, 0, 0)),
        out_shape=jax.ShapeDtypeStruct((n_heads, 3, tile, tile), F32),
        compiler_params=pltpu.CompilerParams(
            dimension_semantics=("parallel", "parallel"),
            vmem_limit_bytes=_vmem_limit(_nbytes((tile, tile), F32), 0, 8 * _nbytes((tile, tile), F32)),
        ),
        name="t5_bias_tiles",
    )(rel_bias.reshape(-1))


def _lane_tile(x, reps):
    return x if reps == 1 else jnp.concatenate([x] * reps, axis=1)


def _qk(q, k):
    return lax.dot_general(q, k, (((1,), (1,)), ((), ())), preferred_element_type=F32)


def _pipelined_key_blocks(i, score, softmax, softmax_diag, pv):
    score(0, 0)

    @pl.when(i == 0)
    def _():
        softmax_diag(0, 0)
        pv(0, 0)

    @pl.when(i > 0)
    def _():
        score(1, 1)
        softmax(0, 0)

        def pair(t, carry):
            j = 2 * t
            score(j + 2, 0)
            softmax(j + 1, 1)
            pv(j, 0)
            score(j + 3, 1)
            softmax(j + 2, 0)
            pv(j + 1, 1)
            return carry

        lax.fori_loop(0, (i - 1) >> 1, pair, 0)

        @pl.when((i & 1) == 1)
        def _():
            softmax_diag(i, 1)
            pv(i - 1, 0)
            pv(i, 1)

        @pl.when((i & 1) == 0)
        def _():
            score(i, 0)
            softmax(i - 1, 1)
            pv(i - 2, 0)
            softmax_diag(i, 0)
            pv(i - 1, 1)
            pv(i, 0)


def _fox_kernel(q_ref, k_ref, v_ref, cx_ref, o_ref, kx_ref, vx_ref, s0_ref, s1_ref, p0_ref, p1_ref,
                a0_ref, a1_ref, m_ref, acc_ref, *, tile, scale, n_heads):
    s_bufs, p_bufs, a_bufs = (s0_ref, s1_ref), (p0_ref, p1_ref), (a0_ref, a1_ref)
    h = pl.program_id(1)
    i = pl.program_id(2)
    hd = HEAD_DIM

    @pl.when(i == 0)
    def _():
        kx_ref[:, :hd] = k_ref[...]
        kx_ref[:, hd:] = cx_ref[...]
        vx_ref[:, :hd] = v_ref[...]
        vx_ref[:, hd:] = jnp.ones((vx_ref.shape[0], vx_ref.shape[1] - hd), BF16)

    lane = lax.broadcasted_iota(jnp.int32, (tile, cx_ref.shape[1]), 1)
    sel = (lane == h) | (lane == h + n_heads) | (lane == h + 2 * n_heads)
    qx = jnp.concatenate([(q_ref[...].astype(F32) * (scale * LOG2E)).astype(BF16),
                          jnp.where(sel, 1.0, 0.0).astype(BF16)], axis=1)
    m_ref[...] = jnp.full(m_ref.shape, NEG_INF, F32)
    acc_ref[...] = jnp.zeros(acc_ref.shape, F32)

    def score_stage(j, slot):
        k0 = pl.multiple_of(j * tile, tile)
        s_bufs[slot][...] = _qk(qx, kx_ref[pl.ds(k0, tile), :])

    def softmax_stage(j, slot, masked):
        s = s_bufs[slot][...]
        if masked:
            row = lax.broadcasted_iota(jnp.int32, s.shape, 0)
            col = lax.broadcasted_iota(jnp.int32, s.shape, 1)
            s = jnp.where(col <= row, s, NEG_INF)
        m_prev = m_ref[...]
        m_new = jnp.maximum(m_prev, jnp.max(s, axis=1, keepdims=True))
        a_bufs[slot][...] = jnp.exp2(m_prev - m_new)
        p_bufs[slot][...] = jnp.exp2(s - _lane_tile(m_new, tile // V7X_LANES)).astype(BF16)
        m_ref[...] = m_new

    def pv_stage(j, slot):
        k0 = pl.multiple_of(j * tile, tile)
        pv = jnp.dot(p_bufs[slot][...], vx_ref[pl.ds(k0, tile), :], preferred_element_type=F32)
        acc_ref[...] = _lane_tile(a_bufs[slot][...], acc_ref.shape[1] // V7X_LANES) * acc_ref[...] + pv

    _pipelined_key_blocks(i, score_stage, functools.partial(softmax_stage, masked=False),
                          functools.partial(softmax_stage, masked=True), pv_stage)
    acc = acc_ref[...]
    o_ref[...] = (acc[:, :hd] / acc[:, hd:2 * hd]).astype(o_ref.dtype)


def _fox_attention(proj, cx, batch, seq, n_heads, tile):
    nq = seq // tile
    hd = HEAD_DIM
    nx = cx.shape[1]
    blocks = 2 * _nbytes((tile, hd), BF16) + 2 * _nbytes((seq, hd), BF16) + _nbytes((seq, nx), BF16)
    scratch = (_nbytes((seq, hd + nx), BF16) + _nbytes((seq, 2 * hd), BF16) + 2 * _nbytes((tile, tile), F32)
               + 2 * _nbytes((tile, tile), BF16) + 4 * _nbytes((tile, 2 * hd), F32))
    return pl.pallas_call(
        functools.partial(_fox_kernel, tile=tile, scale=hd ** -0.5, n_heads=n_heads),
        grid=(batch, n_heads, nq),
        in_specs=[
            pl.BlockSpec((tile, hd), lambda b, h, i: (b * nq + i, h)),
            pl.BlockSpec((seq, hd), lambda b, h, i: (b, n_heads + h)),
            pl.BlockSpec((seq, hd), lambda b, h, i: (b, 2 * n_heads + h)),
            pl.BlockSpec((seq, nx), lambda b, h, i: (b, 0)),
        ],
        out_specs=pl.BlockSpec((tile, hd), lambda b, h, i: (b * nq + i, h)),
        out_shape=jax.ShapeDtypeStruct((batch * seq, n_heads * hd), BF16),
        scratch_shapes=[
            pltpu.VMEM((seq, hd + nx), BF16),
            pltpu.VMEM((seq, 2 * hd), BF16),
            pltpu.VMEM((tile, tile), F32), pltpu.VMEM((tile, tile), F32),
            pltpu.VMEM((tile, tile), BF16), pltpu.VMEM((tile, tile), BF16),
            pltpu.VMEM((tile, V7X_LANES), F32), pltpu.VMEM((tile, V7X_LANES), F32),
            pltpu.VMEM((tile, V7X_LANES), F32),
            pltpu.VMEM((tile, 2 * hd), F32),
        ],
        compiler_params=pltpu.CompilerParams(
            dimension_semantics=("parallel", "parallel", "arbitrary"),
            vmem_limit_bytes=_vmem_limit(blocks, scratch, 4 * _nbytes((tile, tile), F32)),
        ),
        name="fox_attention",
    )(proj, proj, proj, cx)


def _diff_kernel(q_ref, k_ref, v_ref, bias_ref, lq1_ref, lk1_ref, lq2_ref, lk2_ref, g_ref, o_ref,
                 s0_ref, s1_ref, p0_ref, p1_ref, a0_ref, a1_ref, m_ref, l_ref, acc_ref,
                 *, tile, scale, lambda_init):
    s_bufs, p_bufs, a_bufs = (s0_ref, s1_ref), (p0_ref, p1_ref), (a0_ref, a1_ref)
    i = pl.program_id(2)
    hd = HEAD_DIM
    qf = q_ref[...].astype(F32) * (scale * LOG2E)
    lane = lax.broadcasted_iota(jnp.int32, qf.shape, 1)
    qq = jnp.concatenate([jnp.where(lane < hd, qf, 0.0).astype(BF16),
                          jnp.where(lane >= hd, qf, 0.0).astype(BF16)], axis=0)
    m_ref[...] = jnp.full(m_ref.shape, NEG_INF, F32)
    l_ref[...] = jnp.zeros(l_ref.shape, F32)
    acc_ref[...] = jnp.zeros(acc_ref.shape, F32)

    def score_stage(j, slot):
        k0 = pl.multiple_of(j * tile, tile)
        s_bufs[slot][...] = _qk(qq, k_ref[pl.ds(k0, tile), :])

    def softmax_stage(j, slot):
        bias = bias_ref[0, jnp.minimum(i - j, 2)]
        for c in range(2):
            rows = slice(c * tile, (c + 1) * tile)
            s = s_bufs[slot][rows, :] + bias
            m_prev = m_ref[rows, :]
            m_new = jnp.maximum(m_prev, jnp.max(s, axis=1, keepdims=True))
            alpha = jnp.exp2(m_prev - m_new)
            p = jnp.exp2(s - _lane_tile(m_new, tile // V7X_LANES))
            l_ref[rows, :] = alpha * l_ref[rows, :] + jnp.sum(p, axis=1, keepdims=True)
            a_bufs[slot][rows, :] = alpha
            p_bufs[slot][rows, :] = p.astype(BF16)
            m_ref[rows, :] = m_new

    def pv_stage(j, slot):
        k0 = pl.multiple_of(j * tile, tile)
        pv = jnp.dot(p_bufs[slot][...], v_ref[pl.ds(k0, tile), :], preferred_element_type=F32)
        acc_ref[...] = _lane_tile(a_bufs[slot][...], acc_ref.shape[1] // V7X_LANES) * acc_ref[...] + pv

    _pipelined_key_blocks(i, score_stage, softmax_stage, softmax_stage, pv_stage)

    lam = (jnp.exp(jnp.sum(lq1_ref[...] * lk1_ref[...], axis=1, keepdims=True))
           - jnp.exp(jnp.sum(lq2_ref[...] * lk2_ref[...], axis=1, keepdims=True))
           + lambda_init)
    reps = acc_ref.shape[1] // V7X_LANES
    out = (acc_ref[:tile, :] / _lane_tile(l_ref[:tile, :], reps)
           - lam * (acc_ref[tile:, :] / _lane_tile(l_ref[tile:, :], reps)))
    ms = jnp.mean(out * out, axis=-1, keepdims=True)
    y = out * lax.rsqrt(ms + SUBLN_EPS) * g_ref[...]
    o_ref[...] = (y * (1.0 - lambda_init)).astype(o_ref.dtype)


def _diff_attention(proj, bias, lq1, lk1, lq2, lk2, g_subln, batch, seq, n_heads, col0, tile, lambda_init):
    nq = seq // tile
    hd2 = 2 * HEAD_DIM
    c0 = col0 // hd2
    vec = lambda: pl.BlockSpec((1, HEAD_DIM), lambda b, h, i: (0, 0))
    blocks = (2 * _nbytes((tile, hd2), BF16) + 2 * _nbytes((seq, hd2), BF16)
              + _nbytes((3, tile, tile), F32))
    return pl.pallas_call(
        functools.partial(_diff_kernel, tile=tile, scale=HEAD_DIM ** -0.5, lambda_init=lambda_init),
        grid=(batch, n_heads, nq),
        in_specs=[
            pl.BlockSpec((tile, hd2), lambda b, h, i: (b * nq + i, c0 + h)),
            pl.BlockSpec((seq, hd2), lambda b, h, i: (b, c0 + n_heads + h)),
            pl.BlockSpec((seq, hd2), lambda b, h, i: (b, c0 + 2 * n_heads + h)),
            pl.BlockSpec((1, 3, tile, tile), lambda b, h, i: (h, 0, 0, 0)),
            vec(), vec(), vec(), vec(),
            pl.BlockSpec((1, hd2), lambda b, h, i: (0, 0)),
        ],
        out_specs=pl.BlockSpec((tile, hd2), lambda b, h, i: (b * nq + i, h)),
        out_shape=jax.ShapeDtypeStruct((batch * seq, n_heads * hd2), BF16),
        scratch_shapes=[
            pltpu.VMEM((2 * tile, tile), F32), pltpu.VMEM((2 * tile, tile), F32),
            pltpu.VMEM((2 * tile, tile), BF16), pltpu.VMEM((2 * tile, tile), BF16),
            pltpu.VMEM((2 * tile, V7X_LANES), F32), pltpu.VMEM((2 * tile, V7X_LANES), F32),
            pltpu.VMEM((2 * tile, V7X_LANES), F32),
            pltpu.VMEM((2 * tile, V7X_LANES), F32),
            pltpu.VMEM((2 * tile, hd2), F32),
        ],
        compiler_params=pltpu.CompilerParams(
            dimension_semantics=("parallel", "parallel", "arbitrary"),
            vmem_limit_bytes=_vmem_limit(
                blocks, 6 * _nbytes((tile, tile), F32) + 6 * _nbytes((tile, hd2), F32),
                6 * _nbytes((tile, tile), F32)),
        ),
        name="diff_attention",
    )(proj, proj, proj, bias, lq1, lk1, lq2, lk2, g_subln)


def _cross_kernel(q_ref, k_ref, v_ref, o_ref, *, n_heads, scale):
    dh = q_ref.shape[1] // n_heads
    for h in range(n_heads):
        cols = slice(h * dh, (h + 1) * dh)
        s = _qk(q_ref[:, cols], k_ref[:, cols]) * scale
        m = jnp.max(s, axis=1, keepdims=True)
        p = jnp.exp(s - m)
        p = p / jnp.sum(p, axis=1, keepdims=True)
        o_ref[:, cols] = jnp.dot(p.astype(BF16), v_ref[:, cols], preferred_element_type=F32).astype(o_ref.dtype)


def _cross_attention(q, kv, batch, seq, n_mem, tile):
    d = q.shape[1]
    nq = seq // tile
    blocks = 2 * _nbytes((tile, d), BF16) + 2 * _nbytes((n_mem, d), BF16)
    return pl.pallas_call(
        functools.partial(_cross_kernel, n_heads=H_MEM, scale=(d // H_MEM) ** -0.5),
        grid=(batch, nq),
        in_specs=[
            pl.BlockSpec((tile, d), lambda b, i: (b * nq + i, 0)),
            pl.BlockSpec((n_mem, d), lambda b, i: (b, 0)),
            pl.BlockSpec((n_mem, d), lambda b, i: (b, 1)),
        ],
        out_specs=pl.BlockSpec((tile, d), lambda b, i: (b * nq + i, 0)),
        out_shape=jax.ShapeDtypeStruct((batch * seq, d), BF16),
        compiler_params=pltpu.CompilerParams(
            dimension_semantics=("parallel", "arbitrary"),
            vmem_limit_bytes=_vmem_limit(blocks, 0, 8 * _nbytes((tile, n_mem), F32) + _nbytes((tile, d), F32)),
        ),
        name="cross_attention",
    )(q, kv, kv)


FINAL_NORM_ROWS = 32


def _mlp_kernel(a_ref, ss_ref, hs_ref, wu_ref, wd_ref, gf_ref, o_ref, *, eps):
    f = pl.program_id(1)
    rows = hs_ref.shape[0]

    @pl.when(f == 0)
    def _():
        o_ref[...] = jnp.zeros(o_ref.shape, F32)

    r0 = pl.multiple_of(f * rows, rows)
    o_ref[pl.ds(r0, rows), :] += hs_ref[...]

    r = _row_scale(ss_ref, a_ref.shape[1], eps)
    u = jnp.dot(a_ref[...], wu_ref[...], preferred_element_type=F32)
    act = (jnp.square(jnp.maximum(u, 0.0)) * _lane_tile(r * r, u.shape[1] // V7X_LANES)).astype(BF16)
    o_ref[...] += jnp.dot(act, wd_ref[...], preferred_element_type=F32)

    @pl.when(f == pl.num_programs(1) - 1)
    def _():
        def norm_rows(c, carry):
            c0 = pl.multiple_of(c * FINAL_NORM_ROWS, FINAL_NORM_ROWS)
            y = o_ref[pl.ds(c0, FINAL_NORM_ROWS), :]
            ms = jnp.mean(y * y, axis=-1, keepdims=True)
            o_ref[pl.ds(c0, FINAL_NORM_ROWS), :] = y * lax.rsqrt(ms + eps) * gf_ref[...]
            return carry

        lax.fori_loop(0, o_ref.shape[0] // FINAL_NORM_ROWS, norm_rows, 0)


def _mlp(a, ss, h, w_up, w_down, g_final, *, eps=NORM_EPS, tm=512, tf=512):
    m, d = h.shape
    dff = w_up.shape[1]
    tm = _pick_tile(m, tm)
    tf = _pick_tile(dff, tf)
    nf = dff // tf
    rows = tm // nf
    assert rows * nf == tm and rows % 8 == 0 and tm % FINAL_NORM_ROWS == 0
    nss = ss.shape[1]
    blocks = (2 * _nbytes((d, tf), BF16) + _nbytes((tm, d), BF16) + _nbytes((tm, nss), F32)
              + _nbytes((rows, d), F32) + _nbytes((tm, d), F32))
    return pl.pallas_call(
        functools.partial(_mlp_kernel, eps=eps),
        grid=(m // tm, nf),
        in_specs=[
            pl.BlockSpec((tm, d), lambda i, f: (i, 0)),
            pl.BlockSpec((tm, nss), lambda i, f: (i, 0)),
            pl.BlockSpec((rows, d), lambda i, f: (i * nf + f, 0)),
            pl.BlockSpec((d, tf), lambda i, f: (0, f)),
            pl.BlockSpec((tf, d), lambda i, f: (f, 0)),
            pl.BlockSpec((1, d), lambda i, f: (0, 0)),
        ],
        out_specs=pl.BlockSpec((tm, d), lambda i, f: (i, 0)),
        out_shape=jax.ShapeDtypeStruct((m, d), F32),
        compiler_params=pltpu.CompilerParams(
            dimension_semantics=("parallel", "arbitrary"),
            vmem_limit_bytes=_vmem_limit(blocks, 0, 4 * _nbytes((tm, tf), F32)),
        ),
        name="mlp_final_norm",
    )(a, ss, h, w_up, w_down, g_final.reshape(1, d))


def _attn_tile(seq):
    return _pick_tile(seq, 512)


def _layer(h, mem, l, g_mix, w_in, b_forget, lambda_q1, lambda_k1, lambda_q2, lambda_k2, g_subln, rel_bias,
           w_out, g_cross, g_mem, wq_mem, wk_mem, wv_mem, wo_mem, g_mlp, w_up, w_down, g_final):
    batch, seq, d = h.shape
    n_mem = mem.shape[1]
    h_fox = b_forget.shape[-1]
    h_diff = rel_bias.shape[1]
    w_fox = h_fox * HEAD_DIM
    tile = _attn_tile(seq)
    x2 = h.reshape(batch * seq, d)

    def gained(g, w):
        return (g.astype(F32)[:, None] * w).astype(BF16)

    wi = w_in[l]
    f0 = 3 * w_fox
    n_gate = -(-h_fox // V7X_LANES) * V7X_LANES
    w_gate = jnp.pad(gained(g_mix[l], wi[:, f0:f0 + h_fox]), ((0, 0), (0, n_gate - h_fox)))
    b_gate = jnp.pad(b_forget[l].astype(F32), (0, n_gate - h_fox)).reshape(1, n_gate)

    proj, f_logit = _in_proj(x2, gained(g_mix[l], wi[:, :f0]), gained(g_mix[l], wi[:, f0 + h_fox:]), w_gate)

    cx = _gate_cumsum(f_logit, b_gate, batch, seq, h_fox)
    fox = _fox_attention(proj, cx, batch, seq, h_fox, tile)

    lambda_init = 0.8 - 0.6 * math.exp(-0.3 * l)
    bias = _bias_tiles(rel_bias.astype(F32), tile)
    row = lambda v: v[l].astype(F32).reshape(1, -1)
    diff = _diff_attention(proj, bias, row(lambda_q1), row(lambda_k1), row(lambda_q2), row(lambda_k2),
                           row(g_subln), batch, seq, h_diff, 3 * w_fox, tile, lambda_init)

    w_o = w_out[l].astype(BF16)
    h1, h1b, ss1 = _matmul_resid((fox, 0), (diff, 0), w_fox, (w_o, 0), (w_o, 1), w_fox, x2, name="out_proj")

    q = _scaled_matmul(h1b, ss1, gained(g_cross[l], wq_mem[l]), name="cross_q_proj")
    w_kv = jnp.concatenate([wk_mem[l].astype(BF16), wv_mem[l].astype(BF16)], axis=1)
    kv = _norm_matmul(mem.reshape(batch * n_mem, d), g_mem[l], w_kv, name="cross_kv_proj")
    o = _cross_attention(q, kv, batch, seq, n_mem, tile)
    w_om = wo_mem[l].astype(BF16)
    half = d // 2
    h2, h2b, ss2 = _matmul_resid((o, 0), (o, 1), half, (w_om, 0), (w_om, 1), half, h1, name="cross_o_proj")

    out = _mlp(h2b, ss2, h2, gained(g_mlp[l], w_up[l]), w_down[l].astype(BF16), g_final)
    return out.reshape(batch, seq, d)


def kernel(x, mem, g_mix, w_in, b_forget, lambda_q1, lambda_k1, lambda_q2, lambda_k2, g_subln, rel_bias, w_out,
           g_cross, g_mem, wq_mem, wk_mem, wv_mem, wo_mem, g_mlp, w_up, w_down, g_final):
    depth = g_mix.shape[0]
    assert depth == 1, "the fused MLP epilogue applies the final norm; only depth 1 is supported"
    return _layer(x, mem, 0, g_mix, w_in, b_forget, lambda_q1, lambda_k1, lambda_q2, lambda_k2, g_subln,
                  rel_bias, w_out, g_cross, g_mem, wq_mem, wk_mem, wv_mem, wo_mem, g_mlp, w_up, w_down,
                  g_final)
```

```python
import functools
import math

import jax
import jax.numpy as jnp
from jax import lax
from jax.experimental import pallas as pl
from jax.experimental.pallas import tpu as pltpu

HEAD_DIM = 128
CHUNK = 64
N_BUCKETS = 32
MAX_DISTANCE = 128
H_MEM = 4
NORM_EPS = 1e-6
SUBLN_EPS = 1e-5
NEG_INF = -1e30
LOG2E = math.log2(math.e)

V7X_LANES = 128
V7X_VMEM_BYTES = 64 * 1024 * 1024
V7X_VMEM_REQUEST_CAP = V7X_VMEM_BYTES - 8 * 1024 * 1024

F32 = jnp.float32
BF16 = jnp.bfloat16


def _vmem_limit(block_bytes, scratch_bytes=0, temp_bytes=0):
    need = 2 * block_bytes + scratch_bytes + temp_bytes + (4 << 20)
    return int(min(max(need, 16 << 20), V7X_VMEM_REQUEST_CAP))


def _nbytes(shape, dtype):
    return math.prod(shape) * jnp.dtype(dtype).itemsize


def _pick_tile(n, target):
    if n <= target:
        return n
    t = target
    while t >= V7X_LANES:
        if n % t == 0:
            return t
        t -= V7X_LANES
    return n


def _t5_thresholds():
    half = N_BUCKETS // 2
    max_exact = half // 2
    steps = half - max_exact
    ratio = MAX_DISTANCE // max_exact
    thr = []
    for k in range(1, steps):
        n = max_exact
        while n ** steps < (ratio ** k) * (max_exact ** steps):
            n += 1
        thr.append(n)
    return half, max_exact, tuple(thr)


def _norm_matmul_kernel(x_ref, g_ref, w_ref, *rest, eps, with_gate):
    if with_gate:
        wf_ref, o_ref, f_ref, a_ref = rest
    else:
        o_ref, a_ref = rest

    @pl.when(pl.program_id(1) == 0)
    def _():
        x = x_ref[...]
        ms = jnp.mean(x * x, axis=-1, keepdims=True)
        a = (x * lax.rsqrt(ms + eps) * g_ref[...]).astype(BF16)
        a_ref[...] = a
        if with_gate:
            f_ref[...] = jnp.dot(a, wf_ref[...], preferred_element_type=F32)

    o_ref[...] = jnp.dot(a_ref[...], w_ref[...], preferred_element_type=F32).astype(o_ref.dtype)


def _norm_matmul(x, g, w, wf=None, *, eps=NORM_EPS, tm=512, tn=1024, name):
    m, k = x.shape
    n = w.shape[1]
    tm = _pick_tile(m, tm)
    tn = _pick_tile(n, tn)
    with_gate = wf is not None
    in_specs = [
        pl.BlockSpec((tm, k), lambda i, j: (i, 0), pipeline_mode=pl.Buffered(1)),
        pl.BlockSpec((1, k), lambda i, j: (0, 0)),
        pl.BlockSpec((k, tn), lambda i, j: (0, j)),
    ]
    args = [x, g.reshape(1, k), w]
    out_shape = [jax.ShapeDtypeStruct((m, n), BF16)]
    out_specs = [pl.BlockSpec((tm, tn), lambda i, j: (i, j))]
    blocks = _nbytes((k, tn), BF16) + _nbytes((tm, tn), BF16)
    if with_gate:
        nf = wf.shape[1]
        in_specs.append(pl.BlockSpec((k, nf), lambda i, j: (0, 0)))
        args.append(wf)
        out_shape.append(jax.ShapeDtypeStruct((m, nf), F32))
        out_specs.append(pl.BlockSpec((tm, nf), lambda i, j: (i, 0)))
        blocks += _nbytes((k, nf), BF16) + _nbytes((tm, nf), F32)
    res = pl.pallas_call(
        functools.partial(_norm_matmul_kernel, eps=eps, with_gate=with_gate),
        grid=(m // tm, n // tn),
        in_specs=in_specs,
        out_specs=out_specs,
        out_shape=out_shape,
        scratch_shapes=[pltpu.VMEM((tm, k), BF16)],
        compiler_params=pltpu.CompilerParams(
            dimension_semantics=("parallel", "arbitrary"),
            vmem_limit_bytes=_vmem_limit(
                blocks, _nbytes((tm, k), F32) + _nbytes((tm, k), BF16), 2 * _nbytes((tm, k), F32)),
        ),
        name=name,
    )(*args)
    return res if with_gate else res[0]


def _row_scale(ss_ref, k, eps):
    ss = ss_ref[:, :V7X_LANES]
    for c in range(1, ss_ref.shape[1] // V7X_LANES):
        ss = ss + ss_ref[:, c * V7X_LANES:(c + 1) * V7X_LANES]
    return lax.rsqrt(ss * (1.0 / k) + eps)


def _in_proj_kernel(xs_ref, g_ref, w_ref, wf_ref, o_ref, f_ref, a0_ref, a1_ref, fs0_ref, fs1_ref, *, eps):
    t = pl.program_id(0)
    j = pl.program_id(1)
    rows = xs_ref.shape[0]

    def step(a_cur, fs_cur, a_nxt, fs_nxt):
        def norm_slice():
            x = xs_ref[...]
            ms = jnp.mean(x * x, axis=-1, keepdims=True)
            a = (x * lax.rsqrt(ms + eps) * g_ref[...]).astype(BF16)
            r0 = pl.multiple_of(j * rows, rows)
            a_nxt[pl.ds(r0, rows), :] = a
            fs_nxt[pl.ds(r0, rows), :] = jnp.dot(a, wf_ref[...], preferred_element_type=F32)

        @pl.when((t > 0) & (j == 0))
        def _():
            f_ref[...] = fs_cur[...]

        @pl.when(t == 0)
        def _():
            o_ref[...] = jnp.zeros(o_ref.shape, o_ref.dtype)
            f_ref[...] = jnp.zeros(f_ref.shape, f_ref.dtype)
            norm_slice()

        @pl.when(t > 0)
        def _():
            o_ref[...] = jnp.dot(a_cur[...], w_ref[...], preferred_element_type=F32).astype(o_ref.dtype)
            norm_slice()

    @pl.when((t & 1) == 0)
    def _():
        step(a1_ref, fs1_ref, a0_ref, fs0_ref)

    @pl.when((t & 1) == 1)
    def _():
        step(a0_ref, fs0_ref, a1_ref, fs1_ref)


def _in_proj(x, g, w, wf, *, eps=NORM_EPS, tm=1024, tn=768):
    m, k = x.shape
    n = w.shape[1]
    tm = _pick_tile(m, tm)
    tn = _pick_tile(n, tn)
    nj = n // tn
    n_rt = m // tm
    rows = tm // nj
    assert rows * nj == tm and rows % 16 == 0, (tm, nj)
    nf = wf.shape[1]
    blocks = (_nbytes((k, tn), BF16) + _nbytes((tm, tn), BF16) + _nbytes((k, nf), BF16)
              + _nbytes((tm, nf), F32) + _nbytes((rows, k), F32))
    scratch = 2 * _nbytes((tm, k), BF16) + 2 * _nbytes((tm, nf), F32)
    return pl.pallas_call(
        functools.partial(_in_proj_kernel, eps=eps),
        grid=(n_rt + 1, nj),
        in_specs=[
            pl.BlockSpec((rows, k), lambda t, j: (jnp.minimum(t, n_rt - 1) * nj + j, 0)),
            pl.BlockSpec((1, k), lambda t, j: (0, 0)),
            pl.BlockSpec((k, tn), lambda t, j: (0, j)),
            pl.BlockSpec((k, nf), lambda t, j: (0, 0)),
        ],
        out_specs=[
            pl.BlockSpec((tm, tn), lambda t, j: (jnp.where(t == 0, n_rt, t - 1), j)),
            pl.BlockSpec((tm, nf), lambda t, j: (jnp.where(t == 0, n_rt, t - 1), 0)),
        ],
        out_shape=[
            jax.ShapeDtypeStruct((m + tm, n), BF16),
            jax.ShapeDtypeStruct((m + tm, nf), F32),
        ],
        scratch_shapes=[pltpu.VMEM((tm, k), BF16), pltpu.VMEM((tm, k), BF16),
                        pltpu.VMEM((tm, nf), F32), pltpu.VMEM((tm, nf), F32)],
        compiler_params=pltpu.CompilerParams(
            dimension_semantics=("arbitrary", "arbitrary"),
            vmem_limit_bytes=_vmem_limit(blocks, scratch, 2 * _nbytes((tm, tn), F32)),
        ),
        name="in_proj",
    )(x, g.reshape(1, k), w, wf)


def _scaled_matmul_kernel(a_ref, ss_ref, w_ref, o_ref, *, eps):
    r = _row_scale(ss_ref, a_ref.shape[1], eps)
    acc = jnp.dot(a_ref[...], w_ref[...], preferred_element_type=F32)
    o_ref[...] = (_lane_tile(r, o_ref.shape[1] // V7X_LANES) * acc).astype(o_ref.dtype)


def _scaled_matmul(a, ss, w, *, eps=NORM_EPS, tm=1024, tn=1024, name):
    m, k = a.shape
    n = w.shape[1]
    tm = _pick_tile(m, tm)
    tn = _pick_tile(n, tn)
    nss = ss.shape[1]
    blocks = (_nbytes((tm, k), BF16) + _nbytes((tm, nss), F32) + _nbytes((k, tn), BF16)
              + _nbytes((tm, tn), BF16))
    return pl.pallas_call(
        functools.partial(_scaled_matmul_kernel, eps=eps),
        grid=(m // tm, n // tn),
        in_specs=[
            pl.BlockSpec((tm, k), lambda i, j: (i, 0)),
            pl.BlockSpec((tm, nss), lambda i, j: (i, 0)),
            pl.BlockSpec((k, tn), lambda i, j: (0, j)),
        ],
        out_specs=pl.BlockSpec((tm, tn), lambda i, j: (i, j)),
        out_shape=jax.ShapeDtypeStruct((m, n), BF16),
        compiler_params=pltpu.CompilerParams(
            dimension_semantics=("parallel", "arbitrary"),
            vmem_limit_bytes=_vmem_limit(blocks, 0, 2 * _nbytes((tm, tn), F32)),
        ),
        name=name,
    )(a, ss, w)


def _matmul_resid_kernel(a1_ref, a2_ref, w1_ref, w2_ref, r_ref, o_ref, ob_ref, ss_ref):
    acc = jnp.dot(a1_ref[...], w1_ref[...], preferred_element_type=F32)
    acc = acc + jnp.dot(a2_ref[...], w2_ref[...], preferred_element_type=F32)
    h = r_ref[...] + acc
    o_ref[...] = h
    ob_ref[...] = h.astype(BF16)
    ss_ref[...] = jnp.broadcast_to(jnp.sum(h * h, axis=1, keepdims=True), ss_ref.shape)


def _matmul_resid(a1, a2, a_blk, w1, w2, w_blk, resid, *, tm=1024, tn=512, name):
    (a1, a1c), (a2, a2c) = a1, a2
    (w1, w1r), (w2, w2r) = w1, w2
    m, n = resid.shape
    tm = _pick_tile(m, tm)
    tn = _pick_tile(n, tn)
    blocks = (2 * _nbytes((tm, a_blk), BF16) + 2 * _nbytes((w_blk, tn), BF16)
              + 2 * _nbytes((tm, tn), F32) + _nbytes((tm, tn), BF16) + _nbytes((tm, V7X_LANES), F32))
    return pl.pallas_call(
        _matmul_resid_kernel,
        grid=(m // tm, n // tn),
        in_specs=[
            pl.BlockSpec((tm, a_blk), lambda i, j: (i, a1c)),
            pl.BlockSpec((tm, a_blk), lambda i, j: (i, a2c)),
            pl.BlockSpec((w_blk, tn), lambda i, j: (w1r, j)),
            pl.BlockSpec((w_blk, tn), lambda i, j: (w2r, j)),
            pl.BlockSpec((tm, tn), lambda i, j: (i, j)),
        ],
        out_specs=[
            pl.BlockSpec((tm, tn), lambda i, j: (i, j)),
            pl.BlockSpec((tm, tn), lambda i, j: (i, j)),
            pl.BlockSpec((tm, V7X_LANES), lambda i, j: (i, j)),
        ],
        out_shape=[
            jax.ShapeDtypeStruct((m, n), F32),
            jax.ShapeDtypeStruct((m, n), BF16),
            jax.ShapeDtypeStruct((m, (n // tn) * V7X_LANES), F32),
        ],
        compiler_params=pltpu.CompilerParams(
            dimension_semantics=("parallel", "arbitrary"),
            vmem_limit_bytes=_vmem_limit(blocks, 0, 3 * _nbytes((tm, tn), F32)),
        ),
        name=name,
    )(a1, a2, w1, w2, resid)


def _gate_cumsum_kernel(f_ref, b_ref, o_ref, *, n_heads):
    z = f_ref[...] + b_ref[...]
    x = jnp.minimum(z, 0.0) - jnp.log1p(jnp.exp(-jnp.abs(z)))
    s_len = x.shape[0]
    row = lax.broadcasted_iota(jnp.int32, x.shape, 0)
    d = 1
    while d < s_len:
        x = x + jnp.where(row >= d, pltpu.roll(x, d, axis=0), 0.0)
        d *= 2
    c2 = x * (-LOG2E)
    hi = c2.astype(BF16)
    r1 = c2 - hi.astype(F32)
    mid = r1.astype(BF16)
    lo = (r1 - mid.astype(F32)).astype(BF16)
    lane = lax.broadcasted_iota(jnp.int32, x.shape, 1)
    mid_s = pltpu.roll(mid.astype(F32), n_heads, axis=1)
    lo_s = pltpu.roll(lo.astype(F32), 2 * n_heads, axis=1)
    out = jnp.where(lane < n_heads, hi.astype(F32),
                    jnp.where(lane < 2 * n_heads, mid_s, jnp.where(lane < 3 * n_heads, lo_s, 0.0)))
    o_ref[...] = out.astype(BF16)


def _gate_cumsum(f_logit, b_pad, batch, seq, n_heads):
    nf = f_logit.shape[1]
    assert 3 * n_heads <= nf
    return pl.pallas_call(
        functools.partial(_gate_cumsum_kernel, n_heads=n_heads),
        grid=(batch,),
        in_specs=[
            pl.BlockSpec((seq, nf), lambda b: (b, 0)),
            pl.BlockSpec((1, nf), lambda b: (0, 0)),
        ],
        out_specs=pl.BlockSpec((seq, nf), lambda b: (b, 0)),
        out_shape=jax.ShapeDtypeStruct((batch * seq, nf), BF16),
        compiler_params=pltpu.CompilerParams(
            dimension_semantics=("parallel",),
            vmem_limit_bytes=_vmem_limit(2 * _nbytes((seq, nf), F32), 0, 8 * _nbytes((seq, nf), F32)),
        ),
        name="gate_cumsum",
    )(f_logit, b_pad)


def _bias_tile_kernel(rb_ref, o_ref, *, tile, n_heads):
    h = pl.program_id(0)
    d = pl.program_id(1)
    half, max_exact, thresholds = _t5_thresholds()
    t = lax.broadcasted_iota(jnp.int32, (tile, tile), 0)
    s = lax.broadcasted_iota(jnp.int32, (tile, tile), 1)
    rel = s - t - d * tile
    n = jnp.abs(rel)
    large = jnp.full((tile, tile), max_exact, jnp.int32)
    for thr in thresholds:
        large = large + (n >= thr).astype(jnp.int32)
    idx = jnp.where(n < max_exact, n, large) + jnp.where(rel > 0, half, 0)
    val = jnp.zeros((tile, tile), F32)
    for b in range(N_BUCKETS):
        val = jnp.where(idx == b, rb_ref[b * n_heads + h], val)
    shift = int(math.log2(CHUNK))
    allowed = (s >> shift) <= ((t >> shift) + d * tile)
    o_ref[0, 0] = jnp.where(allowed, val * LOG2E, NEG_INF)


def _bias_tiles(rel_bias, tile):
    n_heads = rel_bias.shape[1]
    return pl.pallas_call(
        functools.partial(_bias_tile_kernel, tile=tile, n_heads=n_heads),
        grid=(n_heads, 3),
        in_specs=[pl.BlockSpec(memory_space=pltpu.SMEM)],
        out_specs=pl.BlockSpec((1, 1, tile, tile), lambda h, d: (h, d, 0, 0)),
        out_shape=jax.ShapeDtypeStruct((n_heads, 3, tile, tile), F32),
        compiler_params=pltpu.CompilerParams(
            dimension_semantics=("parallel", "parallel"),
            vmem_limit_bytes=_vmem_limit(_nbytes((tile, tile), F32), 0, 8 * _nbytes((tile, tile), F32)),
        ),
        name="t5_bias_tiles",
    )(rel_bias.reshape(-1))


def _lane_tile(x, reps):
    return x if reps == 1 else jnp.concatenate([x] * reps, axis=1)


def _qk(q, k):
    return lax.dot_general(q, k, (((1,), (1,)), ((), ())), preferred_element_type=F32)


def _pipelined_key_blocks(i, score, softmax, softmax_diag, pv):
    score(0, 0)

    @pl.when(i == 0)
    def _():
        softmax_diag(0, 0)
        pv(0, 0)

    @pl.when(i > 0)
    def _():
        score(1, 1)
        softmax(0, 0)

        def pair(t, carry):
            j = 2 * t
            score(j + 2, 0)
            softmax(j + 1, 1)
            pv(j, 0)
            score(j + 3, 1)
            softmax(j + 2, 0)
            pv(j + 1, 1)
            return carry

        lax.fori_loop(0, (i - 1) >> 1, pair, 0)

        @pl.when((i & 1) == 1)
        def _():
            softmax_diag(i, 1)
            pv(i - 1, 0)
            pv(i, 1)

        @pl.when((i & 1) == 0)
        def _():
            score(i, 0)
            softmax(i - 1, 1)
            pv(i - 2, 0)
            softmax_diag(i, 0)
            pv(i - 1, 1)
            pv(i, 0)


def _fox_kernel(q_ref, k_ref, v_ref, cx_ref, o_ref, kx_ref, vx_ref, s0_ref, s1_ref, p0_ref, p1_ref,
                a0_ref, a1_ref, m_ref, acc_ref, *, tile, scale, n_heads):
    s_bufs, p_bufs, a_bufs = (s0_ref, s1_ref), (p0_ref, p1_ref), (a0_ref, a1_ref)
    h = pl.program_id(1)
    i = pl.program_id(2)
    hd = HEAD_DIM

    @pl.when(i == 0)
    def _():
        kx_ref[:, :hd] = k_ref[...]
        kx_ref[:, hd:] = cx_ref[...]
        vx_ref[:, :hd] = v_ref[...]
        vx_ref[:, hd:] = jnp.ones((vx_ref.shape[0], vx_ref.shape[1] - hd), BF16)

    lane = lax.broadcasted_iota(jnp.int32, (tile, cx_ref.shape[1]), 1)
    sel = (lane == h) | (lane == h + n_heads) | (lane == h + 2 * n_heads)
    qx = jnp.concatenate([(q_ref[...].astype(F32) * (scale * LOG2E)).astype(BF16),
                          jnp.where(sel, 1.0, 0.0).astype(BF16)], axis=1)
    m_ref[...] = jnp.full(m_ref.shape, NEG_INF, F32)
    acc_ref[...] = jnp.zeros(acc_ref.shape, F32)

    def score_stage(j, slot):
        k0 = pl.multiple_of(j * tile, tile)
        s_bufs[slot][...] = _qk(qx, kx_ref[pl.ds(k0, tile), :])

    def softmax_stage(j, slot, masked):
        s = s_bufs[slot][...]
        if masked:
            row = lax.broadcasted_iota(jnp.int32, s.shape, 0)
            col = lax.broadcasted_iota(jnp.int32, s.shape, 1)
            s = jnp.where(col <= row, s, NEG_INF)
        m_prev = m_ref[...]
        m_new = jnp.maximum(m_prev, jnp.max(s, axis=1, keepdims=True))
        a_bufs[slot][...] = jnp.exp2(m_prev - m_new)
        p_bufs[slot][...] = jnp.exp2(s - _lane_tile(m_new, tile // V7X_LANES)).astype(BF16)
        m_ref[...] = m_new

    def pv_stage(j, slot):
        k0 = pl.multiple_of(j * tile, tile)
        pv = jnp.dot(p_bufs[slot][...], vx_ref[pl.ds(k0, tile), :], preferred_element_type=F32)
        acc_ref[...] = _lane_tile(a_bufs[slot][...], acc_ref.shape[1] // V7X_LANES) * acc_ref[...] + pv

    _pipelined_key_blocks(i, score_stage, functools.partial(softmax_stage, masked=False),
                          functools.partial(softmax_stage, masked=True), pv_stage)
    acc = acc_ref[...]
    o_ref[...] = (acc[:, :hd] / acc[:, hd:2 * hd]).astype(o_ref.dtype)


def _fox_attention(proj, cx, batch, seq, n_heads, tile):
    nq = seq // tile
    hd = HEAD_DIM
    nx = cx.shape[1]
    blocks = 2 * _nbytes((tile, hd), BF16) + 2 * _nbytes((seq, hd), BF16) + _nbytes((seq, nx), BF16)
    scratch = (_nbytes((seq, hd + nx), BF16) + _nbytes((seq, 2 * hd), BF16) + 2 * _nbytes((tile, tile), F32)
               + 2 * _nbytes((tile, tile), BF16) + 4 * _nbytes((tile, 2 * hd), F32))
    return pl.pallas_call(
        functools.partial(_fox_kernel, tile=tile, scale=hd ** -0.5, n_heads=n_heads),
        grid=(batch, n_heads, nq),
        in_specs=[
            pl.BlockSpec((tile, hd), lambda b, h, i: (b * nq + i, h)),
            pl.BlockSpec((seq, hd), lambda b, h, i: (b, n_heads + h)),
            pl.BlockSpec((seq, hd), lambda b, h, i: (b, 2 * n_heads + h)),
            pl.BlockSpec((seq, nx), lambda b, h, i: (b, 0)),
        ],
        out_specs=pl.BlockSpec((tile, hd), lambda b, h, i: (b * nq + i, h)),
        out_shape=jax.ShapeDtypeStruct((batch * seq, n_heads * hd), BF16),
        scratch_shapes=[
            pltpu.VMEM((seq, hd + nx), BF16),
            pltpu.VMEM((seq, 2 * hd), BF16),
            pltpu.VMEM((tile, tile), F32), pltpu.VMEM((tile, tile), F32),
            pltpu.VMEM((tile, tile), BF16), pltpu.VMEM((tile, tile), BF16),
            pltpu.VMEM((tile, V7X_LANES), F32), pltpu.VMEM((tile, V7X_LANES), F32),
            pltpu.VMEM((tile, V7X_LANES), F32),
            pltpu.VMEM((tile, 2 * hd), F32),
        ],
        compiler_params=pltpu.CompilerParams(
            dimension_semantics=("parallel", "parallel", "arbitrary"),
            vmem_limit_bytes=_vmem_limit(blocks, scratch, 4 * _nbytes((tile, tile), F32)),
        ),
        name="fox_attention",
    )(proj, proj, proj, cx)


def _diff_kernel(q_ref, k_ref, v_ref, bias_ref, lq1_ref, lk1_ref, lq2_ref, lk2_ref, g_ref, o_ref,
                 s0_ref, s1_ref, p0_ref, p1_ref, a0_ref, a1_ref, m_ref, l_ref, acc_ref,
                 *, tile, scale, lambda_init):
    s_bufs, p_bufs, a_bufs = (s0_ref, s1_ref), (p0_ref, p1_ref), (a0_ref, a1_ref)
    i = pl.program_id(2)
    hd = HEAD_DIM
    qf = q_ref[...].astype(F32) * (scale * LOG2E)
    lane = lax.broadcasted_iota(jnp.int32, qf.shape, 1)
    qq = jnp.concatenate([jnp.where(lane < hd, qf, 0.0).astype(BF16),
                          jnp.where(lane >= hd, qf, 0.0).astype(BF16)], axis=0)
    m_ref[...] = jnp.full(m_ref.shape, NEG_INF, F32)
    l_ref[...] = jnp.zeros(l_ref.shape, F32)
    acc_ref[...] = jnp.zeros(acc_ref.shape, F32)

    def score_stage(j, slot):
        k0 = pl.multiple_of(j * tile, tile)
        s_bufs[slot][...] = _qk(qq, k_ref[pl.ds(k0, tile), :])

    def softmax_stage(j, slot):
        bias = bias_ref[0, jnp.minimum(i - j, 2)]
        for c in range(2):
            rows = slice(c * tile, (c + 1) * tile)
            s = s_bufs[slot][rows, :] + bias
            m_prev = m_ref[rows, :]
            m_new = jnp.maximum(m_prev, jnp.max(s, axis=1, keepdims=True))
            alpha = jnp.exp2(m_prev - m_new)
            p = jnp.exp2(s - _lane_tile(m_new, tile // V7X_LANES))
            l_ref[rows, :] = alpha * l_ref[rows, :] + jnp.sum(p, axis=1, keepdims=True)
            a_bufs[slot][rows, :] = alpha
            p_bufs[slot][rows, :] = p.astype(BF16)
            m_ref[rows, :] = m_new

    def pv_stage(j, slot):
        k0 = pl.multiple_of(j * tile, tile)
        pv = jnp.dot(p_bufs[slot][...], v_ref[pl.ds(k0, tile), :], preferred_element_type=F32)
        acc_ref[...] = _lane_tile(a_bufs[slot][...], acc_ref.shape[1] // V7X_LANES) * acc_ref[...] + pv

    _pipelined_key_blocks(i, score_stage, softmax_stage, softmax_stage, pv_stage)

    lam = (jnp.exp(jnp.sum(lq1_ref[...] * lk1_ref[...], axis=1, keepdims=True))
           - jnp.exp(jnp.sum(lq2_ref[...] * lk2_ref[...], axis=1, keepdims=True))
           + lambda_init)
    reps = acc_ref.shape[1] // V7X_LANES
    out = (acc_ref[:tile, :] / _lane_tile(l_ref[:tile, :], reps)
           - lam * (acc_ref[tile:, :] / _lane_tile(l_ref[tile:, :], reps)))
    ms = jnp.mean(out * out, axis=-1, keepdims=True)
    y = out * lax.rsqrt(ms + SUBLN_EPS) * g_ref[...]
    o_ref[...] = (y * (1.0 - lambda_init)).astype(o_ref.dtype)


def _diff_attention(proj, bias, lq1, lk1, lq2, lk2, g_subln, batch, seq, n_heads, col0, tile, lambda_init):
    nq = seq // tile
    hd2 = 2 * HEAD_DIM
    c0 = col0 // hd2
    vec = lambda: pl.BlockSpec((1, HEAD_DIM), lambda b, h, i: (0, 0))
    blocks = (2 * _nbytes((tile, hd2), BF16) + 2 * _nbytes((seq, hd2), BF16)
              + _nbytes((3, tile, tile), F32))
    return pl.pallas_call(
        functools.partial(_diff_kernel, tile=tile, scale=HEAD_DIM ** -0.5, lambda_init=lambda_init),
        grid=(batch, n_heads, nq),
        in_specs=[
            pl.BlockSpec((tile, hd2), lambda b, h, i: (b * nq + i, c0 + h)),
            pl.BlockSpec((seq, hd2), lambda b, h, i: (b, c0 + n_heads + h)),
            pl.BlockSpec((seq, hd2), lambda b, h, i: (b, c0 + 2 * n_heads + h)),
            pl.BlockSpec((1, 3, tile, tile), lambda b, h, i: (h, 0, 0, 0)),
            vec(), vec(), vec(), vec(),
            pl.BlockSpec((1, hd2), lambda b, h, i: (0, 0)),
        ],
        out_specs=pl.BlockSpec((tile, hd2), lambda b, h, i: (b * nq + i, h)),
        out_shape=jax.ShapeDtypeStruct((batch * seq, n_heads * hd2), BF16),
        scratch_shapes=[
            pltpu.VMEM((2 * tile, tile), F32), pltpu.VMEM((2 * tile, tile), F32),
            pltpu.VMEM((2 * tile, tile), BF16), pltpu.VMEM((2 * tile, tile), BF16),
            pltpu.VMEM((2 * tile, V7X_LANES), F32), pltpu.VMEM((2 * tile, V7X_LANES), F32),
            pltpu.VMEM((2 * tile, V7X_LANES), F32),
            pltpu.VMEM((2 * tile, V7X_LANES), F32),
            pltpu.VMEM((2 * tile, hd2), F32),
        ],
        compiler_params=pltpu.CompilerParams(
            dimension_semantics=("parallel", "parallel", "arbitrary"),
            vmem_limit_bytes=_vmem_limit(
                blocks, 6 * _nbytes((tile, tile), F32) + 6 * _nbytes((tile, hd2), F32),
                6 * _nbytes((tile, tile), F32)),
        ),
        name="diff_attention",
    )(proj, proj, proj, bias, lq1, lk1, lq2, lk2, g_subln)


def _cross_kernel(q_ref, k_ref, v_ref, o_ref, *, n_heads, scale):
    dh = q_ref.shape[1] // n_heads
    for h in range(n_heads):
        cols = slice(h * dh, (h + 1) * dh)
        s = _qk(q_ref[:, cols], k_ref[:, cols]) * scale
        m = jnp.max(s, axis=1, keepdims=True)
        p = jnp.exp(s - m)
        p = p / jnp.sum(p, axis=1, keepdims=True)
        o_ref[:, cols] = jnp.dot(p.astype(BF16), v_ref[:, cols], preferred_element_type=F32).astype(o_ref.dtype)


def _cross_attention(q, kv, batch, seq, n_mem, tile):
    d = q.shape[1]
    nq = seq // tile
    blocks = 2 * _nbytes((tile, d), BF16) + 2 * _nbytes((n_mem, d), BF16)
    return pl.pallas_call(
        functools.partial(_cross_kernel, n_heads=H_MEM, scale=(d // H_MEM) ** -0.5),
        grid=(batch, nq),
        in_specs=[
            pl.BlockSpec((tile, d), lambda b, i: (b * nq + i, 0)),
            pl.BlockSpec((n_mem, d), lambda b, i: (b, 0)),
            pl.BlockSpec((n_mem, d), lambda b, i: (b, 1)),
        ],
        out_specs=pl.BlockSpec((tile, d), lambda b, i: (b * nq + i, 0)),
        out_shape=jax.ShapeDtypeStruct((batch * seq, d), BF16),
        compiler_params=pltpu.CompilerParams(
            dimension_semantics=("parallel", "arbitrary"),
            vmem_limit_bytes=_vmem_limit(blocks, 0, 8 * _nbytes((tile, n_mem), F32) + _nbytes((tile, d), F32)),
        ),
        name="cross_attention",
    )(q, kv, kv)


FINAL_NORM_ROWS = 32


def _mlp_kernel(a_ref, ss_ref, hs_ref, wu_ref, wd_ref, gf_ref, o_ref, *, eps):
    f = pl.program_id(1)
    rows = hs_ref.shape[0]

    @pl.when(f == 0)
    def _():
        o_ref[...] = jnp.zeros(o_ref.shape, F32)

    r0 = pl.multiple_of(f * rows, rows)
    o_ref[pl.ds(r0, rows), :] += hs_ref[...]

    r = _row_scale(ss_ref, a_ref.shape[1], eps)
    u = jnp.dot(a_ref[...], wu_ref[...], preferred_element_type=F32)
    act = (jnp.square(jnp.maximum(u, 0.0)) * _lane_tile(r * r, u.shape[1] // V7X_LANES)).astype(BF16)
    o_ref[...] += jnp.dot(act, wd_ref[...], preferred_element_type=F32)

    @pl.when(f == pl.num_programs(1) - 1)
    def _():
        def norm_rows(c, carry):
            c0 = pl.multiple_of(c * FINAL_NORM_ROWS, FINAL_NORM_ROWS)
            y = o_ref[pl.ds(c0, FINAL_NORM_ROWS), :]
            ms = jnp.mean(y * y, axis=-1, keepdims=True)
            o_ref[pl.ds(c0, FINAL_NORM_ROWS), :] = y * lax.rsqrt(ms + eps) * gf_ref[...]
            return carry

        lax.fori_loop(0, o_ref.shape[0] // FINAL_NORM_ROWS, norm_rows, 0)


def _mlp(a, ss, h, w_up, w_down, g_final, *, eps=NORM_EPS, tm=512, tf=512):
    m, d = h.shape
    dff = w_up.shape[1]
    tm = _pick_tile(m, tm)
    tf = _pick_tile(dff, tf)
    nf = dff // tf
    rows = tm // nf
    assert rows * nf == tm and rows % 8 == 0 and tm % FINAL_NORM_ROWS == 0
    nss = ss.shape[1]
    blocks = (2 * _nbytes((d, tf), BF16) + _nbytes((tm, d), BF16) + _nbytes((tm, nss), F32)
              + _nbytes((rows, d), F32) + _nbytes((tm, d), F32))
    return pl.pallas_call(
        functools.partial(_mlp_kernel, eps=eps),
        grid=(m // tm, nf),
        in_specs=[
            pl.BlockSpec((tm, d), lambda i, f: (i, 0)),
            pl.BlockSpec((tm, nss), lambda i, f: (i, 0)),
            pl.BlockSpec((rows, d), lambda i, f: (i * nf + f, 0)),
            pl.BlockSpec((d, tf), lambda i, f: (0, f)),
            pl.BlockSpec((tf, d), lambda i, f: (f, 0)),
            pl.BlockSpec((1, d), lambda i, f: (0, 0)),
        ],
        out_specs=pl.BlockSpec((tm, d), lambda i, f: (i, 0)),
        out_shape=jax.ShapeDtypeStruct((m, d), F32),
        compiler_params=pltpu.CompilerParams(
            dimension_semantics=("parallel", "arbitrary"),
            vmem_limit_bytes=_vmem_limit(blocks, 0, 4 * _nbytes((tm, tf), F32)),
        ),
        name="mlp_final_norm",
    )(a, ss, h, w_up, w_down, g_final.reshape(1, d))


def _attn_tile(seq):
    return _pick_tile(seq, 512)


def _layer(h, mem, l, g_mix, w_in, b_forget, lambda_q1, lambda_k1, lambda_q2, lambda_k2, g_subln, rel_bias,
           w_out, g_cross, g_mem, wq_mem, wk_mem, wv_mem, wo_mem, g_mlp, w_up, w_down, g_final):
    batch, seq, d = h.shape
    n_mem = mem.shape[1]
    h_fox = b_forget.shape[-1]
    h_diff = rel_bias.shape[1]
    w_fox = h_fox * HEAD_DIM
    tile = _attn_tile(seq)
    x2 = h.reshape(batch * seq, d)

    def gained(g, w):
        return (g.astype(F32)[:, None] * w).astype(BF16)

    wi = w_in[l]
    f0 = 3 * w_fox
    w_qkv = jnp.concatenate([wi[:, :f0].astype(BF16), wi[:, f0 + h_fox:].astype(BF16)], axis=1)
    n_gate = -(-h_fox // V7X_LANES) * V7X_LANES
    w_gate = jnp.pad(wi[:, f0:f0 + h_fox].astype(BF16), ((0, 0), (0, n_gate - h_fox)))
    b_gate = jnp.pad(b_forget[l].astype(F32), (0, n_gate - h_fox)).reshape(1, n_gate)

    proj, f_logit = _in_proj(x2, g_mix[l], w_qkv, w_gate)

    cx = _gate_cumsum(f_logit, b_gate, batch, seq, h_fox)
    fox = _fox_attention(proj, cx, batch, seq, h_fox, tile)

    lambda_init = 0.8 - 0.6 * math.exp(-0.3 * l)
    bias = _bias_tiles(rel_bias.astype(F32), tile)
    row = lambda v: v[l].astype(F32).reshape(1, -1)
    diff = _diff_attention(proj, bias, row(lambda_q1), row(lambda_k1), row(lambda_q2), row(lambda_k2),
                           row(g_subln), batch, seq, h_diff, 3 * w_fox, tile, lambda_init)

    w_o = w_out[l].astype(BF16)
    h1, h1b, ss1 = _matmul_resid((fox, 0), (diff, 0), w_fox, (w_o, 0), (w_o, 1), w_fox, x2, name="out_proj")

    q = _scaled_matmul(h1b, ss1, gained(g_cross[l], wq_mem[l]), name="cross_q_proj")
    w_kv = jnp.concatenate([wk_mem[l].astype(BF16), wv_mem[l].astype(BF16)], axis=1)
    kv = _norm_matmul(mem.reshape(batch * n_mem, d), g_mem[l], w_kv, name="cross_kv_proj")
    o = _cross_attention(q, kv, batch, seq, n_mem, tile)
    w_om = wo_mem[l].astype(BF16)
    half = d // 2
    h2, h2b, ss2 = _matmul_resid((o, 0), (o, 1), half, (w_om, 0), (w_om, 1), half, h1, name="cross_o_proj")

    out = _mlp(h2b, ss2, h2, gained(g_mlp[l], w_up[l]), w_down[l].astype(BF16), g_final)
    return out.reshape(batch, seq, d)


def kernel(x, mem, g_mix, w_in, b_forget, lambda_q1, lambda_k1, lambda_q2, lambda_k2, g_subln, rel_bias, w_out,
           g_cross, g_mem, wq_mem, wk_mem, wv_mem, wo_mem, g_mlp, w_up, w_down, g_final):
    depth = g_mix.shape[0]
    assert depth == 1, "the fused MLP epilogue applies the final norm; only depth 1 is supported"
    return _layer(x, mem, 0, g_mix, w_in, b_forget, lambda_q1, lambda_k1, lambda_q2, lambda_k2, g_subln,
                  rel_bias, w_out, g_cross, g_mem, wq_mem, wk_mem, wv_mem, wo_mem, g_mlp, w_up, w_down,
                  g_final)
```

```python
import functools
import math

import jax
import jax.numpy as jnp
from jax import lax
from jax.experimental import pallas as pl
from jax.experimental.pallas import tpu as pltpu

HEAD_DIM = 128
CHUNK = 64
N_BUCKETS = 32
MAX_DISTANCE = 128
H_MEM = 4
NORM_EPS = 1e-6
SUBLN_EPS = 1e-5
NEG_INF = -1e30
LOG2E = math.log2(math.e)

V7X_LANES = 128
V7X_VMEM_BYTES = 64 * 1024 * 1024
V7X_VMEM_REQUEST_CAP = V7X_VMEM_BYTES - 8 * 1024 * 1024

F32 = jnp.float32
BF16 = jnp.bfloat16


def _vmem_limit(block_bytes, scratch_bytes=0, temp_bytes=0):
    need = 2 * block_bytes + scratch_bytes + temp_bytes + (4 << 20)
    return int(min(max(need, 16 << 20), V7X_VMEM_REQUEST_CAP))


def _nbytes(shape, dtype):
    return math.prod(shape) * jnp.dtype(dtype).itemsize


def _pick_tile(n, target):
    if n <= target:
        return n
    t = target
    while t >= V7X_LANES:
        if n % t == 0:
            return t
        t -= V7X_LANES
    return n


def _t5_thresholds():
    half = N_BUCKETS // 2
    max_exact = half // 2
    steps = half - max_exact
    ratio = MAX_DISTANCE // max_exact
    thr = []
    for k in range(1, steps):
        n = max_exact
        while n ** steps < (ratio ** k) * (max_exact ** steps):
            n += 1
        thr.append(n)
    return half, max_exact, tuple(thr)


def _norm_matmul_kernel(x_ref, g_ref, w_ref, *rest, eps, with_gate):
    if with_gate:
        wf_ref, o_ref, f_ref, a_ref = rest
    else:
        o_ref, a_ref = rest

    @pl.when(pl.program_id(1) == 0)
    def _():
        x = x_ref[...]
        ms = jnp.mean(x * x, axis=-1, keepdims=True)
        a = (x * lax.rsqrt(ms + eps) * g_ref[...]).astype(BF16)
        a_ref[...] = a
        if with_gate:
            f_ref[...] = jnp.dot(a, wf_ref[...], preferred_element_type=F32)

    o_ref[...] = jnp.dot(a_ref[...], w_ref[...], preferred_element_type=F32).astype(o_ref.dtype)


def _norm_matmul(x, g, w, wf=None, *, eps=NORM_EPS, tm=512, tn=1024, name):
    m, k = x.shape
    n = w.shape[1]
    tm = _pick_tile(m, tm)
    tn = _pick_tile(n, tn)
    with_gate = wf is not None
    in_specs = [
        pl.BlockSpec((tm, k), lambda i, j: (i, 0), pipeline_mode=pl.Buffered(1)),
        pl.BlockSpec((1, k), lambda i, j: (0, 0)),
        pl.BlockSpec((k, tn), lambda i, j: (0, j)),
    ]
    args = [x, g.reshape(1, k), w]
    out_shape = [jax.ShapeDtypeStruct((m, n), BF16)]
    out_specs = [pl.BlockSpec((tm, tn), lambda i, j: (i, j))]
    blocks = _nbytes((k, tn), BF16) + _nbytes((tm, tn), BF16)
    if with_gate:
        nf = wf.shape[1]
        in_specs.append(pl.BlockSpec((k, nf), lambda i, j: (0, 0)))
        args.append(wf)
        out_shape.append(jax.ShapeDtypeStruct((m, nf), F32))
        out_specs.append(pl.BlockSpec((tm, nf), lambda i, j: (i, 0)))
        blocks += _nbytes((k, nf), BF16) + _nbytes((tm, nf), F32)
    res = pl.pallas_call(
        functools.partial(_norm_matmul_kernel, eps=eps, with_gate=with_gate),
        grid=(m // tm, n // tn),
        in_specs=in_specs,
        out_specs=out_specs,
        out_shape=out_shape,
        scratch_shapes=[pltpu.VMEM((tm, k), BF16)],
        compiler_params=pltpu.CompilerParams(
            dimension_semantics=("parallel", "arbitrary"),
            vmem_limit_bytes=_vmem_limit(
                blocks, _nbytes((tm, k), F32) + _nbytes((tm, k), BF16), 2 * _nbytes((tm, k), F32)),
        ),
        name=name,
    )(*args)
    return res if with_gate else res[0]


def _row_scale(ss_ref, k, eps):
    ss = ss_ref[:, :V7X_LANES]
    for c in range(1, ss_ref.shape[1] // V7X_LANES):
        ss = ss + ss_ref[:, c * V7X_LANES:(c + 1) * V7X_LANES]
    return lax.rsqrt(ss * (1.0 / k) + eps)


def _in_proj_kernel(xs_ref, g_ref, w_ref, wf_ref, o_ref, f_ref, a0_ref, a1_ref, fs0_ref, fs1_ref, *, eps):
    t = pl.program_id(0)
    j = pl.program_id(1)
    rows = xs_ref.shape[0]

    def step(a_cur, fs_cur, a_nxt, fs_nxt):
        def norm_slice():
            x = xs_ref[...]
            ms = jnp.mean(x * x, axis=-1, keepdims=True)
            a = (x * lax.rsqrt(ms + eps) * g_ref[...]).astype(BF16)
            r0 = pl.multiple_of(j * rows, rows)
            a_nxt[pl.ds(r0, rows), :] = a
            fs_nxt[pl.ds(r0, rows), :] = jnp.dot(a, wf_ref[...], preferred_element_type=F32)

        @pl.when((t > 0) & (j == 0))
        def _():
            f_ref[...] = fs_cur[...]

        @pl.when(t == 0)
        def _():
            o_ref[...] = jnp.zeros(o_ref.shape, o_ref.dtype)
            f_ref[...] = jnp.zeros(f_ref.shape, f_ref.dtype)
            norm_slice()

        @pl.when(t > 0)
        def _():
            o_ref[...] = jnp.dot(a_cur[...], w_ref[...], preferred_element_type=F32).astype(o_ref.dtype)
            norm_slice()

    @pl.when((t & 1) == 0)
    def _():
        step(a1_ref, fs1_ref, a0_ref, fs0_ref)

    @pl.when((t & 1) == 1)
    def _():
        step(a0_ref, fs0_ref, a1_ref, fs1_ref)


def _in_proj(x, g, w, wf, *, eps=NORM_EPS, tm=1024, tn=768):
    m, k = x.shape
    n = w.shape[1]
    tm = _pick_tile(m, tm)
    tn = _pick_tile(n, tn)
    nj = n // tn
    n_rt = m // tm
    rows = tm // nj
    assert rows * nj == tm and rows % 16 == 0, (tm, nj)
    nf = wf.shape[1]
    blocks = (_nbytes((k, tn), BF16) + _nbytes((tm, tn), BF16) + _nbytes((k, nf), BF16)
              + _nbytes((tm, nf), F32) + _nbytes((rows, k), F32))
    scratch = 2 * _nbytes((tm, k), BF16) + 2 * _nbytes((tm, nf), F32)
    return pl.pallas_call(
        functools.partial(_in_proj_kernel, eps=eps),
        grid=(n_rt + 1, nj),
        in_specs=[
            pl.BlockSpec((rows, k), lambda t, j: (jnp.minimum(t, n_rt - 1) * nj + j, 0)),
            pl.BlockSpec((1, k), lambda t, j: (0, 0)),
            pl.BlockSpec((k, tn), lambda t, j: (0, j)),
            pl.BlockSpec((k, nf), lambda t, j: (0, 0)),
        ],
        out_specs=[
            pl.BlockSpec((tm, tn), lambda t, j: (jnp.where(t == 0, n_rt, t - 1), j)),
            pl.BlockSpec((tm, nf), lambda t, j: (jnp.where(t == 0, n_rt, t - 1), 0)),
        ],
        out_shape=[
            jax.ShapeDtypeStruct((m + tm, n), BF16),
            jax.ShapeDtypeStruct((m + tm, nf), F32),
        ],
        scratch_shapes=[pltpu.VMEM((tm, k), BF16), pltpu.VMEM((tm, k), BF16),
                        pltpu.VMEM((tm, nf), F32), pltpu.VMEM((tm, nf), F32)],
        compiler_params=pltpu.CompilerParams(
            dimension_semantics=("arbitrary", "arbitrary"),
            vmem_limit_bytes=_vmem_limit(blocks, scratch, 2 * _nbytes((tm, tn), F32)),
        ),
        name="in_proj",
    )(x, g.reshape(1, k), w, wf)


def _scaled_matmul_kernel(a_ref, ss_ref, w_ref, o_ref, *, eps):
    r = _row_scale(ss_ref, a_ref.shape[1], eps)
    acc = jnp.dot(a_ref[...], w_ref[...], preferred_element_type=F32)
    o_ref[...] = (_lane_tile(r, o_ref.shape[1] // V7X_LANES) * acc).astype(o_ref.dtype)


def _scaled_matmul(a, ss, w, *, eps=NORM_EPS, tm=1024, tn=1024, name):
    m, k = a.shape
    n = w.shape[1]
    tm = _pick_tile(m, tm)
    tn = _pick_tile(n, tn)
    nss = ss.shape[1]
    blocks = (_nbytes((tm, k), BF16) + _nbytes((tm, nss), F32) + _nbytes((k, tn), BF16)
              + _nbytes((tm, tn), BF16))
    return pl.pallas_call(
        functools.partial(_scaled_matmul_kernel, eps=eps),
        grid=(m // tm, n // tn),
        in_specs=[
            pl.BlockSpec((tm, k), lambda i, j: (i, 0)),
            pl.BlockSpec((tm, nss), lambda i, j: (i, 0)),
            pl.BlockSpec((k, tn), lambda i, j: (0, j)),
        ],
        out_specs=pl.BlockSpec((tm, tn), lambda i, j: (i, j)),
        out_shape=jax.ShapeDtypeStruct((m, n), BF16),
        compiler_params=pltpu.CompilerParams(
            dimension_semantics=("parallel", "arbitrary"),
            vmem_limit_bytes=_vmem_limit(blocks, 0, 2 * _nbytes((tm, tn), F32)),
        ),
        name=name,
    )(a, ss, w)


def _matmul_resid_kernel(a1_ref, a2_ref, w1_ref, w2_ref, r_ref, o_ref, ob_ref, ss_ref):
    acc = jnp.dot(a1_ref[...], w1_ref[...], preferred_element_type=F32)
    acc = acc + jnp.dot(a2_ref[...], w2_ref[...], preferred_element_type=F32)
    h = r_ref[...] + acc
    o_ref[...] = h
    ob_ref[...] = h.astype(BF16)
    ss_ref[...] = jnp.broadcast_to(jnp.sum(h * h, axis=1, keepdims=True), ss_ref.shape)


def _matmul_resid(a1, a2, a_blk, w1, w2, w_blk, resid, *, tm=1024, tn=512, name):
    (a1, a1c), (a2, a2c) = a1, a2
    (w1, w1r), (w2, w2r) = w1, w2
    m, n = resid.shape
    tm = _pick_tile(m, tm)
    tn = _pick_tile(n, tn)
    blocks = (2 * _nbytes((tm, a_blk), BF16) + 2 * _nbytes((w_blk, tn), BF16)
              + 2 * _nbytes((tm, tn), F32) + _nbytes((tm, tn), BF16) + _nbytes((tm, V7X_LANES), F32))
    return pl.pallas_call(
        _matmul_resid_kernel,
        grid=(m // tm, n // tn),
        in_specs=[
            pl.BlockSpec((tm, a_blk), lambda i, j: (i, a1c)),
            pl.BlockSpec((tm, a_blk), lambda i, j: (i, a2c)),
            pl.BlockSpec((w_blk, tn), lambda i, j: (w1r, j)),
            pl.BlockSpec((w_blk, tn), lambda i, j: (w2r, j)),
            pl.BlockSpec((tm, tn), lambda i, j: (i, j)),
        ],
        out_specs=[
            pl.BlockSpec((tm, tn), lambda i, j: (i, j)),
            pl.BlockSpec((tm, tn), lambda i, j: (i, j)),
            pl.BlockSpec((tm, V7X_LANES), lambda i, j: (i, j)),
        ],
        out_shape=[
            jax.ShapeDtypeStruct((m, n), F32),
            jax.ShapeDtypeStruct((m, n), BF16),
            jax.ShapeDtypeStruct((m, (n // tn) * V7X_LANES), F32),
        ],
        compiler_params=pltpu.CompilerParams(
            dimension_semantics=("parallel", "arbitrary"),
            vmem_limit_bytes=_vmem_limit(blocks, 0, 3 * _nbytes((tm, tn), F32)),
        ),
        name=name,
    )(a1, a2, w1, w2, resid)


def _gate_cumsum_kernel(f_ref, b_ref, o_ref, *, n_heads):
    z = f_ref[...] + b_ref[...]
    x = jnp.minimum(z, 0.0) - jnp.log1p(jnp.exp(-jnp.abs(z)))
    s_len = x.shape[0]
    row = lax.broadcasted_iota(jnp.int32, x.shape, 0)
    d = 1
    while d < s_len:
        x = x + jnp.where(row >= d, pltpu.roll(x, d, axis=0), 0.0)
        d *= 2
    c2 = x * (-LOG2E)
    hi = c2.astype(BF16)
    r1 = c2 - hi.astype(F32)
    mid = r1.astype(BF16)
    lo = (r1 - mid.astype(F32)).astype(BF16)
    lane = lax.broadcasted_iota(jnp.int32, x.shape, 1)
    mid_s = pltpu.roll(mid.astype(F32), n_heads, axis=1)
    lo_s = pltpu.roll(lo.astype(F32), 2 * n_heads, axis=1)
    out = jnp.where(lane < n_heads, hi.astype(F32),
                    jnp.where(lane < 2 * n_heads, mid_s, jnp.where(lane < 3 * n_heads, lo_s, 0.0)))
    o_ref[...] = out.astype(BF16)


def _gate_cumsum(f_logit, b_pad, batch, seq, n_heads):
    nf = f_logit.shape[1]
    assert 3 * n_heads <= nf
    return pl.pallas_call(
        functools.partial(_gate_cumsum_kernel, n_heads=n_heads),
        grid=(batch,),
        in_specs=[
            pl.BlockSpec((seq, nf), lambda b: (b, 0)),
            pl.BlockSpec((1, nf), lambda b: (0, 0)),
        ],
        out_specs=pl.BlockSpec((seq, nf), lambda b: (b, 0)),
        out_shape=jax.ShapeDtypeStruct((batch * seq, nf), BF16),
        compiler_params=pltpu.CompilerParams(
            dimension_semantics=("parallel",),
            vmem_limit_bytes=_vmem_limit(2 * _nbytes((seq, nf), F32), 0, 8 * _nbytes((seq, nf), F32)),
        ),
        name="gate_cumsum",
    )(f_logit, b_pad)


def _bias_tile_kernel(rb_ref, o_ref, *, tile, n_heads):
    h = pl.program_id(0)
    d = pl.program_id(1)
    half, max_exact, thresholds = _t5_thresholds()
    t = lax.broadcasted_iota(jnp.int32, (tile, tile), 0)
    s = lax.broadcasted_iota(jnp.int32, (tile, tile), 1)
    rel = s - t - d * tile
    n = jnp.abs(rel)
    large = jnp.full((tile, tile), max_exact, jnp.int32)
    for thr in thresholds:
        large = large + (n >= thr).astype(jnp.int32)
    idx = jnp.where(n < max_exact, n, large) + jnp.where(rel > 0, half, 0)
    val = jnp.zeros((tile, tile), F32)
    for b in range(N_BUCKETS):
        val = jnp.where(idx == b, rb_ref[b * n_heads + h], val)
    shift = int(math.log2(CHUNK))
    allowed = (s >> shift) <= ((t >> shift) + d * tile)
    o_ref[0, 0] = jnp.where(allowed, val * LOG2E, NEG_INF)


def _bias_tiles(rel_bias, tile):
    n_heads = rel_bias.shape[1]
    return pl.pallas_call(
        functools.partial(_bias_tile_kernel, tile=tile, n_heads=n_heads),
        grid=(n_heads, 3),
        in_specs=[pl.BlockSpec(memory_space=pltpu.SMEM)],
        out_specs=pl.BlockSpec((1, 1, tile, tile), lambda h, d: (h, d, 0, 0)),
        out_shape=jax.ShapeDtypeStruct((n_heads, 3, tile, tile), F32),
        compiler_params=pltpu.CompilerParams(
            dimension_semantics=("parallel", "parallel"),
            vmem_limit_bytes=_vmem_limit(_nbytes((tile, tile), F32), 0, 8 * _nbytes((tile, tile), F32)),
        ),
        name="t5_bias_tiles",
    )(rel_bias.reshape(-1))


def _lane_tile(x, reps):
    return x if reps == 1 else jnp.concatenate([x] * reps, axis=1)


def _qk(q, k):
    return lax.dot_general(q, k, (((1,), (1,)), ((), ())), preferred_element_type=F32)


def _pipelined_tiles(n_tiles, score, softmax, pv):
    if n_tiles == 0:
        return
    score(0, 0)
    if n_tiles == 1:
        softmax(0, 0)
        pv(0, 0)
        return
    score(1, 1)
    softmax(0, 0)
    n_pairs = (n_tiles - 2) // 2

    def pair(t, carry):
        n = 2 * t
        score(n + 2, 0)
        softmax(n + 1, 1)
        pv(n, 0)
        score(n + 3, 1)
        softmax(n + 2, 0)
        pv(n + 1, 1)
        return carry

    lax.fori_loop(0, n_pairs, pair, 0)
    last = n_tiles - 1
    if n_tiles % 2 == 0:
        softmax(last, 1)
        pv(last - 1, 0)
        pv(last, 1)
    else:
        score(last, 0)
        softmax(last - 1, 1)
        pv(last - 2, 0)
        softmax(last, 0)
        pv(last - 1, 1)
        pv(last, 0)


def _causal_tile_lists(nq):
    pairs = [(i, j) for i in range(nq) for j in range(i)]
    qi = jnp.asarray([p[0] for p in pairs] or [0], jnp.int32)
    kj = jnp.asarray([p[1] for p in pairs] or [0], jnp.int32)
    return len(pairs), qi, kj


def _fox_kernel(qi_ref, kj_ref, q_ref, k_ref, v_ref, cx_ref, o_ref, qx_ref, kx_ref, vx_ref,
                s0_ref, s1_ref, p0_ref, p1_ref, a0_ref, a1_ref, m_ref, acc_ref,
                *, tile, scale, n_heads, n_lower):
    s_bufs, p_bufs, a_bufs = (s0_ref, s1_ref), (p0_ref, p1_ref), (a0_ref, a1_ref)
    h = pl.program_id(1)
    hd = HEAD_DIM
    seq = q_ref.shape[0]
    nq = seq // tile

    lane = lax.broadcasted_iota(jnp.int32, (seq, cx_ref.shape[1]), 1)
    sel = (lane == h) | (lane == h + n_heads) | (lane == h + 2 * n_heads)
    qx_ref[:, :hd] = (q_ref[...].astype(F32) * (scale * LOG2E)).astype(BF16)
    qx_ref[:, hd:] = jnp.where(sel, 1.0, 0.0).astype(BF16)
    kx_ref[:, :hd] = k_ref[...]
    kx_ref[:, hd:] = cx_ref[...]
    vx_ref[:, :hd] = v_ref[...]
    vx_ref[:, hd:] = jnp.ones((seq, vx_ref.shape[1] - hd), BF16)
    m_ref[...] = jnp.full(m_ref.shape, NEG_INF, F32)
    acc_ref[...] = jnp.zeros(acc_ref.shape, F32)

    def rows_of(block):
        return pl.ds(pl.multiple_of(block * tile, tile), tile)

    def score_stage(qb, kb, slot):
        s_bufs[slot][...] = _qk(qx_ref[rows_of(qb), :], kx_ref[rows_of(kb), :])

    def softmax_stage(qb, slot, masked):
        s = s_bufs[slot][...]
        if masked:
            row = lax.broadcasted_iota(jnp.int32, s.shape, 0)
            col = lax.broadcasted_iota(jnp.int32, s.shape, 1)
            s = jnp.where(col <= row, s, NEG_INF)
        m_prev = m_ref[rows_of(qb), :]
        m_new = jnp.maximum(m_prev, jnp.max(s, axis=1, keepdims=True))
        a_bufs[slot][...] = jnp.exp2(m_prev - m_new)
        p_bufs[slot][...] = jnp.exp2(s - _lane_tile(m_new, tile // V7X_LANES)).astype(BF16)
        m_ref[rows_of(qb), :] = m_new

    def pv_stage(qb, kb, slot):
        pv = jnp.dot(p_bufs[slot][...], vx_ref[rows_of(kb), :], preferred_element_type=F32)
        acc = acc_ref[rows_of(qb), :]
        acc_ref[rows_of(qb), :] = _lane_tile(a_bufs[slot][...], acc.shape[1] // V7X_LANES) * acc + pv

    _pipelined_tiles(
        n_lower,
        lambda n, slot: score_stage(qi_ref[n], kj_ref[n], slot),
        lambda n, slot: softmax_stage(qi_ref[n], slot, False),
        lambda n, slot: pv_stage(qi_ref[n], kj_ref[n], slot))
    _pipelined_tiles(
        nq,
        lambda n, slot: score_stage(n, n, slot),
        lambda n, slot: softmax_stage(n, slot, True),
        lambda n, slot: pv_stage(n, n, slot))

    def finish(qb, carry):
        acc = acc_ref[rows_of(qb), :]
        o_ref[rows_of(qb), :] = (acc[:, :hd] / acc[:, hd:2 * hd]).astype(o_ref.dtype)
        return carry

    lax.fori_loop(0, nq, finish, 0)


def _fox_attention(proj, cx, batch, seq, n_heads, tile):
    nq = seq // tile
    hd = HEAD_DIM
    nx = cx.shape[1]
    n_lower, qi, kj = _causal_tile_lists(nq)
    blocks = 4 * _nbytes((seq, hd), BF16) + _nbytes((seq, nx), BF16)
    scratch = (3 * _nbytes((seq, hd + nx), BF16) + 2 * _nbytes((tile, tile), F32)
               + 2 * _nbytes((tile, tile), BF16) + 2 * _nbytes((tile, V7X_LANES), F32)
               + _nbytes((seq, V7X_LANES), F32) + _nbytes((seq, 2 * hd), F32))
    smem = pl.BlockSpec(memory_space=pltpu.SMEM)
    return pl.pallas_call(
        functools.partial(_fox_kernel, tile=tile, scale=hd ** -0.5, n_heads=n_heads, n_lower=n_lower),
        grid=(batch, n_heads),
        in_specs=[
            smem, smem,
            pl.BlockSpec((seq, hd), lambda b, h: (b, h)),
            pl.BlockSpec((seq, hd), lambda b, h: (b, n_heads + h)),
            pl.BlockSpec((seq, hd), lambda b, h: (b, 2 * n_heads + h)),
            pl.BlockSpec((seq, nx), lambda b, h: (b, 0)),
        ],
        out_specs=pl.BlockSpec((seq, hd), lambda b, h: (b, h)),
        out_shape=jax.ShapeDtypeStruct((batch * seq, n_heads * hd), BF16),
        scratch_shapes=[
            pltpu.VMEM((seq, hd + nx), BF16),
            pltpu.VMEM((seq, hd + nx), BF16),
            pltpu.VMEM((seq, 2 * hd), BF16),
            pltpu.VMEM((tile, tile), F32), pltpu.VMEM((tile, tile), F32),
            pltpu.VMEM((tile, tile), BF16), pltpu.VMEM((tile, tile), BF16),
            pltpu.VMEM((tile, V7X_LANES), F32), pltpu.VMEM((tile, V7X_LANES), F32),
            pltpu.VMEM((seq, V7X_LANES), F32),
            pltpu.VMEM((seq, 2 * hd), F32),
        ],
        compiler_params=pltpu.CompilerParams(
            dimension_semantics=("parallel", "arbitrary"),
            vmem_limit_bytes=_vmem_limit(blocks, scratch, 12 * _nbytes((tile, tile), F32)),
        ),
        name="fox_attention",
    )(qi, kj, proj, proj, proj, cx)


def _diff_kernel(blk_ref, qs_ref, kj_ref, q_ref, k_ref, v_ref, bias_ref, lq1_ref, lk1_ref, lq2_ref, lk2_ref,
                 g_ref, o_ref, qq_ref, s0_ref, s1_ref, p0_ref, p1_ref, a0_ref, a1_ref, m_ref, l_ref, acc_ref,
                 *, tile, scale, lambda_init, n_tiles, n_slots):
    s_bufs, p_bufs, a_bufs = (s0_ref, s1_ref), (p0_ref, p1_ref), (a0_ref, a1_ref)
    grp = pl.program_id(2)
    hd = HEAD_DIM
    two = 2 * tile

    def rows_of(block, size=tile):
        return pl.ds(pl.multiple_of(block * size, size), size)

    def stage_queries(slot, carry):
        qf = q_ref[rows_of(blk_ref[grp, slot]), :].astype(F32) * (scale * LOG2E)
        lane = lax.broadcasted_iota(jnp.int32, qf.shape, 1)
        base = pl.multiple_of(slot * two, two)
        qq_ref[pl.ds(base, tile), :] = jnp.where(lane < hd, qf, 0.0).astype(BF16)
        qq_ref[pl.ds(base + tile, tile), :] = jnp.where(lane >= hd, qf, 0.0).astype(BF16)
        return carry

    lax.fori_loop(0, n_slots, stage_queries, 0)
    m_ref[...] = jnp.full(m_ref.shape, NEG_INF, F32)
    l_ref[...] = jnp.zeros(l_ref.shape, F32)
    acc_ref[...] = jnp.zeros(acc_ref.shape, F32)

    def score_stage(n, slot):
        s_bufs[slot][...] = _qk(qq_ref[rows_of(qs_ref[grp, n], two), :], k_ref[rows_of(kj_ref[grp, n]), :])

    def softmax_stage(n, slot):
        qs = qs_ref[grp, n]
        bias = bias_ref[0, jnp.minimum(blk_ref[grp, qs] - kj_ref[grp, n], 2)]
        for c in range(2):
            rows = slice(c * tile, (c + 1) * tile)
            state = pl.ds(pl.multiple_of(qs * two + c * tile, tile), tile)
            s = s_bufs[slot][rows, :] + bias
            m_prev = m_ref[state, :]
            m_new = jnp.maximum(m_prev, jnp.max(s, axis=1, keepdims=True))
            alpha = jnp.exp2(m_prev - m_new)
            p = jnp.exp2(s - _lane_tile(m_new, tile // V7X_LANES))
            l_ref[state, :] = alpha * l_ref[state, :] + jnp.sum(p, axis=1, keepdims=True)
            a_bufs[slot][rows, :] = alpha
            p_bufs[slot][rows, :] = p.astype(BF16)
            m_ref[state, :] = m_new

    def pv_stage(n, slot):
        state = rows_of(qs_ref[grp, n], two)
        pv = jnp.dot(p_bufs[slot][...], v_ref[rows_of(kj_ref[grp, n]), :], preferred_element_type=F32)
        acc = acc_ref[state, :]
        acc_ref[state, :] = _lane_tile(a_bufs[slot][...], acc.shape[1] // V7X_LANES) * acc + pv

    _pipelined_tiles(n_tiles, score_stage, softmax_stage, pv_stage)

    lam = (jnp.exp(jnp.sum(lq1_ref[...] * lk1_ref[...], axis=1, keepdims=True))
           - jnp.exp(jnp.sum(lq2_ref[...] * lk2_ref[...], axis=1, keepdims=True))
           + lambda_init)

    def finish(slot, carry):
        base = pl.multiple_of(slot * two, two)
        c1, c2 = pl.ds(base, tile), pl.ds(base + tile, tile)
        reps = acc_ref.shape[1] // V7X_LANES
        out = (acc_ref[c1, :] / _lane_tile(l_ref[c1, :], reps)
               - lam * (acc_ref[c2, :] / _lane_tile(l_ref[c2, :], reps)))
        ms = jnp.mean(out * out, axis=-1, keepdims=True)
        y = out * lax.rsqrt(ms + SUBLN_EPS) * g_ref[...]
        o_ref[rows_of(blk_ref[grp, slot]), :] = (y * (1.0 - lambda_init)).astype(o_ref.dtype)
        return carry

    lax.fori_loop(0, n_slots, finish, 0)


def _balanced_causal_groups(nq):
    n_groups = 2 if nq % 4 == 0 else 1
    groups = [[] for _ in range(n_groups)]
    for p in range((nq + 1) // 2):
        for blk in sorted({p, nq - 1 - p}):
            groups[p % n_groups].append(blk)
    slots = [[s for s, blk in enumerate(g) for _ in range(blk + 1)] for g in groups]
    keys = [[j for blk in g for j in range(blk + 1)] for g in groups]
    assert len({len(s) for s in slots}) == 1 and len({len(g) for g in groups}) == 1
    return groups, slots, keys


def _diff_attention(proj, bias, lq1, lk1, lq2, lk2, g_subln, batch, seq, n_heads, col0, tile, lambda_init):
    nq = seq // tile
    hd2 = 2 * HEAD_DIM
    c0 = col0 // hd2
    groups, slots, keys = _balanced_causal_groups(nq)
    n_groups, n_slots, n_tiles = len(groups), len(groups[0]), len(slots[0])
    vec = lambda: pl.BlockSpec((1, HEAD_DIM), lambda h, b, g: (0, 0))
    smem = pl.BlockSpec(memory_space=pltpu.SMEM)
    blocks = 4 * _nbytes((seq, hd2), BF16) + _nbytes((3, tile, tile), F32)
    state_rows = 2 * n_slots * tile
    scratch = (_nbytes((state_rows, hd2), BF16) + 2 * _nbytes((2 * tile, tile), F32)
               + 2 * _nbytes((2 * tile, tile), BF16) + 2 * _nbytes((2 * tile, V7X_LANES), F32)
               + 2 * _nbytes((state_rows, V7X_LANES), F32) + _nbytes((state_rows, hd2), F32))
    return pl.pallas_call(
        functools.partial(_diff_kernel, tile=tile, scale=HEAD_DIM ** -0.5, lambda_init=lambda_init,
                          n_tiles=n_tiles, n_slots=n_slots),
        grid=(n_heads, batch, n_groups),
        in_specs=[
            smem, smem, smem,
            pl.BlockSpec((seq, hd2), lambda h, b, g: (b, c0 + h)),
            pl.BlockSpec((seq, hd2), lambda h, b, g: (b, c0 + n_heads + h)),
            pl.BlockSpec((seq, hd2), lambda h, b, g: (b, c0 + 2 * n_heads + h)),
            pl.BlockSpec((1, 3, tile, tile), lambda h, b, g: (h, 0, 0, 0)),
            vec(), vec(), vec(), vec(),
            pl.BlockSpec((1, hd2), lambda h, b, g: (0, 0)),
        ],
        out_specs=pl.BlockSpec((seq, hd2), lambda h, b, g: (b, h)),
        out_shape=jax.ShapeDtypeStruct((batch * seq, n_heads * hd2), BF16),
        scratch_shapes=[
            pltpu.VMEM((state_rows, hd2), BF16),
            pltpu.VMEM((2 * tile, tile), F32), pltpu.VMEM((2 * tile, tile), F32),
            pltpu.VMEM((2 * tile, tile), BF16), pltpu.VMEM((2 * tile, tile), BF16),
            pltpu.VMEM((2 * tile, V7X_LANES), F32), pltpu.VMEM((2 * tile, V7X_LANES), F32),
            pltpu.VMEM((state_rows, V7X_LANES), F32),
            pltpu.VMEM((state_rows, V7X_LANES), F32),
            pltpu.VMEM((state_rows, hd2), F32),
        ],
        compiler_params=pltpu.CompilerParams(
            dimension_semantics=("parallel", "parallel", "arbitrary"),
            vmem_limit_bytes=_vmem_limit(blocks, scratch, 12 * _nbytes((tile, tile), F32)),
        ),
        name="diff_attention",
    )(jnp.asarray(groups, jnp.int32), jnp.asarray(slots, jnp.int32), jnp.asarray(keys, jnp.int32),
      proj, proj, proj, bias, lq1, lk1, lq2, lk2, g_subln)


def _cross_kernel(q_ref, k_ref, v_ref, o_ref, *, n_heads, scale):
    dh = q_ref.shape[1] // n_heads
    for h in range(n_heads):
        cols = slice(h * dh, (h + 1) * dh)
        s = _qk(q_ref[:, cols], k_ref[:, cols]) * scale
        m = jnp.max(s, axis=1, keepdims=True)
        p = jnp.exp(s - m)
        p = p / jnp.sum(p, axis=1, keepdims=True)
        o_ref[:, cols] = jnp.dot(p.astype(BF16), v_ref[:, cols], preferred_element_type=F32).astype(o_ref.dtype)


def _cross_attention(q, kv, batch, seq, n_mem, tile):
    d = q.shape[1]
    nq = seq // tile
    blocks = 2 * _nbytes((tile, d), BF16) + 2 * _nbytes((n_mem, d), BF16)
    return pl.pallas_call(
        functools.partial(_cross_kernel, n_heads=H_MEM, scale=(d // H_MEM) ** -0.5),
        grid=(batch, nq),
        in_specs=[
            pl.BlockSpec((tile, d), lambda b, i: (b * nq + i, 0)),
            pl.BlockSpec((n_mem, d), lambda b, i: (b, 0)),
            pl.BlockSpec((n_mem, d), lambda b, i: (b, 1)),
        ],
        out_specs=pl.BlockSpec((tile, d), lambda b, i: (b * nq + i, 0)),
        out_shape=jax.ShapeDtypeStruct((batch * seq, d), BF16),
        compiler_params=pltpu.CompilerParams(
            dimension_semantics=("parallel", "arbitrary"),
            vmem_limit_bytes=_vmem_limit(blocks, 0, 8 * _nbytes((tile, n_mem), F32) + _nbytes((tile, d), F32)),
        ),
        name="cross_attention",
    )(q, kv, kv)


FINAL_NORM_ROWS = 32


def _mlp_kernel(a_ref, ss_ref, hs_ref, wu_ref, wd_ref, gf_ref, o_ref, *, eps):
    f = pl.program_id(1)
    rows = hs_ref.shape[0]

    @pl.when(f == 0)
    def _():
        o_ref[...] = jnp.zeros(o_ref.shape, F32)

    r0 = pl.multiple_of(f * rows, rows)
    o_ref[pl.ds(r0, rows), :] += hs_ref[...]

    r = _row_scale(ss_ref, a_ref.shape[1], eps)
    u = jnp.dot(a_ref[...], wu_ref[...], preferred_element_type=F32)
    act = (jnp.square(jnp.maximum(u, 0.0)) * _lane_tile(r * r, u.shape[1] // V7X_LANES)).astype(BF16)
    o_ref[...] += jnp.dot(act, wd_ref[...], preferred_element_type=F32)

    @pl.when(f == pl.num_programs(1) - 1)
    def _():
        def norm_rows(c, carry):
            c0 = pl.multiple_of(c * FINAL_NORM_ROWS, FINAL_NORM_ROWS)
            y = o_ref[pl.ds(c0, FINAL_NORM_ROWS), :]
            ms = jnp.mean(y * y, axis=-1, keepdims=True)
            o_ref[pl.ds(c0, FINAL_NORM_ROWS), :] = y * lax.rsqrt(ms + eps) * gf_ref[...]
            return carry

        lax.fori_loop(0, o_ref.shape[0] // FINAL_NORM_ROWS, norm_rows, 0)


def _mlp(a, ss, h, w_up, w_down, g_final, *, eps=NORM_EPS, tm=512, tf=512):
    m, d = h.shape
    dff = w_up.shape[1]
    tm = _pick_tile(m, tm)
    tf = _pick_tile(dff, tf)
    nf = dff // tf
    rows = tm // nf
    assert rows * nf == tm and rows % 8 == 0 and tm % FINAL_NORM_ROWS == 0
    nss = ss.shape[1]
    blocks = (2 * _nbytes((d, tf), BF16) + _nbytes((tm, d), BF16) + _nbytes((tm, nss), F32)
              + _nbytes((rows, d), F32) + _nbytes((tm, d), F32))
    return pl.pallas_call(
        functools.partial(_mlp_kernel, eps=eps),
        grid=(m // tm, nf),
        in_specs=[
            pl.BlockSpec((tm, d), lambda i, f: (i, 0)),
            pl.BlockSpec((tm, nss), lambda i, f: (i, 0)),
            pl.BlockSpec((rows, d), lambda i, f: (i * nf + f, 0)),
            pl.BlockSpec((d, tf), lambda i, f: (0, f)),
            pl.BlockSpec((tf, d), lambda i, f: (f, 0)),
            pl.BlockSpec((1, d), lambda i, f: (0, 0)),
        ],
        out_specs=pl.BlockSpec((tm, d), lambda i, f: (i, 0)),
        out_shape=jax.ShapeDtypeStruct((m, d), F32),
        compiler_params=pltpu.CompilerParams(
            dimension_semantics=("parallel", "arbitrary"),
            vmem_limit_bytes=_vmem_limit(blocks, 0, 4 * _nbytes((tm, tf), F32)),
        ),
        name="mlp_final_norm",
    )(a, ss, h, w_up, w_down, g_final.reshape(1, d))


def _attn_tile(seq):
    return _pick_tile(seq, 512)


def _layer(h, mem, l, g_mix, w_in, b_forget, lambda_q1, lambda_k1, lambda_q2, lambda_k2, g_subln, rel_bias,
           w_out, g_cross, g_mem, wq_mem, wk_mem, wv_mem, wo_mem, g_mlp, w_up, w_down, g_final):
    batch, seq, d = h.shape
    n_mem = mem.shape[1]
    h_fox = b_forget.shape[-1]
    h_diff = rel_bias.shape[1]
    w_fox = h_fox * HEAD_DIM
    tile = _attn_tile(seq)
    x2 = h.reshape(batch * seq, d)

    def gained(g, w):
        return (g.astype(F32)[:, None] * w).astype(BF16)

    wi = w_in[l]
    f0 = 3 * w_fox
    w_qkv = jnp.concatenate([wi[:, :f0].astype(BF16), wi[:, f0 + h_fox:].astype(BF16)], axis=1)
    n_gate = -(-h_fox // V7X_LANES) * V7X_LANES
    w_gate = jnp.pad(wi[:, f0:f0 + h_fox].astype(BF16), ((0, 0), (0, n_gate - h_fox)))
    b_gate = jnp.pad(b_forget[l].astype(F32), (0, n_gate - h_fox)).reshape(1, n_gate)

    proj, f_logit = _in_proj(x2, g_mix[l], w_qkv, w_gate)

    cx = _gate_cumsum(f_logit, b_gate, batch, seq, h_fox)
    fox = _fox_attention(proj, cx, batch, seq, h_fox, tile)

    lambda_init = 0.8 - 0.6 * math.exp(-0.3 * l)
    bias = _bias_tiles(rel_bias.astype(F32), tile)
    row = lambda v: v[l].astype(F32).reshape(1, -1)
    diff = _diff_attention(proj, bias, row(lambda_q1), row(lambda_k1), row(lambda_q2), row(lambda_k2),
                           row(g_subln), batch, seq, h_diff, 3 * w_fox, tile, lambda_init)

    w_o = w_out[l].astype(BF16)
    h1, h1b, ss1 = _matmul_resid((fox, 0), (diff, 0), w_fox, (w_o, 0), (w_o, 1), w_fox, x2, name="out_proj")

    q = _scaled_matmul(h1b, ss1, gained(g_cross[l], wq_mem[l]), name="cross_q_proj")
    w_kv = jnp.concatenate([wk_mem[l].astype(BF16), wv_mem[l].astype(BF16)], axis=1)
    kv = _norm_matmul(mem.reshape(batch * n_mem, d), g_mem[l], w_kv, name="cross_kv_proj")
    o = _cross_attention(q, kv, batch, seq, n_mem, tile)
    w_om = wo_mem[l].astype(BF16)
    half = d // 2
    h2, h2b, ss2 = _matmul_resid((o, 0), (o, 1), half, (w_om, 0), (w_om, 1), half, h1, name="cross_o_proj")

    out = _mlp(h2b, ss2, h2, gained(g_mlp[l], w_up[l]), w_down[l].astype(BF16), g_final)
    return out.reshape(batch, seq, d)


def kernel(x, mem, g_mix, w_in, b_forget, lambda_q1, lambda_k1, lambda_q2, lambda_k2, g_subln, rel_bias, w_out,
           g_cross, g_mem, wq_mem, wk_mem, wv_mem, wo_mem, g_mlp, w_up, w_down, g_final):
    depth = g_mix.shape[0]
    assert depth == 1, "the fused MLP epilogue applies the final norm; only depth 1 is supported"
    return _layer(x, mem, 0, g_mix, w_in, b_forget, lambda_q1, lambda_k1, lambda_q2, lambda_k2, g_subln,
                  rel_bias, w_out, g_cross, g_mem, wq_mem, wk_mem, wv_mem, wo_mem, g_mlp, w_up, w_down,
                  g_final)
```

```python
import functools
import math

import jax
import jax.numpy as jnp
from jax import lax
from jax.experimental import pallas as pl
from jax.experimental.pallas import tpu as pltpu

HEAD_DIM = 128
CHUNK = 64
N_BUCKETS = 32
MAX_DISTANCE = 128
H_MEM = 4
NORM_EPS = 1e-6
SUBLN_EPS = 1e-5
NEG_INF = -1e30
LOG2E = math.log2(math.e)

V7X_LANES = 128
V7X_VMEM_BYTES = 64 * 1024 * 1024
V7X_VMEM_REQUEST_CAP = V7X_VMEM_BYTES - 8 * 1024 * 1024

F32 = jnp.float32
BF16 = jnp.bfloat16


def _vmem_limit(block_bytes, scratch_bytes=0, temp_bytes=0):
    need = 2 * block_bytes + scratch_bytes + temp_bytes + (4 << 20)
    return int(min(max(need, 16 << 20), V7X_VMEM_REQUEST_CAP))


def _nbytes(shape, dtype):
    return math.prod(shape) * jnp.dtype(dtype).itemsize


def _pick_tile(n, target):
    if n <= target:
        return n
    t = target
    while t >= V7X_LANES:
        if n % t == 0:
            return t
        t -= V7X_LANES
    return n


def _t5_thresholds():
    half = N_BUCKETS // 2
    max_exact = half // 2
    steps = half - max_exact
    ratio = MAX_DISTANCE // max_exact
    thr = []
    for k in range(1, steps):
        n = max_exact
        while n ** steps < (ratio ** k) * (max_exact ** steps):
            n += 1
        thr.append(n)
    return half, max_exact, tuple(thr)


def _norm_matmul_kernel(x_ref, g_ref, w_ref, *rest, eps, with_gate):
    if with_gate:
        wf_ref, o_ref, f_ref, a_ref = rest
    else:
        o_ref, a_ref = rest

    @pl.when(pl.program_id(1) == 0)
    def _():
        x = x_ref[...]
        ms = jnp.mean(x * x, axis=-1, keepdims=True)
        a = (x * lax.rsqrt(ms + eps) * g_ref[...]).astype(BF16)
        a_ref[...] = a
        if with_gate:
            f_ref[...] = jnp.dot(a, wf_ref[...], preferred_element_type=F32)

    o_ref[...] = jnp.dot(a_ref[...], w_ref[...], preferred_element_type=F32).astype(o_ref.dtype)


def _norm_matmul(x, g, w, wf=None, *, eps=NORM_EPS, tm=512, tn=1024, name):
    m, k = x.shape
    n = w.shape[1]
    tm = _pick_tile(m, tm)
    tn = _pick_tile(n, tn)
    with_gate = wf is not None
    in_specs = [
        pl.BlockSpec((tm, k), lambda i, j: (i, 0), pipeline_mode=pl.Buffered(1)),
        pl.BlockSpec((1, k), lambda i, j: (0, 0)),
        pl.BlockSpec((k, tn), lambda i, j: (0, j)),
    ]
    args = [x, g.reshape(1, k), w]
    out_shape = [jax.ShapeDtypeStruct((m, n), BF16)]
    out_specs = [pl.BlockSpec((tm, tn), lambda i, j: (i, j))]
    blocks = _nbytes((k, tn), BF16) + _nbytes((tm, tn), BF16)
    if with_gate:
        nf = wf.shape[1]
        in_specs.append(pl.BlockSpec((k, nf), lambda i, j: (0, 0)))
        args.append(wf)
        out_shape.append(jax.ShapeDtypeStruct((m, nf), F32))
        out_specs.append(pl.BlockSpec((tm, nf), lambda i, j: (i, 0)))
        blocks += _nbytes((k, nf), BF16) + _nbytes((tm, nf), F32)
    res = pl.pallas_call(
        functools.partial(_norm_matmul_kernel, eps=eps, with_gate=with_gate),
        grid=(m // tm, n // tn),
        in_specs=in_specs,
        out_specs=out_specs,
        out_shape=out_shape,
        scratch_shapes=[pltpu.VMEM((tm, k), BF16)],
        compiler_params=pltpu.CompilerParams(
            dimension_semantics=("parallel", "arbitrary"),
            vmem_limit_bytes=_vmem_limit(
                blocks, _nbytes((tm, k), F32) + _nbytes((tm, k), BF16), 2 * _nbytes((tm, k), F32)),
        ),
        name=name,
    )(*args)
    return res if with_gate else res[0]


def _row_scale(ss_ref, k, eps):
    ss = ss_ref[:, :V7X_LANES]
    for c in range(1, ss_ref.shape[1] // V7X_LANES):
        ss = ss + ss_ref[:, c * V7X_LANES:(c + 1) * V7X_LANES]
    return lax.rsqrt(ss * (1.0 / k) + eps)


def _in_proj_kernel(xs_ref, g_ref, w_ref, wf_ref, o_ref, f_ref, a0_ref, a1_ref, fs0_ref, fs1_ref, *, eps):
    t = pl.program_id(0)
    j = pl.program_id(1)
    rows = xs_ref.shape[0]

    def step(a_cur, fs_cur, a_nxt, fs_nxt):
        def norm_slice():
            x = xs_ref[...]
            ms = jnp.mean(x * x, axis=-1, keepdims=True)
            a = (x * lax.rsqrt(ms + eps) * g_ref[...]).astype(BF16)
            r0 = pl.multiple_of(j * rows, rows)
            a_nxt[pl.ds(r0, rows), :] = a
            fs_nxt[pl.ds(r0, rows), :] = jnp.dot(a, wf_ref[...], preferred_element_type=F32)

        @pl.when((t > 0) & (j == 0))
        def _():
            f_ref[...] = fs_cur[...]

        @pl.when(t == 0)
        def _():
            o_ref[...] = jnp.zeros(o_ref.shape, o_ref.dtype)
            f_ref[...] = jnp.zeros(f_ref.shape, f_ref.dtype)
            norm_slice()

        @pl.when(t > 0)
        def _():
            o_ref[...] = jnp.dot(a_cur[...], w_ref[...], preferred_element_type=F32).astype(o_ref.dtype)
            norm_slice()

    @pl.when((t & 1) == 0)
    def _():
        step(a1_ref, fs1_ref, a0_ref, fs0_ref)

    @pl.when((t & 1) == 1)
    def _():
        step(a0_ref, fs0_ref, a1_ref, fs1_ref)


def _in_proj(x, g, w, wf, *, eps=NORM_EPS, tm=1024, tn=768):
    m, k = x.shape
    n = w.shape[1]
    tm = _pick_tile(m, tm)
    tn = _pick_tile(n, tn)
    nj = n // tn
    n_rt = m // tm
    rows = tm // nj
    assert rows * nj == tm and rows % 16 == 0, (tm, nj)
    nf = wf.shape[1]
    blocks = (_nbytes((k, tn), BF16) + _nbytes((tm, tn), BF16) + _nbytes((k, nf), BF16)
              + _nbytes((tm, nf), F32) + _nbytes((rows, k), F32))
    scratch = 2 * _nbytes((tm, k), BF16) + 2 * _nbytes((tm, nf), F32)
    return pl.pallas_call(
        functools.partial(_in_proj_kernel, eps=eps),
        grid=(n_rt + 1, nj),
        in_specs=[
            pl.BlockSpec((rows, k), lambda t, j: (jnp.minimum(t, n_rt - 1) * nj + j, 0)),
            pl.BlockSpec((1, k), lambda t, j: (0, 0)),
            pl.BlockSpec((k, tn), lambda t, j: (0, j)),
            pl.BlockSpec((k, nf), lambda t, j: (0, 0)),
        ],
        out_specs=[
            pl.BlockSpec((tm, tn), lambda t, j: (jnp.where(t == 0, n_rt, t - 1), j)),
            pl.BlockSpec((tm, nf), lambda t, j: (jnp.where(t == 0, n_rt, t - 1), 0)),
        ],
        out_shape=[
            jax.ShapeDtypeStruct((m + tm, n), BF16),
            jax.ShapeDtypeStruct((m + tm, nf), F32),
        ],
        scratch_shapes=[pltpu.VMEM((tm, k), BF16), pltpu.VMEM((tm, k), BF16),
                        pltpu.VMEM((tm, nf), F32), pltpu.VMEM((tm, nf), F32)],
        compiler_params=pltpu.CompilerParams(
            dimension_semantics=("arbitrary", "arbitrary"),
            vmem_limit_bytes=_vmem_limit(blocks, scratch, 2 * _nbytes((tm, tn), F32)),
        ),
        name="in_proj",
    )(x, g.reshape(1, k), w, wf)


def _scaled_matmul_kernel(a_ref, ss_ref, w_ref, o_ref, *, eps):
    r = _row_scale(ss_ref, a_ref.shape[1], eps)
    acc = jnp.dot(a_ref[...], w_ref[...], preferred_element_type=F32)
    o_ref[...] = (_lane_tile(r, o_ref.shape[1] // V7X_LANES) * acc).astype(o_ref.dtype)


def _scaled_matmul(a, ss, w, *, eps=NORM_EPS, tm=1024, tn=1024, name):
    m, k = a.shape
    n = w.shape[1]
    tm = _pick_tile(m, tm)
    tn = _pick_tile(n, tn)
    nss = ss.shape[1]
    blocks = (_nbytes((tm, k), BF16) + _nbytes((tm, nss), F32) + _nbytes((k, tn), BF16)
              + _nbytes((tm, tn), BF16))
    return pl.pallas_call(
        functools.partial(_scaled_matmul_kernel, eps=eps),
        grid=(m // tm, n // tn),
        in_specs=[
            pl.BlockSpec((tm, k), lambda i, j: (i, 0)),
            pl.BlockSpec((tm, nss), lambda i, j: (i, 0)),
            pl.BlockSpec((k, tn), lambda i, j: (0, j)),
        ],
        out_specs=pl.BlockSpec((tm, tn), lambda i, j: (i, j)),
        out_shape=jax.ShapeDtypeStruct((m, n), BF16),
        compiler_params=pltpu.CompilerParams(
            dimension_semantics=("parallel", "arbitrary"),
            vmem_limit_bytes=_vmem_limit(blocks, 0, 2 * _nbytes((tm, tn), F32)),
        ),
        name=name,
    )(a, ss, w)


def _matmul_resid_kernel(a1_ref, a2_ref, w1_ref, w2_ref, r_ref, o_ref, ob_ref, ss_ref):
    acc = jnp.dot(a1_ref[...], w1_ref[...], preferred_element_type=F32)
    acc = acc + jnp.dot(a2_ref[...], w2_ref[...], preferred_element_type=F32)
    h = r_ref[...] + acc
    o_ref[...] = h
    ob_ref[...] = h.astype(BF16)
    ss_ref[...] = jnp.broadcast_to(jnp.sum(h * h, axis=1, keepdims=True), ss_ref.shape)


def _matmul_resid(a1, a2, a_blk, w1, w2, w_blk, resid, *, tm=1024, tn=512, name):
    (a1, a1c), (a2, a2c) = a1, a2
    (w1, w1r), (w2, w2r) = w1, w2
    m, n = resid.shape
    tm = _pick_tile(m, tm)
    tn = _pick_tile(n, tn)
    blocks = (2 * _nbytes((tm, a_blk), BF16) + 2 * _nbytes((w_blk, tn), BF16)
              + 2 * _nbytes((tm, tn), F32) + _nbytes((tm, tn), BF16) + _nbytes((tm, V7X_LANES), F32))
    return pl.pallas_call(
        _matmul_resid_kernel,
        grid=(m // tm, n // tn),
        in_specs=[
            pl.BlockSpec((tm, a_blk), lambda i, j: (i, a1c)),
            pl.BlockSpec((tm, a_blk), lambda i, j: (i, a2c)),
            pl.BlockSpec((w_blk, tn), lambda i, j: (w1r, j)),
            pl.BlockSpec((w_blk, tn), lambda i, j: (w2r, j)),
            pl.BlockSpec((tm, tn), lambda i, j: (i, j)),
        ],
        out_specs=[
            pl.BlockSpec((tm, tn), lambda i, j: (i, j)),
            pl.BlockSpec((tm, tn), lambda i, j: (i, j)),
            pl.BlockSpec((tm, V7X_LANES), lambda i, j: (i, j)),
        ],
        out_shape=[
            jax.ShapeDtypeStruct((m, n), F32),
            jax.ShapeDtypeStruct((m, n), BF16),
            jax.ShapeDtypeStruct((m, (n // tn) * V7X_LANES), F32),
        ],
        compiler_params=pltpu.CompilerParams(
            dimension_semantics=("parallel", "arbitrary"),
            vmem_limit_bytes=_vmem_limit(blocks, 0, 3 * _nbytes((tm, tn), F32)),
        ),
        name=name,
    )(a1, a2, w1, w2, resid)


def _gate_cumsum_kernel(f_ref, b_ref, o_ref, *, n_heads):
    z = f_ref[...] + b_ref[...]
    x = jnp.minimum(z, 0.0) - jnp.log1p(jnp.exp(-jnp.abs(z)))
    s_len = x.shape[0]
    row = lax.broadcasted_iota(jnp.int32, x.shape, 0)
    d = 1
    while d < s_len:
        x = x + jnp.where(row >= d, pltpu.roll(x, d, axis=0), 0.0)
        d *= 2
    c2 = x * (-LOG2E)
    hi = c2.astype(BF16)
    r1 = c2 - hi.astype(F32)
    mid = r1.astype(BF16)
    lo = (r1 - mid.astype(F32)).astype(BF16)
    lane = lax.broadcasted_iota(jnp.int32, x.shape, 1)
    mid_s = pltpu.roll(mid.astype(F32), n_heads, axis=1)
    lo_s = pltpu.roll(lo.astype(F32), 2 * n_heads, axis=1)
    out = jnp.where(lane < n_heads, hi.astype(F32),
                    jnp.where(lane < 2 * n_heads, mid_s, jnp.where(lane < 3 * n_heads, lo_s, 0.0)))
    o_ref[...] = out.astype(BF16)


def _gate_cumsum(f_logit, b_pad, batch, seq, n_heads):
    nf = f_logit.shape[1]
    assert 3 * n_heads <= nf
    return pl.pallas_call(
        functools.partial(_gate_cumsum_kernel, n_heads=n_heads),
        grid=(batch,),
        in_specs=[
            pl.BlockSpec((seq, nf), lambda b: (b, 0)),
            pl.BlockSpec((1, nf), lambda b: (0, 0)),
        ],
        out_specs=pl.BlockSpec((seq, nf), lambda b: (b, 0)),
        out_shape=jax.ShapeDtypeStruct((batch * seq, nf), BF16),
        compiler_params=pltpu.CompilerParams(
            dimension_semantics=("parallel",),
            vmem_limit_bytes=_vmem_limit(2 * _nbytes((seq, nf), F32), 0, 8 * _nbytes((seq, nf), F32)),
        ),
        name="gate_cumsum",
    )(f_logit, b_pad)


def _bias_tile_kernel(rb_ref, o_ref, *, tile, n_heads):
    h = pl.program_id(0)
    d = pl.program_id(1)
    half, max_exact, thresholds = _t5_thresholds()
    assert tile >= thresholds[-1], "tile 2 must lie wholly in the last (saturated) past bucket"

    @pl.when(d < 2)
    def _():
        t = lax.broadcasted_iota(jnp.int32, (tile, tile), 0)
        s = lax.broadcasted_iota(jnp.int32, (tile, tile), 1)
        rel = s - t - d * tile
        n = jnp.abs(rel)
        large = jnp.full((tile, tile), max_exact, jnp.int32)
        for thr in thresholds:
            large = large + (n >= thr).astype(jnp.int32)
        idx = jnp.where(n < max_exact, n, large) + jnp.where(rel > 0, half, 0)
        val = jnp.zeros((tile, tile), F32)
        for b in range(N_BUCKETS):
            val = jnp.where(idx == b, rb_ref[b * n_heads + h], val)
        shift = int(math.log2(CHUNK))
        allowed = (s >> shift) <= ((t >> shift) + d * tile)
        o_ref[0, 0] = jnp.where(allowed, val * LOG2E, NEG_INF)

    @pl.when(d == 2)
    def _():
        o_ref[0, 0] = jnp.full((tile, tile), rb_ref[(half - 1) * n_heads + h] * LOG2E, F32)


def _bias_tiles(rel_bias, tile):
    n_heads = rel_bias.shape[1]
    return pl.pallas_call(
        functools.partial(_bias_tile_kernel, tile=tile, n_heads=n_heads),
        grid=(n_heads, 3),
        in_specs=[pl.BlockSpec(memory_space=pltpu.SMEM)],
        out_specs=pl.BlockSpec((1, 1, tile, tile), lambda h, d: (h, d, 0, 0)),
        out_shape=jax.ShapeDtypeStruct((n_heads, 3, tile, tile), F32),
        compiler_params=pltpu.CompilerParams(
            dimension_semantics=("parallel", "parallel"),
            vmem_limit_bytes=_vmem_limit(_nbytes((tile, tile), F32), 0, 8 * _nbytes((tile, tile), F32)),
        ),
        name="t5_bias_tiles",
    )(rel_bias.reshape(-1))


def _lane_tile(x, reps):
    return x if reps == 1 else jnp.concatenate([x] * reps, axis=1)


def _qk(q, k):
    return lax.dot_general(q, k, (((1,), (1,)), ((), ())), preferred_element_type=F32)


def _pipelined_tiles(n_tiles, score, softmax, pv):
    if n_tiles == 0:
        return
    score(0, 0)
    if n_tiles == 1:
        softmax(0, 0)
        pv(0, 0)
        return
    score(1, 1)
    softmax(0, 0)
    n_pairs = (n_tiles - 2) // 2

    def pair(t, carry):
        n = 2 * t
        score(n + 2, 0)
        softmax(n + 1, 1)
        pv(n, 0)
        score(n + 3, 1)
        softmax(n + 2, 0)
        pv(n + 1, 1)
        return carry

    lax.fori_loop(0, n_pairs, pair, 0)
    last = n_tiles - 1
    if n_tiles % 2 == 0:
        softmax(last, 1)
        pv(last - 1, 0)
        pv(last, 1)
    else:
        score(last, 0)
        softmax(last - 1, 1)
        pv(last - 2, 0)
        softmax(last, 0)
        pv(last - 1, 1)
        pv(last, 0)


def _causal_tile_lists(nq):
    pairs = [(i, j) for i in range(nq) for j in range(i)]
    qi = jnp.asarray([p[0] for p in pairs] or [0], jnp.int32)
    kj = jnp.asarray([p[1] for p in pairs] or [0], jnp.int32)
    return len(pairs), qi, kj


def _fox_kernel(qi_ref, kj_ref, q_ref, k_ref, v_ref, cx_ref, o_ref, qx_ref, kx_ref, kt_ref, vx_ref,
                s0_ref, s1_ref, p0_ref, p1_ref, a0_ref, a1_ref, m_ref, acc_ref,
                *, tile, scale, n_heads, n_lower):
    s_bufs, p_bufs, a_bufs = (s0_ref, s1_ref), (p0_ref, p1_ref), (a0_ref, a1_ref)
    h = pl.program_id(1)
    hd = HEAD_DIM
    seq = q_ref.shape[0]
    nq = seq // tile

    lane = lax.broadcasted_iota(jnp.int32, (seq, cx_ref.shape[1]), 1)
    sel = (lane == h) | (lane == h + n_heads) | (lane == h + 2 * n_heads)
    qx_ref[:, :hd] = (q_ref[...].astype(F32) * (scale * LOG2E)).astype(BF16)
    qx_ref[:, hd:] = jnp.where(sel, 1.0, 0.0).astype(BF16)
    kx_ref[:, :hd] = k_ref[...]
    kx_ref[:, hd:] = cx_ref[...]
    vx_ref[:, :hd] = v_ref[...]
    vx_ref[:, hd:] = jnp.ones((seq, vx_ref.shape[1] - hd), BF16)
    m_ref[...] = jnp.full(m_ref.shape, NEG_INF, F32)
    acc_ref[...] = jnp.zeros(acc_ref.shape, F32)

    def rows_of(block):
        return pl.ds(pl.multiple_of(block * tile, tile), tile)

    def transpose_keys(kb, carry):
        kt_ref[:, rows_of(kb)] = kx_ref[rows_of(kb), :].astype(F32).T.astype(BF16)
        return carry

    lax.fori_loop(0, nq, transpose_keys, 0)

    def score_stage(qb, kb, slot):
        s_bufs[slot][...] = jnp.dot(qx_ref[rows_of(qb), :], kt_ref[:, rows_of(kb)], preferred_element_type=F32)

    def softmax_stage(qb, slot, masked):
        s = s_bufs[slot][...]
        if masked:
            row = lax.broadcasted_iota(jnp.int32, s.shape, 0)
            col = lax.broadcasted_iota(jnp.int32, s.shape, 1)
            s = jnp.where(col <= row, s, NEG_INF)
        m_prev = m_ref[rows_of(qb), :]
        m_new = jnp.maximum(m_prev, jnp.max(s, axis=1, keepdims=True))
        a_bufs[slot][...] = jnp.exp2(m_prev - m_new)
        p_bufs[slot][...] = jnp.exp2(s - _lane_tile(m_new, tile // V7X_LANES)).astype(BF16)
        m_ref[rows_of(qb), :] = m_new

    def pv_stage(qb, kb, slot):
        pv = jnp.dot(p_bufs[slot][...], vx_ref[rows_of(kb), :], preferred_element_type=F32)
        acc = acc_ref[rows_of(qb), :]
        acc_ref[rows_of(qb), :] = _lane_tile(a_bufs[slot][...], acc.shape[1] // V7X_LANES) * acc + pv

    _pipelined_tiles(
        n_lower,
        lambda n, slot: score_stage(qi_ref[n], kj_ref[n], slot),
        lambda n, slot: softmax_stage(qi_ref[n], slot, False),
        lambda n, slot: pv_stage(qi_ref[n], kj_ref[n], slot))
    _pipelined_tiles(
        nq,
        lambda n, slot: score_stage(n, n, slot),
        lambda n, slot: softmax_stage(n, slot, True),
        lambda n, slot: pv_stage(n, n, slot))

    def finish(qb, carry):
        acc = acc_ref[rows_of(qb), :]
        o_ref[rows_of(qb), :] = (acc[:, :hd] / acc[:, hd:2 * hd]).astype(o_ref.dtype)
        return carry

    lax.fori_loop(0, nq, finish, 0)


def _fox_attention(proj, cx, batch, seq, n_heads, tile):
    nq = seq // tile
    hd = HEAD_DIM
    nx = cx.shape[1]
    n_lower, qi, kj = _causal_tile_lists(nq)
    blocks = 4 * _nbytes((seq, hd), BF16) + _nbytes((seq, nx), BF16)
    scratch = (4 * _nbytes((seq, hd + nx), BF16) + 2 * _nbytes((tile, tile), F32)
               + 2 * _nbytes((tile, tile), BF16) + 2 * _nbytes((tile, V7X_LANES), F32)
               + _nbytes((seq, V7X_LANES), F32) + _nbytes((seq, 2 * hd), F32))
    smem = pl.BlockSpec(memory_space=pltpu.SMEM)
    return pl.pallas_call(
        functools.partial(_fox_kernel, tile=tile, scale=hd ** -0.5, n_heads=n_heads, n_lower=n_lower),
        grid=(batch, n_heads),
        in_specs=[
            smem, smem,
            pl.BlockSpec((seq, hd), lambda b, h: (b, h)),
            pl.BlockSpec((seq, hd), lambda b, h: (b, n_heads + h)),
            pl.BlockSpec((seq, hd), lambda b, h: (b, 2 * n_heads + h)),
            pl.BlockSpec((seq, nx), lambda b, h: (b, 0)),
        ],
        out_specs=pl.BlockSpec((seq, hd), lambda b, h: (b, h)),
        out_shape=jax.ShapeDtypeStruct((batch * seq, n_heads * hd), BF16),
        scratch_shapes=[
            pltpu.VMEM((seq, hd + nx), BF16),
            pltpu.VMEM((seq, hd + nx), BF16),
            pltpu.VMEM((hd + nx, seq), BF16),
            pltpu.VMEM((seq, 2 * hd), BF16),
            pltpu.VMEM((tile, tile), F32), pltpu.VMEM((tile, tile), F32),
            pltpu.VMEM((tile, tile), BF16), pltpu.VMEM((tile, tile), BF16),
            pltpu.VMEM((tile, V7X_LANES), F32), pltpu.VMEM((tile, V7X_LANES), F32),
            pltpu.VMEM((seq, V7X_LANES), F32),
            pltpu.VMEM((seq, 2 * hd), F32),
        ],
        compiler_params=pltpu.CompilerParams(
            dimension_semantics=("parallel", "arbitrary"),
            vmem_limit_bytes=_vmem_limit(blocks, scratch, 12 * _nbytes((tile, tile), F32)),
        ),
        name="fox_attention",
    )(qi, kj, proj, proj, proj, cx)


def _diff_kernel(blk_ref, qs_ref, kj_ref, q_ref, k_ref, v_ref, bias_ref, lq1_ref, lk1_ref, lq2_ref, lk2_ref,
                 g_ref, o_ref, qq_ref, s0_ref, s1_ref, p0_ref, p1_ref, a0_ref, a1_ref, m_ref, l_ref, acc_ref,
                 *, tile, scale, lambda_init, n_tiles, n_slots):
    s_bufs, p_bufs, a_bufs = (s0_ref, s1_ref), (p0_ref, p1_ref), (a0_ref, a1_ref)
    grp = pl.program_id(2)
    hd = HEAD_DIM
    two = 2 * tile

    def rows_of(block, size=tile):
        return pl.ds(pl.multiple_of(block * size, size), size)

    def stage_queries(slot, carry):
        qf = q_ref[rows_of(blk_ref[grp, slot]), :].astype(F32) * (scale * LOG2E)
        lane = lax.broadcasted_iota(jnp.int32, qf.shape, 1)
        base = pl.multiple_of(slot * two, two)
        qq_ref[pl.ds(base, tile), :] = jnp.where(lane < hd, qf, 0.0).astype(BF16)
        qq_ref[pl.ds(base + tile, tile), :] = jnp.where(lane >= hd, qf, 0.0).astype(BF16)
        return carry

    lax.fori_loop(0, n_slots, stage_queries, 0)
    m_ref[...] = jnp.full(m_ref.shape, NEG_INF, F32)
    l_ref[...] = jnp.zeros(l_ref.shape, F32)
    acc_ref[...] = jnp.zeros(acc_ref.shape, F32)

    def score_stage(n, slot):
        s_bufs[slot][...] = _qk(qq_ref[rows_of(qs_ref[grp, n], two), :], k_ref[rows_of(kj_ref[grp, n]), :])

    def softmax_stage(n, slot):
        qs = qs_ref[grp, n]
        bias = bias_ref[0, jnp.minimum(blk_ref[grp, qs] - kj_ref[grp, n], 2)]
        for c in range(2):
            rows = slice(c * tile, (c + 1) * tile)
            state = pl.ds(pl.multiple_of(qs * two + c * tile, tile), tile)
            s = s_bufs[slot][rows, :] + bias
            m_prev = m_ref[state, :]
            m_new = jnp.maximum(m_prev, jnp.max(s, axis=1, keepdims=True))
            alpha = jnp.exp2(m_prev - m_new)
            p = jnp.exp2(s - _lane_tile(m_new, tile // V7X_LANES))
            l_ref[state, :] = alpha * l_ref[state, :] + jnp.sum(p, axis=1, keepdims=True)
            a_bufs[slot][rows, :] = alpha
            p_bufs[slot][rows, :] = p.astype(BF16)
            m_ref[state, :] = m_new

    def pv_stage(n, slot):
        state = rows_of(qs_ref[grp, n], two)
        pv = jnp.dot(p_bufs[slot][...], v_ref[rows_of(kj_ref[grp, n]), :], preferred_element_type=F32)
        acc = acc_ref[state, :]
        acc_ref[state, :] = _lane_tile(a_bufs[slot][...], acc.shape[1] // V7X_LANES) * acc + pv

    _pipelined_tiles(n_tiles, score_stage, softmax_stage, pv_stage)

    lam = (jnp.exp(jnp.sum(lq1_ref[...] * lk1_ref[...], axis=1, keepdims=True))
           - jnp.exp(jnp.sum(lq2_ref[...] * lk2_ref[...], axis=1, keepdims=True))
           + lambda_init)

    def finish(slot, carry):
        base = pl.multiple_of(slot * two, two)
        c1, c2 = pl.ds(base, tile), pl.ds(base + tile, tile)
        reps = acc_ref.shape[1] // V7X_LANES
        out = (acc_ref[c1, :] / _lane_tile(l_ref[c1, :], reps)
               - lam * (acc_ref[c2, :] / _lane_tile(l_ref[c2, :], reps)))
        ms = jnp.mean(out * out, axis=-1, keepdims=True)
        y = out * lax.rsqrt(ms + SUBLN_EPS) * g_ref[...]
        o_ref[rows_of(blk_ref[grp, slot]), :] = (y * (1.0 - lambda_init)).astype(o_ref.dtype)
        return carry

    lax.fori_loop(0, n_slots, finish, 0)


def _balanced_causal_groups(nq):
    n_groups = 2 if nq % 4 == 0 else 1
    groups = [[] for _ in range(n_groups)]
    for p in range((nq + 1) // 2):
        for blk in sorted({p, nq - 1 - p}):
            groups[p % n_groups].append(blk)
    slots = [[s for s, blk in enumerate(g) for _ in range(blk + 1)] for g in groups]
    keys = [[j for blk in g for j in range(blk + 1)] for g in groups]
    assert len({len(s) for s in slots}) == 1 and len({len(g) for g in groups}) == 1
    return groups, slots, keys


def _diff_attention(proj, bias, lq1, lk1, lq2, lk2, g_subln, batch, seq, n_heads, col0, tile, lambda_init):
    nq = seq // tile
    hd2 = 2 * HEAD_DIM
    c0 = col0 // hd2
    groups, slots, keys = _balanced_causal_groups(nq)
    n_groups, n_slots, n_tiles = len(groups), len(groups[0]), len(slots[0])
    vec = lambda: pl.BlockSpec((1, HEAD_DIM), lambda h, b, g: (0, 0))
    smem = pl.BlockSpec(memory_space=pltpu.SMEM)
    blocks = 4 * _nbytes((seq, hd2), BF16) + _nbytes((3, tile, tile), F32)
    state_rows = 2 * n_slots * tile
    scratch = (_nbytes((state_rows, hd2), BF16) + 2 * _nbytes((2 * tile, tile), F32)
               + 2 * _nbytes((2 * tile, tile), BF16) + 2 * _nbytes((2 * tile, V7X_LANES), F32)
               + 2 * _nbytes((state_rows, V7X_LANES), F32) + _nbytes((state_rows, hd2), F32))
    return pl.pallas_call(
        functools.partial(_diff_kernel, tile=tile, scale=HEAD_DIM ** -0.5, lambda_init=lambda_init,
                          n_tiles=n_tiles, n_slots=n_slots),
        grid=(n_heads, batch, n_groups),
        in_specs=[
            smem, smem, smem,
            pl.BlockSpec((seq, hd2), lambda h, b, g: (b, c0 + h)),
            pl.BlockSpec((seq, hd2), lambda h, b, g: (b, c0 + n_heads + h)),
            pl.BlockSpec((seq, hd2), lambda h, b, g: (b, c0 + 2 * n_heads + h)),
            pl.BlockSpec((1, 3, tile, tile), lambda h, b, g: (h, 0, 0, 0)),
            vec(), vec(), vec(), vec(),
            pl.BlockSpec((1, hd2), lambda h, b, g: (0, 0)),
        ],
        out_specs=pl.BlockSpec((seq, hd2), lambda h, b, g: (b, h)),
        out_shape=jax.ShapeDtypeStruct((batch * seq, n_heads * hd2), BF16),
        scratch_shapes=[
            pltpu.VMEM((state_rows, hd2), BF16),
            pltpu.VMEM((2 * tile, tile), F32), pltpu.VMEM((2 * tile, tile), F32),
            pltpu.VMEM((2 * tile, tile), BF16), pltpu.VMEM((2 * tile, tile), BF16),
            pltpu.VMEM((2 * tile, V7X_LANES), F32), pltpu.VMEM((2 * tile, V7X_LANES), F32),
            pltpu.VMEM((state_rows, V7X_LANES), F32),
            pltpu.VMEM((state_rows, V7X_LANES), F32),
            pltpu.VMEM((state_rows, hd2), F32),
        ],
        compiler_params=pltpu.CompilerParams(
            dimension_semantics=("parallel", "parallel", "arbitrary"),
            vmem_limit_bytes=_vmem_limit(blocks, scratch, 12 * _nbytes((tile, tile), F32)),
        ),
        name="diff_attention",
    )(jnp.asarray(groups, jnp.int32), jnp.asarray(slots, jnp.int32), jnp.asarray(keys, jnp.int32),
      proj, proj, proj, bias, lq1, lk1, lq2, lk2, g_subln)


def _cross_kernel(q_ref, k_ref, v_ref, o_ref, *, n_heads, scale):
    dh = q_ref.shape[1] // n_heads
    for h in range(n_heads):
        cols = slice(h * dh, (h + 1) * dh)
        s = _qk(q_ref[:, cols], k_ref[:, cols]) * scale
        m = jnp.max(s, axis=1, keepdims=True)
        p = jnp.exp(s - m)
        p = p / jnp.sum(p, axis=1, keepdims=True)
        o_ref[:, cols] = jnp.dot(p.astype(BF16), v_ref[:, cols], preferred_element_type=F32).astype(o_ref.dtype)


def _cross_attention(q, k_mem, v_mem, batch, seq, n_mem, tile):
    d = q.shape[1]
    nq = seq // tile
    blocks = 2 * _nbytes((tile, d), BF16) + 2 * _nbytes((n_mem, d), BF16)
    return pl.pallas_call(
        functools.partial(_cross_kernel, n_heads=H_MEM, scale=(d // H_MEM) ** -0.5),
        grid=(batch, nq),
        in_specs=[
            pl.BlockSpec((tile, d), lambda b, i: (b * nq + i, 0)),
            pl.BlockSpec((n_mem, d), lambda b, i: (b, 0)),
            pl.BlockSpec((n_mem, d), lambda b, i: (b, 0)),
        ],
        out_specs=pl.BlockSpec((tile, d), lambda b, i: (b * nq + i, 0)),
        out_shape=jax.ShapeDtypeStruct((batch * seq, d), BF16),
        compiler_params=pltpu.CompilerParams(
            dimension_semantics=("parallel", "arbitrary"),
            vmem_limit_bytes=_vmem_limit(blocks, 0, 8 * _nbytes((tile, n_mem), F32) + _nbytes((tile, d), F32)),
        ),
        name="cross_attention",
    )(q, k_mem, v_mem)


FINAL_NORM_ROWS = 32


def _mlp_kernel(a_ref, ss_ref, hs_ref, wu_ref, wd_ref, gf_ref, o_ref, *, eps):
    f = pl.program_id(1)
    rows = hs_ref.shape[0]

    @pl.when(f == 0)
    def _():
        o_ref[...] = jnp.zeros(o_ref.shape, F32)

    r0 = pl.multiple_of(f * rows, rows)
    o_ref[pl.ds(r0, rows), :] += hs_ref[...]

    r = _row_scale(ss_ref, a_ref.shape[1], eps)
    u = jnp.dot(a_ref[...], wu_ref[...], preferred_element_type=F32)
    act = (jnp.square(jnp.maximum(u, 0.0)) * _lane_tile(r * r, u.shape[1] // V7X_LANES)).astype(BF16)
    o_ref[...] += jnp.dot(act, wd_ref[...], preferred_element_type=F32)

    @pl.when(f == pl.num_programs(1) - 1)
    def _():
        def norm_rows(c, carry):
            c0 = pl.multiple_of(c * FINAL_NORM_ROWS, FINAL_NORM_ROWS)
            y = o_ref[pl.ds(c0, FINAL_NORM_ROWS), :]
            ms = jnp.mean(y * y, axis=-1, keepdims=True)
            o_ref[pl.ds(c0, FINAL_NORM_ROWS), :] = y * lax.rsqrt(ms + eps) * gf_ref[...]
            return carry

        lax.fori_loop(0, o_ref.shape[0] // FINAL_NORM_ROWS, norm_rows, 0)


def _mlp(a, ss, h, w_up, w_down, g_final, *, eps=NORM_EPS, tm=512, tf=512):
    m, d = h.shape
    dff = w_up.shape[1]
    tm = _pick_tile(m, tm)
    tf = _pick_tile(dff, tf)
    nf = dff // tf
    rows = tm // nf
    assert rows * nf == tm and rows % 8 == 0 and tm % FINAL_NORM_ROWS == 0
    nss = ss.shape[1]
    blocks = (2 * _nbytes((d, tf), BF16) + _nbytes((tm, d), BF16) + _nbytes((tm, nss), F32)
              + _nbytes((rows, d), F32) + _nbytes((tm, d), F32))
    return pl.pallas_call(
        functools.partial(_mlp_kernel, eps=eps),
        grid=(m // tm, nf),
        in_specs=[
            pl.BlockSpec((tm, d), lambda i, f: (i, 0)),
            pl.BlockSpec((tm, nss), lambda i, f: (i, 0)),
            pl.BlockSpec((rows, d), lambda i, f: (i * nf + f, 0)),
            pl.BlockSpec((d, tf), lambda i, f: (0, f)),
            pl.BlockSpec((tf, d), lambda i, f: (f, 0)),
            pl.BlockSpec((1, d), lambda i, f: (0, 0)),
        ],
        out_specs=pl.BlockSpec((tm, d), lambda i, f: (i, 0)),
        out_shape=jax.ShapeDtypeStruct((m, d), F32),
        compiler_params=pltpu.CompilerParams(
            dimension_semantics=("parallel", "arbitrary"),
            vmem_limit_bytes=_vmem_limit(blocks, 0, 4 * _nbytes((tm, tf), F32)),
        ),
        name="mlp_final_norm",
    )(a, ss, h, w_up, w_down, g_final.reshape(1, d))


def _attn_tile(seq):
    return _pick_tile(seq, 512)


def _layer(h, mem, l, g_mix, w_in, b_forget, lambda_q1, lambda_k1, lambda_q2, lambda_k2, g_subln, rel_bias,
           w_out, g_cross, g_mem, wq_mem, wk_mem, wv_mem, wo_mem, g_mlp, w_up, w_down, g_final):
    batch, seq, d = h.shape
    n_mem = mem.shape[1]
    h_fox = b_forget.shape[-1]
    h_diff = rel_bias.shape[1]
    w_fox = h_fox * HEAD_DIM
    tile = _attn_tile(seq)
    x2 = h.reshape(batch * seq, d)

    def gained(g, w):
        return (g.astype(F32)[:, None] * w).astype(BF16)

    wi = w_in[l]
    f0 = 3 * w_fox
    w_qkv = jnp.concatenate([wi[:, :f0], wi[:, f0 + h_fox:]], axis=1).astype(BF16)
    n_gate = -(-h_fox // V7X_LANES) * V7X_LANES
    w_gate = jnp.pad(wi[:, f0:f0 + h_fox].astype(BF16), ((0, 0), (0, n_gate - h_fox)))
    b_gate = jnp.pad(b_forget[l].astype(F32), (0, n_gate - h_fox)).reshape(1, n_gate)

    proj, f_logit = _in_proj(x2, g_mix[l], w_qkv, w_gate)

    cx = _gate_cumsum(f_logit, b_gate, batch, seq, h_fox)
    fox = _fox_attention(proj, cx, batch, seq, h_fox, tile)

    lambda_init = 0.8 - 0.6 * math.exp(-0.3 * l)
    bias = _bias_tiles(rel_bias.astype(F32), tile)
    row = lambda v: v[l].astype(F32).reshape(1, -1)
    diff = _diff_attention(proj, bias, row(lambda_q1), row(lambda_k1), row(lambda_q2), row(lambda_k2),
                           row(g_subln), batch, seq, h_diff, 3 * w_fox, tile, lambda_init)

    w_o = w_out[l].astype(BF16)
    h1, h1b, ss1 = _matmul_resid((fox, 0), (diff, 0), w_fox, (w_o, 0), (w_o, 1), w_fox, x2, name="out_proj")

    q = _scaled_matmul(h1b, ss1, gained(g_cross[l], wq_mem[l]), name="cross_q_proj")
    mem2 = mem.reshape(batch * n_mem, d)
    k_mem = _norm_matmul(mem2, g_mem[l], wk_mem[l].astype(BF16), name="cross_k_proj")
    v_mem = _norm_matmul(mem2, g_mem[l], wv_mem[l].astype(BF16), name="cross_v_proj")
    o = _cross_attention(q, k_mem, v_mem, batch, seq, n_mem, tile)
    w_om = wo_mem[l].astype(BF16)
    half = d // 2
    h2, h2b, ss2 = _matmul_resid((o, 0), (o, 1), half, (w_om, 0), (w_om, 1), half, h1, name="cross_o_proj")

    out = _mlp(h2b, ss2, h2, gained(g_mlp[l], w_up[l]), w_down[l].astype(BF16), g_final)
    return out.reshape(batch, seq, d)


def kernel(x, mem, g_mix, w_in, b_forget, lambda_q1, lambda_k1, lambda_q2, lambda_k2, g_subln, rel_bias, w_out,
           g_cross, g_mem, wq_mem, wk_mem, wv_mem, wo_mem, g_mlp, w_up, w_down, g_final):
    depth = g_mix.shape[0]
    assert depth == 1, "the fused MLP epilogue applies the final norm; only depth 1 is supported"
    return _layer(x, mem, 0, g_mix, w_in, b_forget, lambda_q1, lambda_k1, lambda_q2, lambda_k2, g_subln,
                  rel_bias, w_out, g_cross, g_mem, wq_mem, wk_mem, wv_mem, wo_mem, g_mlp, w_up, w_down,
                  g_final)
```

```python
import functools
import math

import jax
import jax.numpy as jnp
from jax import lax
from jax.experimental import pallas as pl
from jax.experimental.pallas import tpu as pltpu

HEAD_DIM = 128
CHUNK = 64
N_BUCKETS = 32
MAX_DISTANCE = 128
H_MEM = 4
NORM_EPS = 1e-6
SUBLN_EPS = 1e-5
NEG_INF = -1e30
LOG2E = math.log2(math.e)

V7X_LANES = 128
V7X_VMEM_BYTES = 64 * 1024 * 1024
V7X_VMEM_REQUEST_CAP = V7X_VMEM_BYTES - 8 * 1024 * 1024

F32 = jnp.float32
BF16 = jnp.bfloat16


def _vmem_limit(block_bytes, scratch_bytes=0, temp_bytes=0):
    need = 2 * block_bytes + scratch_bytes + temp_bytes + (4 << 20)
    return int(min(max(need, 16 << 20), V7X_VMEM_REQUEST_CAP))


def _nbytes(shape, dtype):
    return math.prod(shape) * jnp.dtype(dtype).itemsize


def _pick_tile(n, target):
    if n <= target:
        return n
    t = target
    while t >= V7X_LANES:
        if n % t == 0:
            return t
        t -= V7X_LANES
    return n


def _t5_thresholds():
    half = N_BUCKETS // 2
    max_exact = half // 2
    steps = half - max_exact
    ratio = MAX_DISTANCE // max_exact
    thr = []
    for k in range(1, steps):
        n = max_exact
        while n ** steps < (ratio ** k) * (max_exact ** steps):
            n += 1
        thr.append(n)
    return half, max_exact, tuple(thr)


def _norm_matmul_kernel(x_ref, g_ref, w_ref, *rest, eps, with_gate):
    if with_gate:
        wf_ref, o_ref, f_ref, a_ref = rest
    else:
        o_ref, a_ref = rest

    @pl.when(pl.program_id(1) == 0)
    def _():
        x = x_ref[...]
        ms = jnp.mean(x * x, axis=-1, keepdims=True)
        a = (x * lax.rsqrt(ms + eps) * g_ref[...]).astype(BF16)
        a_ref[...] = a
        if with_gate:
            f_ref[...] = jnp.dot(a, wf_ref[...], preferred_element_type=F32)

    o_ref[...] = jnp.dot(a_ref[...], w_ref[...], preferred_element_type=F32).astype(o_ref.dtype)


def _norm_matmul(x, g, w, wf=None, *, eps=NORM_EPS, tm=512, tn=1024, name):
    m, k = x.shape
    n = w.shape[1]
    tm = _pick_tile(m, tm)
    tn = _pick_tile(n, tn)
    with_gate = wf is not None
    in_specs = [
        pl.BlockSpec((tm, k), lambda i, j: (i, 0), pipeline_mode=pl.Buffered(1)),
        pl.BlockSpec((1, k), lambda i, j: (0, 0)),
        pl.BlockSpec((k, tn), lambda i, j: (0, j)),
    ]
    args = [x, g.reshape(1, k), w]
    out_shape = [jax.ShapeDtypeStruct((m, n), BF16)]
    out_specs = [pl.BlockSpec((tm, tn), lambda i, j: (i, j))]
    blocks = _nbytes((k, tn), BF16) + _nbytes((tm, tn), BF16)
    if with_gate:
        nf = wf.shape[1]
        in_specs.append(pl.BlockSpec((k, nf), lambda i, j: (0, 0)))
        args.append(wf)
        out_shape.append(jax.ShapeDtypeStruct((m, nf), F32))
        out_specs.append(pl.BlockSpec((tm, nf), lambda i, j: (i, 0)))
        blocks += _nbytes((k, nf), BF16) + _nbytes((tm, nf), F32)
    res = pl.pallas_call(
        functools.partial(_norm_matmul_kernel, eps=eps, with_gate=with_gate),
        grid=(m // tm, n // tn),
        in_specs=in_specs,
        out_specs=out_specs,
        out_shape=out_shape,
        scratch_shapes=[pltpu.VMEM((tm, k), BF16)],
        compiler_params=pltpu.CompilerParams(
            dimension_semantics=("parallel", "arbitrary"),
            vmem_limit_bytes=_vmem_limit(
                blocks, _nbytes((tm, k), F32) + _nbytes((tm, k), BF16), 2 * _nbytes((tm, k), F32)),
        ),
        name=name,
    )(*args)
    return res if with_gate else res[0]


def _row_scale(ss_ref, k, eps):
    ss = ss_ref[:, :V7X_LANES]
    for c in range(1, ss_ref.shape[1] // V7X_LANES):
        ss = ss + ss_ref[:, c * V7X_LANES:(c + 1) * V7X_LANES]
    return lax.rsqrt(ss * (1.0 / k) + eps)


def _in_proj_kernel(xs_ref, g_ref, w_ref, wf_ref, o_ref, f_ref, a0_ref, a1_ref, fs0_ref, fs1_ref, *, eps):
    t = pl.program_id(0)
    j = pl.program_id(1)
    rows = xs_ref.shape[0]

    def step(a_cur, fs_cur, a_nxt, fs_nxt):
        def norm_slice():
            x = xs_ref[...]
            ms = jnp.mean(x * x, axis=-1, keepdims=True)
            a = (x * lax.rsqrt(ms + eps) * g_ref[...]).astype(BF16)
            r0 = pl.multiple_of(j * rows, rows)
            a_nxt[pl.ds(r0, rows), :] = a
            fs_nxt[pl.ds(r0, rows), :] = jnp.dot(a, wf_ref[...], preferred_element_type=F32)

        @pl.when((t > 0) & (j == 0))
        def _():
            f_ref[...] = fs_cur[...]

        @pl.when(t == 0)
        def _():
            o_ref[...] = jnp.zeros(o_ref.shape, o_ref.dtype)
            f_ref[...] = jnp.zeros(f_ref.shape, f_ref.dtype)
            norm_slice()

        @pl.when(t > 0)
        def _():
            o_ref[...] = jnp.dot(a_cur[...], w_ref[...], preferred_element_type=F32).astype(o_ref.dtype)
            norm_slice()

    @pl.when((t & 1) == 0)
    def _():
        step(a1_ref, fs1_ref, a0_ref, fs0_ref)

    @pl.when((t & 1) == 1)
    def _():
        step(a0_ref, fs0_ref, a1_ref, fs1_ref)


def _stage_w_in_kernel(w_ref, o_ref, g_ref, *, f0, n_gate_cols):
    x = w_ref[0]
    o_ref[:, :f0] = x[:, :f0].astype(BF16)
    o_ref[:, f0:] = x[:, f0 + n_gate_cols:].astype(BF16)
    gate = x[:, f0:f0 + g_ref.shape[1]]
    lane = lax.broadcasted_iota(jnp.int32, gate.shape, 1)
    g_ref[...] = jnp.where(lane < n_gate_cols, gate, 0.0).astype(BF16)


def _stage_w_in(w_in, layer, f0, n_gate_cols, *, rows=128):
    _, k, n = w_in.shape
    n_out = n - n_gate_cols
    assert f0 % V7X_LANES == 0 and f0 + V7X_LANES <= n and k % rows == 0
    blocks = _nbytes((rows, n), F32) + _nbytes((rows, n_out), BF16) + _nbytes((rows, V7X_LANES), BF16)
    return pl.pallas_call(
        functools.partial(_stage_w_in_kernel, f0=f0, n_gate_cols=n_gate_cols),
        grid=(k // rows,),
        in_specs=[pl.BlockSpec((1, rows, n), lambda i: (layer, i, 0))],
        out_specs=[
            pl.BlockSpec((rows, n_out), lambda i: (i, 0)),
            pl.BlockSpec((rows, V7X_LANES), lambda i: (i, 0)),
        ],
        out_shape=[
            jax.ShapeDtypeStruct((k, n_out), BF16),
            jax.ShapeDtypeStruct((k, V7X_LANES), BF16),
        ],
        compiler_params=pltpu.CompilerParams(
            dimension_semantics=("parallel",),
            vmem_limit_bytes=_vmem_limit(blocks, 0, 2 * _nbytes((rows, n), F32)),
        ),
        name="stage_w_in",
    )(w_in)


def _in_proj(x, g, w, wf, *, eps=NORM_EPS, tm=1024, tn=768):
    m, k = x.shape
    n = w.shape[1]
    tm = _pick_tile(m, tm)
    tn = _pick_tile(n, tn)
    nj = n // tn
    n_rt = m // tm
    rows = tm // nj
    assert rows * nj == tm and rows % 16 == 0, (tm, nj)
    nf = wf.shape[1]
    blocks = (_nbytes((k, tn), BF16) + _nbytes((tm, tn), BF16) + _nbytes((k, nf), BF16)
              + _nbytes((tm, nf), F32) + _nbytes((rows, k), F32))
    scratch = 2 * _nbytes((tm, k), BF16) + 2 * _nbytes((tm, nf), F32)
    return pl.pallas_call(
        functools.partial(_in_proj_kernel, eps=eps),
        grid=(n_rt + 1, nj),
        in_specs=[
            pl.BlockSpec((rows, k), lambda t, j: (jnp.minimum(t, n_rt - 1) * nj + j, 0)),
            pl.BlockSpec((1, k), lambda t, j: (0, 0)),
            pl.BlockSpec((k, tn), lambda t, j: (0, j)),
            pl.BlockSpec((k, nf), lambda t, j: (0, 0)),
        ],
        out_specs=[
            pl.BlockSpec((tm, tn), lambda t, j: (jnp.where(t == 0, n_rt, t - 1), j)),
            pl.BlockSpec((tm, nf), lambda t, j: (jnp.where(t == 0, n_rt, t - 1), 0)),
        ],
        out_shape=[
            jax.ShapeDtypeStruct((m + tm, n), BF16),
            jax.ShapeDtypeStruct((m + tm, nf), F32),
        ],
        scratch_shapes=[pltpu.VMEM((tm, k), BF16), pltpu.VMEM((tm, k), BF16),
                        pltpu.VMEM((tm, nf), F32), pltpu.VMEM((tm, nf), F32)],
        compiler_params=pltpu.CompilerParams(
            dimension_semantics=("arbitrary", "arbitrary"),
            vmem_limit_bytes=_vmem_limit(blocks, scratch, 2 * _nbytes((tm, tn), F32)),
        ),
        name="in_proj",
    )(x, g.reshape(1, k), w, wf)


def _scaled_matmul_kernel(a_ref, ss_ref, w_ref, o_ref, *, eps):
    r = _row_scale(ss_ref, a_ref.shape[1], eps)
    acc = jnp.dot(a_ref[...], w_ref[...], preferred_element_type=F32)
    o_ref[...] = (_lane_tile(r, o_ref.shape[1] // V7X_LANES) * acc).astype(o_ref.dtype)


def _scaled_matmul(a, ss, w, *, eps=NORM_EPS, tm=1024, tn=1024, name):
    m, k = a.shape
    n = w.shape[1]
    tm = _pick_tile(m, tm)
    tn = _pick_tile(n, tn)
    nss = ss.shape[1]
    blocks = (_nbytes((tm, k), BF16) + _nbytes((tm, nss), F32) + _nbytes((k, tn), BF16)
              + _nbytes((tm, tn), BF16))
    return pl.pallas_call(
        functools.partial(_scaled_matmul_kernel, eps=eps),
        grid=(m // tm, n // tn),
        in_specs=[
            pl.BlockSpec((tm, k), lambda i, j: (i, 0)),
            pl.BlockSpec((tm, nss), lambda i, j: (i, 0)),
            pl.BlockSpec((k, tn), lambda i, j: (0, j)),
        ],
        out_specs=pl.BlockSpec((tm, tn), lambda i, j: (i, j)),
        out_shape=jax.ShapeDtypeStruct((m, n), BF16),
        compiler_params=pltpu.CompilerParams(
            dimension_semantics=("parallel", "arbitrary"),
            vmem_limit_bytes=_vmem_limit(blocks, 0, 2 * _nbytes((tm, tn), F32)),
        ),
        name=name,
    )(a, ss, w)


def _matmul_resid_kernel(a1_ref, a2_ref, w1_ref, w2_ref, r_ref, o_ref, ob_ref, ss_ref):
    acc = jnp.dot(a1_ref[...], w1_ref[...], preferred_element_type=F32)
    acc = acc + jnp.dot(a2_ref[...], w2_ref[...], preferred_element_type=F32)
    h = r_ref[...] + acc
    o_ref[...] = h
    ob_ref[...] = h.astype(BF16)
    ss_ref[...] = jnp.broadcast_to(jnp.sum(h * h, axis=1, keepdims=True), ss_ref.shape)


def _matmul_resid(a1, a2, a_blk, w1, w2, w_blk, resid, *, tm=1024, tn=512, name):
    (a1, a1c), (a2, a2c) = a1, a2
    (w1, w1r), (w2, w2r) = w1, w2
    m, n = resid.shape
    tm = _pick_tile(m, tm)
    tn = _pick_tile(n, tn)
    blocks = (2 * _nbytes((tm, a_blk), BF16) + 2 * _nbytes((w_blk, tn), BF16)
              + 2 * _nbytes((tm, tn), F32) + _nbytes((tm, tn), BF16) + _nbytes((tm, V7X_LANES), F32))
    return pl.pallas_call(
        _matmul_resid_kernel,
        grid=(m // tm, n // tn),
        in_specs=[
            pl.BlockSpec((tm, a_blk), lambda i, j: (i, a1c)),
            pl.BlockSpec((tm, a_blk), lambda i, j: (i, a2c)),
            pl.BlockSpec((w_blk, tn), lambda i, j: (w1r, j)),
            pl.BlockSpec((w_blk, tn), lambda i, j: (w2r, j)),
            pl.BlockSpec((tm, tn), lambda i, j: (i, j)),
        ],
        out_specs=[
            pl.BlockSpec((tm, tn), lambda i, j: (i, j)),
            pl.BlockSpec((tm, tn), lambda i, j: (i, j)),
            pl.BlockSpec((tm, V7X_LANES), lambda i, j: (i, j)),
        ],
        out_shape=[
            jax.ShapeDtypeStruct((m, n), F32),
            jax.ShapeDtypeStruct((m, n), BF16),
            jax.ShapeDtypeStruct((m, (n // tn) * V7X_LANES), F32),
        ],
        compiler_params=pltpu.CompilerParams(
            dimension_semantics=("parallel", "arbitrary"),
            vmem_limit_bytes=_vmem_limit(blocks, 0, 3 * _nbytes((tm, tn), F32)),
        ),
        name=name,
    )(a1, a2, w1, w2, resid)


def _gate_cumsum_kernel(f_ref, b_ref, o_ref, *, n_heads):
    z = f_ref[...] + b_ref[...]
    x = jnp.minimum(z, 0.0) - jnp.log1p(jnp.exp(-jnp.abs(z)))
    s_len = x.shape[0]
    row = lax.broadcasted_iota(jnp.int32, x.shape, 0)
    d = 1
    while d < s_len:
        x = x + jnp.where(row >= d, pltpu.roll(x, d, axis=0), 0.0)
        d *= 2
    c2 = x * (-LOG2E)
    hi = c2.astype(BF16)
    r1 = c2 - hi.astype(F32)
    mid = r1.astype(BF16)
    lo = (r1 - mid.astype(F32)).astype(BF16)
    lane = lax.broadcasted_iota(jnp.int32, x.shape, 1)
    mid_s = pltpu.roll(mid.astype(F32), n_heads, axis=1)
    lo_s = pltpu.roll(lo.astype(F32), 2 * n_heads, axis=1)
    out = jnp.where(lane < n_heads, hi.astype(F32),
                    jnp.where(lane < 2 * n_heads, mid_s, jnp.where(lane < 3 * n_heads, lo_s, 0.0)))
    o_ref[...] = out.astype(BF16)


def _gate_cumsum(f_logit, b_pad, batch, seq, n_heads):
    nf = f_logit.shape[1]
    assert 3 * n_heads <= nf
    return pl.pallas_call(
        functools.partial(_gate_cumsum_kernel, n_heads=n_heads),
        grid=(batch,),
        in_specs=[
            pl.BlockSpec((seq, nf), lambda b: (b, 0)),
            pl.BlockSpec((1, nf), lambda b: (0, 0)),
        ],
        out_specs=pl.BlockSpec((seq, nf), lambda b: (b, 0)),
        out_shape=jax.ShapeDtypeStruct((batch * seq, nf), BF16),
        compiler_params=pltpu.CompilerParams(
            dimension_semantics=("parallel",),
            vmem_limit_bytes=_vmem_limit(2 * _nbytes((seq, nf), F32), 0, 8 * _nbytes((seq, nf), F32)),
        ),
        name="gate_cumsum",
    )(f_logit, b_pad)


def _bias_tile_kernel(rb_ref, o_ref, *, tile, n_heads):
    h = pl.program_id(0)
    d = pl.program_id(1)
    half, max_exact, thresholds = _t5_thresholds()
    assert tile >= thresholds[-1], "tile 2 must lie wholly in the last (saturated) past bucket"

    @pl.when(d < 2)
    def _():
        t = lax.broadcasted_iota(jnp.int32, (tile, tile), 0)
        s = lax.broadcasted_iota(jnp.int32, (tile, tile), 1)
        rel = s - t - d * tile
        n = jnp.abs(rel)
        large = jnp.full((tile, tile), max_exact, jnp.int32)
        for thr in thresholds:
            large = large + (n >= thr).astype(jnp.int32)
        idx = jnp.where(n < max_exact, n, large) + jnp.where(rel > 0, half, 0)
        val = jnp.zeros((tile, tile), F32)
        for b in range(N_BUCKETS):
            val = jnp.where(idx == b, rb_ref[b * n_heads + h], val)
        shift = int(math.log2(CHUNK))
        allowed = (s >> shift) <= ((t >> shift) + d * tile)
        o_ref[0, 0] = jnp.where(allowed, val * LOG2E, NEG_INF)

    @pl.when(d == 2)
    def _():
        o_ref[0, 0] = jnp.full((tile, tile), rb_ref[(half - 1) * n_heads + h] * LOG2E, F32)


def _bias_tiles(rel_bias, tile):
    n_heads = rel_bias.shape[1]
    return pl.pallas_call(
        functools.partial(_bias_tile_kernel, tile=tile, n_heads=n_heads),
        grid=(n_heads, 3),
        in_specs=[pl.BlockSpec(memory_space=pltpu.SMEM)],
        out_specs=pl.BlockSpec((1, 1, tile, tile), lambda h, d: (h, d, 0, 0)),
        out_shape=jax.ShapeDtypeStruct((n_heads, 3, tile, tile), F32),
        compiler_params=pltpu.CompilerParams(
            dimension_semantics=("parallel", "parallel"),
            vmem_limit_bytes=_vmem_limit(_nbytes((tile, tile), F32), 0, 8 * _nbytes((tile, tile), F32)),
        ),
        name="t5_bias_tiles",
    )(rel_bias.reshape(-1))


def _lane_tile(x, reps):
    return x if reps == 1 else jnp.concatenate([x] * reps, axis=1)


def _qk(q, k):
    return lax.dot_general(q, k, (((1,), (1,)), ((), ())), preferred_element_type=F32)


def _pipelined_tiles(n_tiles, score, softmax, pv):
    if n_tiles == 0:
        return
    score(0, 0)
    if n_tiles == 1:
        softmax(0, 0)
        pv(0, 0)
        return
    score(1, 1)
    softmax(0, 0)
    n_pairs = (n_tiles - 2) // 2

    def pair(t, carry):
        n = 2 * t
        score(n + 2, 0)
        softmax(n + 1, 1)
        pv(n, 0)
        score(n + 3, 1)
        softmax(n + 2, 0)
        pv(n + 1, 1)
        return carry

    lax.fori_loop(0, n_pairs, pair, 0)
    last = n_tiles - 1
    if n_tiles % 2 == 0:
        softmax(last, 1)
        pv(last - 1, 0)
        pv(last, 1)
    else:
        score(last, 0)
        softmax(last - 1, 1)
        pv(last - 2, 0)
        softmax(last, 0)
        pv(last - 1, 1)
        pv(last, 0)


def _causal_tile_lists(nq):
    pairs = [(i, j) for i in range(nq) for j in range(i)]
    qi = jnp.asarray([p[0] for p in pairs] or [0], jnp.int32)
    kj = jnp.asarray([p[1] for p in pairs] or [0], jnp.int32)
    return len(pairs), qi, kj


def _fox_kernel(qi_ref, kj_ref, q_ref, k_ref, v_ref, cx_ref, o_ref, qx_ref, kx_ref, kt_ref, vx_ref,
                s0_ref, s1_ref, p0_ref, p1_ref, a0_ref, a1_ref, m_ref, acc_ref,
                *, tile, scale, n_heads, n_lower):
    s_bufs, p_bufs, a_bufs = (s0_ref, s1_ref), (p0_ref, p1_ref), (a0_ref, a1_ref)
    h = pl.program_id(1)
    hd = HEAD_DIM
    seq = q_ref.shape[0]
    nq = seq // tile

    lane = lax.broadcasted_iota(jnp.int32, (seq, cx_ref.shape[1]), 1)
    sel = (lane == h) | (lane == h + n_heads) | (lane == h + 2 * n_heads)
    qx_ref[:, :hd] = (q_ref[...].astype(F32) * (scale * LOG2E)).astype(BF16)
    qx_ref[:, hd:] = jnp.where(sel, 1.0, 0.0).astype(BF16)
    kx_ref[:, :hd] = k_ref[...]
    kx_ref[:, hd:] = cx_ref[...]
    vx_ref[:, :hd] = v_ref[...]
    vx_ref[:, hd:] = jnp.ones((seq, vx_ref.shape[1] - hd), BF16)
    m_ref[...] = jnp.full(m_ref.shape, NEG_INF, F32)
    acc_ref[...] = jnp.zeros(acc_ref.shape, F32)

    def rows_of(block):
        return pl.ds(pl.multiple_of(block * tile, tile), tile)

    def transpose_keys(kb, carry):
        kt_ref[:, rows_of(kb)] = kx_ref[rows_of(kb), :].astype(F32).T.astype(BF16)
        return carry

    lax.fori_loop(0, nq, transpose_keys, 0)

    def score_stage(qb, kb, slot):
        s_bufs[slot][...] = jnp.dot(qx_ref[rows_of(qb), :], kt_ref[:, rows_of(kb)], preferred_element_type=F32)

    def softmax_stage(qb, slot, masked):
        s = s_bufs[slot][...]
        if masked:
            row = lax.broadcasted_iota(jnp.int32, s.shape, 0)
            col = lax.broadcasted_iota(jnp.int32, s.shape, 1)
            s = jnp.where(col <= row, s, NEG_INF)
        m_prev = m_ref[rows_of(qb), :]
        m_new = jnp.maximum(m_prev, jnp.max(s, axis=1, keepdims=True))
        a_bufs[slot][...] = jnp.exp2(m_prev - m_new)
        p_bufs[slot][...] = jnp.exp2(s - _lane_tile(m_new, tile // V7X_LANES)).astype(BF16)
        m_ref[rows_of(qb), :] = m_new

    def pv_stage(qb, kb, slot):
        pv = jnp.dot(p_bufs[slot][...], vx_ref[rows_of(kb), :], preferred_element_type=F32)
        acc = acc_ref[rows_of(qb), :]
        acc_ref[rows_of(qb), :] = _lane_tile(a_bufs[slot][...], acc.shape[1] // V7X_LANES) * acc + pv

    _pipelined_tiles(
        n_lower,
        lambda n, slot: score_stage(qi_ref[n], kj_ref[n], slot),
        lambda n, slot: softmax_stage(qi_ref[n], slot, False),
        lambda n, slot: pv_stage(qi_ref[n], kj_ref[n], slot))
    _pipelined_tiles(
        nq,
        lambda n, slot: score_stage(n, n, slot),
        lambda n, slot: softmax_stage(n, slot, True),
        lambda n, slot: pv_stage(n, n, slot))

    def finish(qb, carry):
        acc = acc_ref[rows_of(qb), :]
        o_ref[rows_of(qb), :] = (acc[:, :hd] / acc[:, hd:2 * hd]).astype(o_ref.dtype)
        return carry

    lax.fori_loop(0, nq, finish, 0)


def _fox_attention(proj, cx, batch, seq, n_heads, tile):
    nq = seq // tile
    hd = HEAD_DIM
    nx = cx.shape[1]
    n_lower, qi, kj = _causal_tile_lists(nq)
    blocks = 4 * _nbytes((seq, hd), BF16) + _nbytes((seq, nx), BF16)
    scratch = (4 * _nbytes((seq, hd + nx), BF16) + 2 * _nbytes((tile, tile), F32)
               + 2 * _nbytes((tile, tile), BF16) + 2 * _nbytes((tile, V7X_LANES), F32)
               + _nbytes((seq, V7X_LANES), F32) + _nbytes((seq, 2 * hd), F32))
    smem = pl.BlockSpec(memory_space=pltpu.SMEM)
    return pl.pallas_call(
        functools.partial(_fox_kernel, tile=tile, scale=hd ** -0.5, n_heads=n_heads, n_lower=n_lower),
        grid=(batch, n_heads),
        in_specs=[
            smem, smem,
            pl.BlockSpec((seq, hd), lambda b, h: (b, h)),
            pl.BlockSpec((seq, hd), lambda b, h: (b, n_heads + h)),
            pl.BlockSpec((seq, hd), lambda b, h: (b, 2 * n_heads + h)),
            pl.BlockSpec((seq, nx), lambda b, h: (b, 0)),
        ],
        out_specs=pl.BlockSpec((seq, hd), lambda b, h: (b, h)),
        out_shape=jax.ShapeDtypeStruct((batch * seq, n_heads * hd), BF16),
        scratch_shapes=[
            pltpu.VMEM((seq, hd + nx), BF16),
            pltpu.VMEM((seq, hd + nx), BF16),
            pltpu.VMEM((hd + nx, seq), BF16),
            pltpu.VMEM((seq, 2 * hd), BF16),
            pltpu.VMEM((tile, tile), F32), pltpu.VMEM((tile, tile), F32),
            pltpu.VMEM((tile, tile), BF16), pltpu.VMEM((tile, tile), BF16),
            pltpu.VMEM((tile, V7X_LANES), F32), pltpu.VMEM((tile, V7X_LANES), F32),
            pltpu.VMEM((seq, V7X_LANES), F32),
            pltpu.VMEM((seq, 2 * hd), F32),
        ],
        compiler_params=pltpu.CompilerParams(
            dimension_semantics=("parallel", "arbitrary"),
            vmem_limit_bytes=_vmem_limit(blocks, scratch, 12 * _nbytes((tile, tile), F32)),
        ),
        name="fox_attention",
    )(qi, kj, proj, proj, proj, cx)


def _diff_kernel(blk_ref, qs_ref, kj_ref, q_ref, k_ref, v_ref, bias_ref, lq1_ref, lk1_ref, lq2_ref, lk2_ref,
                 g_ref, o_ref, qq_ref, s0_ref, s1_ref, p0_ref, p1_ref, a0_ref, a1_ref, m_ref, l_ref, acc_ref,
                 *, tile, scale, lambda_init, n_tiles, n_slots):
    s_bufs, p_bufs, a_bufs = (s0_ref, s1_ref), (p0_ref, p1_ref), (a0_ref, a1_ref)
    grp = pl.program_id(2)
    hd = HEAD_DIM
    two = 2 * tile

    def rows_of(block, size=tile):
        return pl.ds(pl.multiple_of(block * size, size), size)

    def stage_queries(slot, carry):
        qf = q_ref[rows_of(blk_ref[grp, slot]), :].astype(F32) * (scale * LOG2E)
        lane = lax.broadcasted_iota(jnp.int32, qf.shape, 1)
        base = pl.multiple_of(slot * two, two)
        qq_ref[pl.ds(base, tile), :] = jnp.where(lane < hd, qf, 0.0).astype(BF16)
        qq_ref[pl.ds(base + tile, tile), :] = jnp.where(lane >= hd, qf, 0.0).astype(BF16)
        return carry

    lax.fori_loop(0, n_slots, stage_queries, 0)
    m_ref[...] = jnp.full(m_ref.shape, NEG_INF, F32)
    l_ref[...] = jnp.zeros(l_ref.shape, F32)
    acc_ref[...] = jnp.zeros(acc_ref.shape, F32)

    def score_stage(n, slot):
        qs, kb = qs_ref[grp, n], kj_ref[grp, n]
        qk = _qk(qq_ref[rows_of(qs, two), :], k_ref[rows_of(kb), :])
        bias = bias_ref[0, jnp.minimum(blk_ref[grp, qs] - kb, 2)]
        s_bufs[slot][:tile, :] = qk[:tile] + bias
        s_bufs[slot][tile:, :] = qk[tile:] + bias

    def softmax_stage(n, slot):
        qs = qs_ref[grp, n]
        for c in range(2):
            rows = slice(c * tile, (c + 1) * tile)
            state = pl.ds(pl.multiple_of(qs * two + c * tile, tile), tile)
            s = s_bufs[slot][rows, :]
            m_prev = m_ref[state, :]
            m_new = jnp.maximum(m_prev, jnp.max(s, axis=1, keepdims=True))
            alpha = jnp.exp2(m_prev - m_new)
            p = jnp.exp2(s - _lane_tile(m_new, tile // V7X_LANES))
            l_ref[state, :] = alpha * l_ref[state, :] + jnp.sum(p, axis=1, keepdims=True)
            a_bufs[slot][rows, :] = alpha
            p_bufs[slot][rows, :] = p.astype(BF16)
            m_ref[state, :] = m_new

    def pv_stage(n, slot):
        state = rows_of(qs_ref[grp, n], two)
        pv = jnp.dot(p_bufs[slot][...], v_ref[rows_of(kj_ref[grp, n]), :], preferred_element_type=F32)
        acc = acc_ref[state, :]
        acc_ref[state, :] = _lane_tile(a_bufs[slot][...], acc.shape[1] // V7X_LANES) * acc + pv

    _pipelined_tiles(n_tiles, score_stage, softmax_stage, pv_stage)

    lam = (jnp.exp(jnp.sum(lq1_ref[...] * lk1_ref[...], axis=1, keepdims=True))
           - jnp.exp(jnp.sum(lq2_ref[...] * lk2_ref[...], axis=1, keepdims=True))
           + lambda_init)

    def finish(slot, carry):
        base = pl.multiple_of(slot * two, two)
        c1, c2 = pl.ds(base, tile), pl.ds(base + tile, tile)
        reps = acc_ref.shape[1] // V7X_LANES
        out = (acc_ref[c1, :] / _lane_tile(l_ref[c1, :], reps)
               - lam * (acc_ref[c2, :] / _lane_tile(l_ref[c2, :], reps)))
        ms = jnp.mean(out * out, axis=-1, keepdims=True)
        y = out * lax.rsqrt(ms + SUBLN_EPS) * g_ref[...]
        o_ref[rows_of(blk_ref[grp, slot]), :] = (y * (1.0 - lambda_init)).astype(o_ref.dtype)
        return carry

    lax.fori_loop(0, n_slots, finish, 0)


def _balanced_causal_groups(nq):
    n_groups = 2 if nq % 4 == 0 else 1
    groups = [[] for _ in range(n_groups)]
    for p in range((nq + 1) // 2):
        for blk in sorted({p, nq - 1 - p}):
            groups[p % n_groups].append(blk)
    slots = [[s for s, blk in enumerate(g) for _ in range(blk + 1)] for g in groups]
    keys = [[j for blk in g for j in range(blk + 1)] for g in groups]
    assert len({len(s) for s in slots}) == 1 and len({len(g) for g in groups}) == 1
    return groups, slots, keys


def _diff_attention(proj, bias, lq1, lk1, lq2, lk2, g_subln, batch, seq, n_heads, col0, tile, lambda_init):
    nq = seq // tile
    hd2 = 2 * HEAD_DIM
    c0 = col0 // hd2
    groups, slots, keys = _balanced_causal_groups(nq)
    n_groups, n_slots, n_tiles = len(groups), len(groups[0]), len(slots[0])
    vec = lambda: pl.BlockSpec((1, HEAD_DIM), lambda h, b, g: (0, 0))
    smem = pl.BlockSpec(memory_space=pltpu.SMEM)
    blocks = 4 * _nbytes((seq, hd2), BF16) + _nbytes((3, tile, tile), F32)
    state_rows = 2 * n_slots * tile
    scratch = (_nbytes((state_rows, hd2), BF16) + 2 * _nbytes((2 * tile, tile), F32)
               + 2 * _nbytes((2 * tile, tile), BF16) + 2 * _nbytes((2 * tile, V7X_LANES), F32)
               + 2 * _nbytes((state_rows, V7X_LANES), F32) + _nbytes((state_rows, hd2), F32))
    return pl.pallas_call(
        functools.partial(_diff_kernel, tile=tile, scale=HEAD_DIM ** -0.5, lambda_init=lambda_init,
                          n_tiles=n_tiles, n_slots=n_slots),
        grid=(n_heads, batch, n_groups),
        in_specs=[
            smem, smem, smem,
            pl.BlockSpec((seq, hd2), lambda h, b, g: (b, c0 + h)),
            pl.BlockSpec((seq, hd2), lambda h, b, g: (b, c0 + n_heads + h)),
            pl.BlockSpec((seq, hd2), lambda h, b, g: (b, c0 + 2 * n_heads + h)),
            pl.BlockSpec((1, 3, tile, tile), lambda h, b, g: (h, 0, 0, 0)),
            vec(), vec(), vec(), vec(),
            pl.BlockSpec((1, hd2), lambda h, b, g: (0, 0)),
        ],
        out_specs=pl.BlockSpec((seq, hd2), lambda h, b, g: (b, h)),
        out_shape=jax.ShapeDtypeStruct((batch * seq, n_heads * hd2), BF16),
        scratch_shapes=[
            pltpu.VMEM((state_rows, hd2), BF16),
            pltpu.VMEM((2 * tile, tile), F32), pltpu.VMEM((2 * tile, tile), F32),
            pltpu.VMEM((2 * tile, tile), BF16), pltpu.VMEM((2 * tile, tile), BF16),
            pltpu.VMEM((2 * tile, V7X_LANES), F32), pltpu.VMEM((2 * tile, V7X_LANES), F32),
            pltpu.VMEM((state_rows, V7X_LANES), F32),
            pltpu.VMEM((state_rows, V7X_LANES), F32),
            pltpu.VMEM((state_rows, hd2), F32),
        ],
        compiler_params=pltpu.CompilerParams(
            dimension_semantics=("parallel", "parallel", "arbitrary"),
            vmem_limit_bytes=_vmem_limit(blocks, scratch, 12 * _nbytes((tile, tile), F32)),
        ),
        name="diff_attention",
    )(jnp.asarray(groups, jnp.int32), jnp.asarray(slots, jnp.int32), jnp.asarray(keys, jnp.int32),
      proj, proj, proj, bias, lq1, lk1, lq2, lk2, g_subln)


def _cross_kernel(q_ref, k_ref, v_ref, o_ref, *, n_heads, scale):
    dh = q_ref.shape[1] // n_heads
    for h in range(n_heads):
        cols = slice(h * dh, (h + 1) * dh)
        s = _qk(q_ref[:, cols], k_ref[:, cols]) * scale
        m = jnp.max(s, axis=1, keepdims=True)
        p = jnp.exp(s - m)
        p = p / jnp.sum(p, axis=1, keepdims=True)
        o_ref[:, cols] = jnp.dot(p.astype(BF16), v_ref[:, cols], preferred_element_type=F32).astype(o_ref.dtype)


def _cross_attention(q, k_mem, v_mem, batch, seq, n_mem, tile):
    d = q.shape[1]
    nq = seq // tile
    blocks = 2 * _nbytes((tile, d), BF16) + 2 * _nbytes((n_mem, d), BF16)
    return pl.pallas_call(
        functools.partial(_cross_kernel, n_heads=H_MEM, scale=(d // H_MEM) ** -0.5),
        grid=(batch, nq),
        in_specs=[
            pl.BlockSpec((tile, d), lambda b, i: (b * nq + i, 0)),
            pl.BlockSpec((n_mem, d), lambda b, i: (b, 0)),
            pl.BlockSpec((n_mem, d), lambda b, i: (b, 0)),
        ],
        out_specs=pl.BlockSpec((tile, d), lambda b, i: (b * nq + i, 0)),
        out_shape=jax.ShapeDtypeStruct((batch * seq, d), BF16),
        compiler_params=pltpu.CompilerParams(
            dimension_semantics=("parallel", "arbitrary"),
            vmem_limit_bytes=_vmem_limit(blocks, 0, 8 * _nbytes((tile, n_mem), F32) + _nbytes((tile, d), F32)),
        ),
        name="cross_attention",
    )(q, k_mem, v_mem)


FINAL_NORM_ROWS = 32


def _mlp_kernel(a_ref, ss_ref, hs_ref, wu_ref, wd_ref, gf_ref, o_ref, *, eps):
    f = pl.program_id(1)
    rows = hs_ref.shape[0]

    @pl.when(f == 0)
    def _():
        o_ref[...] = jnp.zeros(o_ref.shape, F32)

    r0 = pl.multiple_of(f * rows, rows)
    o_ref[pl.ds(r0, rows), :] += hs_ref[...]

    r = _row_scale(ss_ref, a_ref.shape[1], eps)
    u = jnp.dot(a_ref[...], wu_ref[...], preferred_element_type=F32)
    act = (jnp.square(jnp.maximum(u, 0.0)) * _lane_tile(r * r, u.shape[1] // V7X_LANES)).astype(BF16)
    o_ref[...] += jnp.dot(act, wd_ref[...], preferred_element_type=F32)

    @pl.when(f == pl.num_programs(1) - 1)
    def _():
        def norm_rows(c, carry):
            c0 = pl.multiple_of(c * FINAL_NORM_ROWS, FINAL_NORM_ROWS)
            y = o_ref[pl.ds(c0, FINAL_NORM_ROWS), :]
            ms = jnp.mean(y * y, axis=-1, keepdims=True)
            o_ref[pl.ds(c0, FINAL_NORM_ROWS), :] = y * lax.rsqrt(ms + eps) * gf_ref[...]
            return carry

        lax.fori_loop(0, o_ref.shape[0] // FINAL_NORM_ROWS, norm_rows, 0)


def _mlp(a, ss, h, w_up, w_down, g_final, *, eps=NORM_EPS, tm=512, tf=512):
    m, d = h.shape
    dff = w_up.shape[1]
    tm = _pick_tile(m, tm)
    tf = _pick_tile(dff, tf)
    nf = dff // tf
    rows = tm // nf
    assert rows * nf == tm and rows % 8 == 0 and tm % FINAL_NORM_ROWS == 0
    nss = ss.shape[1]
    blocks = (2 * _nbytes((d, tf), BF16) + _nbytes((tm, d), BF16) + _nbytes((tm, nss), F32)
              + _nbytes((rows, d), F32) + _nbytes((tm, d), F32))
    return pl.pallas_call(
        functools.partial(_mlp_kernel, eps=eps),
        grid=(m // tm, nf),
        in_specs=[
            pl.BlockSpec((tm, d), lambda i, f: (i, 0)),
            pl.BlockSpec((tm, nss), lambda i, f: (i, 0)),
            pl.BlockSpec((rows, d), lambda i, f: (i * nf + f, 0)),
            pl.BlockSpec((d, tf), lambda i, f: (0, f)),
            pl.BlockSpec((tf, d), lambda i, f: (f, 0)),
            pl.BlockSpec((1, d), lambda i, f: (0, 0)),
        ],
        out_specs=pl.BlockSpec((tm, d), lambda i, f: (i, 0)),
        out_shape=jax.ShapeDtypeStruct((m, d), F32),
        compiler_params=pltpu.CompilerParams(
            dimension_semantics=("parallel", "arbitrary"),
            vmem_limit_bytes=_vmem_limit(blocks, 0, 4 * _nbytes((tm, tf), F32)),
        ),
        name="mlp_final_norm",
    )(a, ss, h, w_up, w_down, g_final.reshape(1, d))


def _attn_tile(seq):
    return _pick_tile(seq, 512)


def _layer(h, mem, l, g_mix, w_in, b_forget, lambda_q1, lambda_k1, lambda_q2, lambda_k2, g_subln, rel_bias,
           w_out, g_cross, g_mem, wq_mem, wk_mem, wv_mem, wo_mem, g_mlp, w_up, w_down, g_final):
    batch, seq, d = h.shape
    n_mem = mem.shape[1]
    h_fox = b_forget.shape[-1]
    h_diff = rel_bias.shape[1]
    w_fox = h_fox * HEAD_DIM
    tile = _attn_tile(seq)
    x2 = h.reshape(batch * seq, d)

    def gained(g, w):
        return (g.astype(F32)[:, None] * w).astype(BF16)

    f0 = 3 * w_fox
    assert h_fox <= V7X_LANES
    w_qkv, w_gate = _stage_w_in(w_in, l, f0, h_fox)
    n_gate = w_gate.shape[1]
    b_gate = jnp.pad(b_forget[l].astype(F32), (0, n_gate - h_fox)).reshape(1, n_gate)

    proj, f_logit = _in_proj(x2, g_mix[l], w_qkv, w_gate)

    cx = _gate_cumsum(f_logit, b_gate, batch, seq, h_fox)
    fox = _fox_attention(proj, cx, batch, seq, h_fox, tile)

    lambda_init = 0.8 - 0.6 * math.exp(-0.3 * l)
    bias = _bias_tiles(rel_bias.astype(F32), tile)
    row = lambda v: v[l].astype(F32).reshape(1, -1)
    diff = _diff_attention(proj, bias, row(lambda_q1), row(lambda_k1), row(lambda_q2), row(lambda_k2),
                           row(g_subln), batch, seq, h_diff, 3 * w_fox, tile, lambda_init)

    w_o = w_out[l].astype(BF16)
    h1, h1b, ss1 = _matmul_resid((fox, 0), (diff, 0), w_fox, (w_o, 0), (w_o, 1), w_fox, x2, name="out_proj")

    q = _scaled_matmul(h1b, ss1, gained(g_cross[l], wq_mem[l]), name="cross_q_proj")
    mem2 = mem.reshape(batch * n_mem, d)
    k_mem = _norm_matmul(mem2, g_mem[l], wk_mem[l].astype(BF16), name="cross_k_proj")
    v_mem = _norm_matmul(mem2, g_mem[l], wv_mem[l].astype(BF16), name="cross_v_proj")
    o = _cross_attention(q, k_mem, v_mem, batch, seq, n_mem, tile)
    w_om = wo_mem[l].astype(BF16)
    half = d // 2
    h2, h2b, ss2 = _matmul_resid((o, 0), (o, 1), half, (w_om, 0), (w_om, 1), half, h1, name="cross_o_proj")

    out = _mlp(h2b, ss2, h2, gained(g_mlp[l], w_up[l]), w_down[l].astype(BF16), g_final)
    return out.reshape(batch, seq, d)


def kernel(x, mem, g_mix, w_in, b_forget, lambda_q1, lambda_k1, lambda_q2, lambda_k2, g_subln, rel_bias, w_out,
           g_cross, g_mem, wq_mem, wk_mem, wv_mem, wo_mem, g_mlp, w_up, w_down, g_final):
    depth = g_mix.shape[0]
    assert depth == 1, "the fused MLP epilogue applies the final norm; only depth 1 is supported"
    return _layer(x, mem, 0, g_mix, w_in, b_forget, lambda_q1, lambda_k1, lambda_q2, lambda_k2, g_subln,
                  rel_bias, w_out, g_cross, g_mem, wq_mem, wk_mem, wv_mem, wo_mem, g_mlp, w_up, w_down,
                  g_final)
```

```python
import functools
import math

import jax
import jax.numpy as jnp
from jax import lax
from jax.experimental import pallas as pl
from jax.experimental.pallas import tpu as pltpu

HEAD_DIM = 128
CHUNK = 64
N_BUCKETS = 32
MAX_DISTANCE = 128
H_MEM = 4
NORM_EPS = 1e-6
SUBLN_EPS = 1e-5
NEG_INF = -1e30
LOG2E = math.log2(math.e)

V7X_LANES = 128
BF16_ROWS_PER_TILE = 16
V7X_VMEM_BYTES = 64 * 1024 * 1024
V7X_VMEM_REQUEST_CAP = V7X_VMEM_BYTES - 8 * 1024 * 1024

F32 = jnp.float32
BF16 = jnp.bfloat16


def _vmem_limit(block_bytes, scratch_bytes=0, temp_bytes=0):
    need = 2 * block_bytes + scratch_bytes + temp_bytes + (4 << 20)
    return int(min(max(need, 16 << 20), V7X_VMEM_REQUEST_CAP))


def _nbytes(shape, dtype):
    return math.prod(shape) * jnp.dtype(dtype).itemsize


def _pick_tile(n, target):
    if n <= target:
        return n
    t = target
    while t >= V7X_LANES:
        if n % t == 0:
            return t
        t -= V7X_LANES
    return n


def _t5_thresholds():
    half = N_BUCKETS // 2
    max_exact = half // 2
    steps = half - max_exact
    ratio = MAX_DISTANCE // max_exact
    thr = []
    for k in range(1, steps):
        n = max_exact
        while n ** steps < (ratio ** k) * (max_exact ** steps):
            n += 1
        thr.append(n)
    return half, max_exact, tuple(thr)


def _norm_matmul_kernel(x_ref, g_ref, w_ref, *rest, eps, with_gate):
    if with_gate:
        wf_ref, o_ref, f_ref, a_ref = rest
    else:
        o_ref, a_ref = rest

    @pl.when(pl.program_id(1) == 0)
    def _():
        x = x_ref[...]
        ms = jnp.mean(x * x, axis=-1, keepdims=True)
        a = (x * lax.rsqrt(ms + eps) * g_ref[...]).astype(BF16)
        a_ref[...] = a
        if with_gate:
            f_ref[...] = jnp.dot(a, wf_ref[...], preferred_element_type=F32)

    o_ref[...] = jnp.dot(a_ref[...], w_ref[...], preferred_element_type=F32).astype(o_ref.dtype)


def _norm_matmul(x, g, w, wf=None, *, eps=NORM_EPS, tm=512, tn=1024, name):
    m, k = x.shape
    n = w.shape[1]
    tm = _pick_tile(m, tm)
    tn = _pick_tile(n, tn)
    with_gate = wf is not None
    in_specs = [
        pl.BlockSpec((tm, k), lambda i, j: (i, 0), pipeline_mode=pl.Buffered(1)),
        pl.BlockSpec((1, k), lambda i, j: (0, 0)),
        pl.BlockSpec((k, tn), lambda i, j: (0, j)),
    ]
    args = [x, g.reshape(1, k), w]
    out_shape = [jax.ShapeDtypeStruct((m, n), BF16)]
    out_specs = [pl.BlockSpec((tm, tn), lambda i, j: (i, j))]
    blocks = _nbytes((k, tn), BF16) + _nbytes((tm, tn), BF16)
    if with_gate:
        nf = wf.shape[1]
        in_specs.append(pl.BlockSpec((k, nf), lambda i, j: (0, 0)))
        args.append(wf)
        out_shape.append(jax.ShapeDtypeStruct((m, nf), F32))
        out_specs.append(pl.BlockSpec((tm, nf), lambda i, j: (i, 0)))
        blocks += _nbytes((k, nf), BF16) + _nbytes((tm, nf), F32)
    res = pl.pallas_call(
        functools.partial(_norm_matmul_kernel, eps=eps, with_gate=with_gate),
        grid=(m // tm, n // tn),
        in_specs=in_specs,
        out_specs=out_specs,
        out_shape=out_shape,
        scratch_shapes=[pltpu.VMEM((tm, k), BF16)],
        compiler_params=pltpu.CompilerParams(
            dimension_semantics=("parallel", "arbitrary"),
            vmem_limit_bytes=_vmem_limit(
                blocks, _nbytes((tm, k), F32) + _nbytes((tm, k), BF16), 2 * _nbytes((tm, k), F32)),
        ),
        name=name,
    )(*args)
    return res if with_gate else res[0]


def _row_scale(ss_ref, k, eps):
    ss = ss_ref[:, :V7X_LANES]
    for c in range(1, ss_ref.shape[1] // V7X_LANES):
        ss = ss + ss_ref[:, c * V7X_LANES:(c + 1) * V7X_LANES]
    return lax.rsqrt(ss * (1.0 / k) + eps)


def _in_proj_kernel(xs_ref, g_ref, w_ref, wf_ref, o_ref, f_ref, a0_ref, a1_ref, fs0_ref, fs1_ref, *, eps):
    t = pl.program_id(0)
    j = pl.program_id(1)
    rows = xs_ref.shape[0]

    def step(a_cur, fs_cur, a_nxt, fs_nxt):
        def norm_slice():
            x = xs_ref[...]
            ms = jnp.mean(x * x, axis=-1, keepdims=True)
            a = (x * lax.rsqrt(ms + eps) * g_ref[...]).astype(BF16)
            r0 = pl.multiple_of(j * rows, rows)
            a_nxt[pl.ds(r0, rows), :] = a
            fs_nxt[pl.ds(r0, rows), :] = _qk(a, wf_ref[...])

        @pl.when((t > 0) & (j == 0))
        def _():
            f_ref[...] = fs_cur[...]

        @pl.when(t == 0)
        def _():
            o_ref[...] = jnp.zeros(o_ref.shape, o_ref.dtype)
            f_ref[...] = jnp.zeros(f_ref.shape, f_ref.dtype)
            norm_slice()

        @pl.when(t > 0)
        def _():
            o_ref[...] = _qk(a_cur[...], w_ref[...]).astype(o_ref.dtype)
            norm_slice()

    @pl.when((t & 1) == 0)
    def _():
        step(a1_ref, fs1_ref, a0_ref, fs0_ref)

    @pl.when((t & 1) == 1)
    def _():
        step(a0_ref, fs0_ref, a1_ref, fs1_ref)


def _in_proj(x, g, w_t, skip, wf_t, *, eps=NORM_EPS, tm=1024, tn=768):
    m, k = x.shape
    skip0, n_skip = skip
    n = w_t.shape[0] - n_skip
    tm = _pick_tile(m, tm)
    tn = _pick_tile(math.gcd(n, skip0), tn)
    nj = n // tn
    n_rt = m // tm
    rows = tm // nj
    assert rows * nj == tm and rows % BF16_ROWS_PER_TILE == 0, (tm, nj)
    assert skip0 % tn == 0 and n_skip % BF16_ROWS_PER_TILE == 0, skip

    def w_rows(t, j):
        start = j * tn
        return pl.multiple_of(jnp.where(start < skip0, start, start + n_skip), BF16_ROWS_PER_TILE), 0
    nf = wf_t.shape[0]
    blocks = (_nbytes((k, tn), BF16) + _nbytes((tm, tn), BF16) + _nbytes((k, nf), BF16)
              + _nbytes((tm, nf), F32) + _nbytes((rows, k), F32))
    scratch = 2 * _nbytes((tm, k), BF16) + 2 * _nbytes((tm, nf), F32)
    return pl.pallas_call(
        functools.partial(_in_proj_kernel, eps=eps),
        grid=(n_rt + 1, nj),
        in_specs=[
            pl.BlockSpec((rows, k), lambda t, j: (jnp.minimum(t, n_rt - 1) * nj + j, 0)),
            pl.BlockSpec((1, k), lambda t, j: (0, 0)),
            pl.BlockSpec((pl.Element(tn), pl.Element(k)), w_rows),
            pl.BlockSpec((nf, k), lambda t, j: (0, 0)),
        ],
        out_specs=[
            pl.BlockSpec((tm, tn), lambda t, j: (jnp.where(t == 0, n_rt, t - 1), j)),
            pl.BlockSpec((tm, nf), lambda t, j: (jnp.where(t == 0, n_rt, t - 1), 0)),
        ],
        out_shape=[
            jax.ShapeDtypeStruct((m + tm, n), BF16),
            jax.ShapeDtypeStruct((m + tm, nf), F32),
        ],
        scratch_shapes=[pltpu.VMEM((tm, k), BF16), pltpu.VMEM((tm, k), BF16),
                        pltpu.VMEM((tm, nf), F32), pltpu.VMEM((tm, nf), F32)],
        compiler_params=pltpu.CompilerParams(
            dimension_semantics=("arbitrary", "arbitrary"),
            vmem_limit_bytes=_vmem_limit(blocks, scratch, 2 * _nbytes((tm, tn), F32)),
        ),
        name="in_proj",
    )(x, g.reshape(1, k), w_t, wf_t)


def _scaled_matmul_kernel(a_ref, ss_ref, w_ref, o_ref, *, eps):
    r = _row_scale(ss_ref, a_ref.shape[1], eps)
    acc = jnp.dot(a_ref[...], w_ref[...], preferred_element_type=F32)
    o_ref[...] = (_lane_tile(r, o_ref.shape[1] // V7X_LANES) * acc).astype(o_ref.dtype)


def _scaled_matmul(a, ss, w, *, eps=NORM_EPS, tm=1024, tn=1024, name):
    m, k = a.shape
    n = w.shape[1]
    tm = _pick_tile(m, tm)
    tn = _pick_tile(n, tn)
    nss = ss.shape[1]
    blocks = (_nbytes((tm, k), BF16) + _nbytes((tm, nss), F32) + _nbytes((k, tn), BF16)
              + _nbytes((tm, tn), BF16))
    return pl.pallas_call(
        functools.partial(_scaled_matmul_kernel, eps=eps),
        grid=(m // tm, n // tn),
        in_specs=[
            pl.BlockSpec((tm, k), lambda i, j: (i, 0)),
            pl.BlockSpec((tm, nss), lambda i, j: (i, 0)),
            pl.BlockSpec((k, tn), lambda i, j: (0, j)),
        ],
        out_specs=pl.BlockSpec((tm, tn), lambda i, j: (i, j)),
        out_shape=jax.ShapeDtypeStruct((m, n), BF16),
        compiler_params=pltpu.CompilerParams(
            dimension_semantics=("parallel", "arbitrary"),
            vmem_limit_bytes=_vmem_limit(blocks, 0, 2 * _nbytes((tm, tn), F32)),
        ),
        name=name,
    )(a, ss, w)


def _matmul_resid_kernel(a1_ref, a2_ref, w1_ref, w2_ref, r_ref, o_ref, ob_ref, ss_ref):
    acc = jnp.dot(a1_ref[...], w1_ref[...], preferred_element_type=F32)
    acc = acc + jnp.dot(a2_ref[...], w2_ref[...], preferred_element_type=F32)
    h = r_ref[...] + acc
    o_ref[...] = h
    ob_ref[...] = h.astype(BF16)
    ss_ref[...] = jnp.broadcast_to(jnp.sum(h * h, axis=1, keepdims=True), ss_ref.shape)


def _matmul_resid(a1, a2, a_blk, w1, w2, w_blk, resid, *, tm=1024, tn=512, name):
    (a1, a1c), (a2, a2c) = a1, a2
    (w1, w1r), (w2, w2r) = w1, w2
    m, n = resid.shape
    tm = _pick_tile(m, tm)
    tn = _pick_tile(n, tn)
    blocks = (2 * _nbytes((tm, a_blk), BF16) + 2 * _nbytes((w_blk, tn), BF16)
              + 2 * _nbytes((tm, tn), F32) + _nbytes((tm, tn), BF16) + _nbytes((tm, V7X_LANES), F32))
    return pl.pallas_call(
        _matmul_resid_kernel,
        grid=(m // tm, n // tn),
        in_specs=[
            pl.BlockSpec((tm, a_blk), lambda i, j: (i, a1c)),
            pl.BlockSpec((tm, a_blk), lambda i, j: (i, a2c)),
            pl.BlockSpec((w_blk, tn), lambda i, j: (w1r, j)),
            pl.BlockSpec((w_blk, tn), lambda i, j: (w2r, j)),
            pl.BlockSpec((tm, tn), lambda i, j: (i, j)),
        ],
        out_specs=[
            pl.BlockSpec((tm, tn), lambda i, j: (i, j)),
            pl.BlockSpec((tm, tn), lambda i, j: (i, j)),
            pl.BlockSpec((tm, V7X_LANES), lambda i, j: (i, j)),
        ],
        out_shape=[
            jax.ShapeDtypeStruct((m, n), F32),
            jax.ShapeDtypeStruct((m, n), BF16),
            jax.ShapeDtypeStruct((m, (n // tn) * V7X_LANES), F32),
        ],
        compiler_params=pltpu.CompilerParams(
            dimension_semantics=("parallel", "arbitrary"),
            vmem_limit_bytes=_vmem_limit(blocks, 0, 3 * _nbytes((tm, tn), F32)),
        ),
        name=name,
    )(a1, a2, w1, w2, resid)


def _gate_cumsum_kernel(f_ref, b_ref, o_ref, *, n_heads):
    z = f_ref[...] + b_ref[...]
    x = jnp.minimum(z, 0.0) - jnp.log1p(jnp.exp(-jnp.abs(z)))
    s_len = x.shape[0]
    row = lax.broadcasted_iota(jnp.int32, x.shape, 0)
    d = 1
    while d < s_len:
        x = x + jnp.where(row >= d, pltpu.roll(x, d, axis=0), 0.0)
        d *= 2
    c2 = x * (-LOG2E)
    hi = c2.astype(BF16)
    r1 = c2 - hi.astype(F32)
    mid = r1.astype(BF16)
    lo = (r1 - mid.astype(F32)).astype(BF16)
    lane = lax.broadcasted_iota(jnp.int32, x.shape, 1)
    mid_s = pltpu.roll(mid.astype(F32), n_heads, axis=1)
    lo_s = pltpu.roll(lo.astype(F32), 2 * n_heads, axis=1)
    out = jnp.where(lane < n_heads, hi.astype(F32),
                    jnp.where(lane < 2 * n_heads, mid_s, jnp.where(lane < 3 * n_heads, lo_s, 0.0)))
    o_ref[...] = out.astype(BF16)


def _gate_cumsum(f_logit, b_pad, batch, seq, n_heads):
    nf = f_logit.shape[1]
    assert 3 * n_heads <= nf
    return pl.pallas_call(
        functools.partial(_gate_cumsum_kernel, n_heads=n_heads),
        grid=(batch,),
        in_specs=[
            pl.BlockSpec((seq, nf), lambda b: (b, 0)),
            pl.BlockSpec((1, nf), lambda b: (0, 0)),
        ],
        out_specs=pl.BlockSpec((seq, nf), lambda b: (b, 0)),
        out_shape=jax.ShapeDtypeStruct((batch * seq, nf), BF16),
        compiler_params=pltpu.CompilerParams(
            dimension_semantics=("parallel",),
            vmem_limit_bytes=_vmem_limit(2 * _nbytes((seq, nf), F32), 0, 8 * _nbytes((seq, nf), F32)),
        ),
        name="gate_cumsum",
    )(f_logit, b_pad)


def _bias_tile_kernel(rb_ref, o_ref, *, tile, n_heads):
    h = pl.program_id(0)
    d = pl.program_id(1)
    half, max_exact, thresholds = _t5_thresholds()
    assert tile >= thresholds[-1], "tile 2 must lie wholly in the last (saturated) past bucket"

    @pl.when(d < 2)
    def _():
        t = lax.broadcasted_iota(jnp.int32, (tile, tile), 0)
        s = lax.broadcasted_iota(jnp.int32, (tile, tile), 1)
        rel = s - t - d * tile
        n = jnp.abs(rel)
        large = jnp.full((tile, tile), max_exact, jnp.int32)
        for thr in thresholds:
            large = large + (n >= thr).astype(jnp.int32)
        idx = jnp.where(n < max_exact, n, large) + jnp.where(rel > 0, half, 0)
        val = jnp.zeros((tile, tile), F32)
        for b in range(N_BUCKETS):
            val = jnp.where(idx == b, rb_ref[b * n_heads + h], val)
        shift = int(math.log2(CHUNK))
        allowed = (s >> shift) <= ((t >> shift) + d * tile)
        o_ref[0, 0] = jnp.where(allowed, val * LOG2E, NEG_INF)

    @pl.when(d == 2)
    def _():
        o_ref[0, 0] = jnp.full((tile, tile), rb_ref[(half - 1) * n_heads + h] * LOG2E, F32)


def _bias_tiles(rel_bias, tile):
    n_heads = rel_bias.shape[1]
    return pl.pallas_call(
        functools.partial(_bias_tile_kernel, tile=tile, n_heads=n_heads),
        grid=(n_heads, 3),
        in_specs=[pl.BlockSpec(memory_space=pltpu.SMEM)],
        out_specs=pl.BlockSpec((1, 1, tile, tile), lambda h, d: (h, d, 0, 0)),
        out_shape=jax.ShapeDtypeStruct((n_heads, 3, tile, tile), F32),
        compiler_params=pltpu.CompilerParams(
            dimension_semantics=("parallel", "parallel"),
            vmem_limit_bytes=_vmem_limit(_nbytes((tile, tile), F32), 0, 8 * _nbytes((tile, tile), F32)),
        ),
        name="t5_bias_tiles",
    )(rel_bias.reshape(-1))


def _lane_tile(x, reps):
    return x if reps == 1 else jnp.concatenate([x] * reps, axis=1)


def _qk(q, k):
    return lax.dot_general(q, k, (((1,), (1,)), ((), ())), preferred_element_type=F32)


def _pipelined_tiles(n_tiles, score, softmax, pv):
    if n_tiles == 0:
        return
    score(0, 0)
    if n_tiles == 1:
        softmax(0, 0)
        pv(0, 0)
        return
    score(1, 1)
    softmax(0, 0)
    n_pairs = (n_tiles - 2) // 2

    def pair(t, carry):
        n = 2 * t
        score(n + 2, 0)
        softmax(n + 1, 1)
        pv(n, 0)
        score(n + 3, 1)
        softmax(n + 2, 0)
        pv(n + 1, 1)
        return carry

    lax.fori_loop(0, n_pairs, pair, 0)
    last = n_tiles - 1
    if n_tiles % 2 == 0:
        softmax(last, 1)
        pv(last - 1, 0)
        pv(last, 1)
    else:
        score(last, 0)
        softmax(last - 1, 1)
        pv(last - 2, 0)
        softmax(last, 0)
        pv(last - 1, 1)
        pv(last, 0)


def _causal_tile_lists(nq):
    pairs = [(i, j) for i in range(nq) for j in range(i)]
    qi = jnp.asarray([p[0] for p in pairs] or [0], jnp.int32)
    kj = jnp.asarray([p[1] for p in pairs] or [0], jnp.int32)
    return len(pairs), qi, kj


def _fox_kernel(qi_ref, kj_ref, q_ref, k_ref, v_ref, cx_ref, o_ref, qx_ref, kx_ref, kt_ref, vx_ref,
                s0_ref, s1_ref, p0_ref, p1_ref, a0_ref, a1_ref, m_ref, acc_ref,
                *, tile, scale, n_heads, n_lower):
    s_bufs, p_bufs, a_bufs = (s0_ref, s1_ref), (p0_ref, p1_ref), (a0_ref, a1_ref)
    h = pl.program_id(1)
    hd = HEAD_DIM
    seq = q_ref.shape[0]
    nq = seq // tile

    lane = lax.broadcasted_iota(jnp.int32, (seq, cx_ref.shape[1]), 1)
    sel = (lane == h) | (lane == h + n_heads) | (lane == h + 2 * n_heads)
    qx_ref[:, :hd] = (q_ref[...].astype(F32) * (scale * LOG2E)).astype(BF16)
    qx_ref[:, hd:] = jnp.where(sel, 1.0, 0.0).astype(BF16)
    kx_ref[:, :hd] = k_ref[...]
    kx_ref[:, hd:] = cx_ref[...]
    vx_ref[:, :hd] = v_ref[...]
    vx_ref[:, hd:] = jnp.ones((seq, vx_ref.shape[1] - hd), BF16)
    m_ref[...] = jnp.full(m_ref.shape, NEG_INF, F32)
    acc_ref[...] = jnp.zeros(acc_ref.shape, F32)

    def rows_of(block):
        return pl.ds(pl.multiple_of(block * tile, tile), tile)

    def transpose_keys(kb, carry):
        kt_ref[:, rows_of(kb)] = kx_ref[rows_of(kb), :].astype(F32).T.astype(BF16)
        return carry

    lax.fori_loop(0, nq, transpose_keys, 0)

    def score_stage(qb, kb, slot):
        s_bufs[slot][...] = jnp.dot(qx_ref[rows_of(qb), :], kt_ref[:, rows_of(kb)], preferred_element_type=F32)

    def softmax_stage(qb, slot, masked):
        s = s_bufs[slot][...]
        if masked:
            row = lax.broadcasted_iota(jnp.int32, s.shape, 0)
            col = lax.broadcasted_iota(jnp.int32, s.shape, 1)
            s = jnp.where(col <= row, s, NEG_INF)
        m_prev = m_ref[rows_of(qb), :]
        m_new = jnp.maximum(m_prev, jnp.max(s, axis=1, keepdims=True))
        a_bufs[slot][...] = jnp.exp2(m_prev - m_new)
        p_bufs[slot][...] = jnp.exp2(s - _lane_tile(m_new, tile // V7X_LANES)).astype(BF16)
        m_ref[rows_of(qb), :] = m_new

    def pv_stage(qb, kb, slot):
        pv = jnp.dot(p_bufs[slot][...], vx_ref[rows_of(kb), :], preferred_element_type=F32)
        acc = acc_ref[rows_of(qb), :]
        acc_ref[rows_of(qb), :] = _lane_tile(a_bufs[slot][...], acc.shape[1] // V7X_LANES) * acc + pv

    _pipelined_tiles(
        n_lower,
        lambda n, slot: score_stage(qi_ref[n], kj_ref[n], slot),
        lambda n, slot: softmax_stage(qi_ref[n], slot, False),
        lambda n, slot: pv_stage(qi_ref[n], kj_ref[n], slot))
    _pipelined_tiles(
        nq,
        lambda n, slot: score_stage(n, n, slot),
        lambda n, slot: softmax_stage(n, slot, True),
        lambda n, slot: pv_stage(n, n, slot))

    def finish(qb, carry):
        acc = acc_ref[rows_of(qb), :]
        o_ref[rows_of(qb), :] = (acc[:, :hd] / acc[:, hd:2 * hd]).astype(o_ref.dtype)
        return carry

    lax.fori_loop(0, nq, finish, 0)


def _fox_attention(proj, cx, batch, seq, n_heads, tile):
    nq = seq // tile
    hd = HEAD_DIM
    nx = cx.shape[1]
    n_lower, qi, kj = _causal_tile_lists(nq)
    blocks = 4 * _nbytes((seq, hd), BF16) + _nbytes((seq, nx), BF16)
    scratch = (4 * _nbytes((seq, hd + nx), BF16) + 2 * _nbytes((tile, tile), F32)
               + 2 * _nbytes((tile, tile), BF16) + 2 * _nbytes((tile, V7X_LANES), F32)
               + _nbytes((seq, V7X_LANES), F32) + _nbytes((seq, 2 * hd), F32))
    smem = pl.BlockSpec(memory_space=pltpu.SMEM)
    return pl.pallas_call(
        functools.partial(_fox_kernel, tile=tile, scale=hd ** -0.5, n_heads=n_heads, n_lower=n_lower),
        grid=(batch, n_heads),
        in_specs=[
            smem, smem,
            pl.BlockSpec((seq, hd), lambda b, h: (b, h)),
            pl.BlockSpec((seq, hd), lambda b, h: (b, n_heads + h)),
            pl.BlockSpec((seq, hd), lambda b, h: (b, 2 * n_heads + h)),
            pl.BlockSpec((seq, nx), lambda b, h: (b, 0)),
        ],
        out_specs=pl.BlockSpec((seq, hd), lambda b, h: (b, h)),
        out_shape=jax.ShapeDtypeStruct((batch * seq, n_heads * hd), BF16),
        scratch_shapes=[
            pltpu.VMEM((seq, hd + nx), BF16),
            pltpu.VMEM((seq, hd + nx), BF16),
            pltpu.VMEM((hd + nx, seq), BF16),
            pltpu.VMEM((seq, 2 * hd), BF16),
            pltpu.VMEM((tile, tile), F32), pltpu.VMEM((tile, tile), F32),
            pltpu.VMEM((tile, tile), BF16), pltpu.VMEM((tile, tile), BF16),
            pltpu.VMEM((tile, V7X_LANES), F32), pltpu.VMEM((tile, V7X_LANES), F32),
            pltpu.VMEM((seq, V7X_LANES), F32),
            pltpu.VMEM((seq, 2 * hd), F32),
        ],
        compiler_params=pltpu.CompilerParams(
            dimension_semantics=("parallel", "arbitrary"),
            vmem_limit_bytes=_vmem_limit(blocks, scratch, 12 * _nbytes((tile, tile), F32)),
        ),
        name="fox_attention",
    )(qi, kj, proj, proj, proj, cx)


def _diff_kernel(blk_ref, qs_ref, kj_ref, q_ref, k_ref, v_ref, bias_ref, lq1_ref, lk1_ref, lq2_ref, lk2_ref,
                 g_ref, o_ref, qq_ref, s0_ref, s1_ref, p0_ref, p1_ref, a0_ref, a1_ref, m_ref, l_ref, acc_ref,
                 *, tile, scale, lambda_init, n_tiles, n_slots):
    s_bufs, p_bufs, a_bufs = (s0_ref, s1_ref), (p0_ref, p1_ref), (a0_ref, a1_ref)
    grp = pl.program_id(2)
    hd = HEAD_DIM
    two = 2 * tile

    def rows_of(block, size=tile):
        return pl.ds(pl.multiple_of(block * size, size), size)

    def stage_queries(slot, carry):
        qf = q_ref[rows_of(blk_ref[grp, slot]), :].astype(F32) * (scale * LOG2E)
        lane = lax.broadcasted_iota(jnp.int32, qf.shape, 1)
        base = pl.multiple_of(slot * two, two)
        qq_ref[pl.ds(base, tile), :] = jnp.where(lane < hd, qf, 0.0).astype(BF16)
        qq_ref[pl.ds(base + tile, tile), :] = jnp.where(lane >= hd, qf, 0.0).astype(BF16)
        return carry

    lax.fori_loop(0, n_slots, stage_queries, 0)
    m_ref[...] = jnp.full(m_ref.shape, NEG_INF, F32)
    l_ref[...] = jnp.zeros(l_ref.shape, F32)
    acc_ref[...] = jnp.zeros(acc_ref.shape, F32)

    def score_stage(n, slot):
        qs, kb = qs_ref[grp, n], kj_ref[grp, n]
        qk = _qk(qq_ref[rows_of(qs, two), :], k_ref[rows_of(kb), :])
        bias = bias_ref[0, jnp.minimum(blk_ref[grp, qs] - kb, 2)]
        s_bufs[slot][:tile, :] = qk[:tile] + bias
        s_bufs[slot][tile:, :] = qk[tile:] + bias

    def softmax_stage(n, slot):
        qs = qs_ref[grp, n]
        for c in range(2):
            rows = slice(c * tile, (c + 1) * tile)
            state = pl.ds(pl.multiple_of(qs * two + c * tile, tile), tile)
            s = s_bufs[slot][rows, :]
            m_prev = m_ref[state, :]
            m_new = jnp.maximum(m_prev, jnp.max(s, axis=1, keepdims=True))
            alpha = jnp.exp2(m_prev - m_new)
            p = jnp.exp2(s - _lane_tile(m_new, tile // V7X_LANES))
            l_ref[state, :] = alpha * l_ref[state, :] + jnp.sum(p, axis=1, keepdims=True)
            a_bufs[slot][rows, :] = alpha
            p_bufs[slot][rows, :] = p.astype(BF16)
            m_ref[state, :] = m_new

    def pv_stage(n, slot):
        state = rows_of(qs_ref[grp, n], two)
        pv = jnp.dot(p_bufs[slot][...], v_ref[rows_of(kj_ref[grp, n]), :], preferred_element_type=F32)
        acc = acc_ref[state, :]
        acc_ref[state, :] = _lane_tile(a_bufs[slot][...], acc.shape[1] // V7X_LANES) * acc + pv

    _pipelined_tiles(n_tiles, score_stage, softmax_stage, pv_stage)

    lam = (jnp.exp(jnp.sum(lq1_ref[...] * lk1_ref[...], axis=1, keepdims=True))
           - jnp.exp(jnp.sum(lq2_ref[...] * lk2_ref[...], axis=1, keepdims=True))
           + lambda_init)

    def finish(slot, carry):
        base = pl.multiple_of(slot * two, two)
        c1, c2 = pl.ds(base, tile), pl.ds(base + tile, tile)
        reps = acc_ref.shape[1] // V7X_LANES
        out = (acc_ref[c1, :] / _lane_tile(l_ref[c1, :], reps)
               - lam * (acc_ref[c2, :] / _lane_tile(l_ref[c2, :], reps)))
        ms = jnp.mean(out * out, axis=-1, keepdims=True)
        y = out * lax.rsqrt(ms + SUBLN_EPS) * g_ref[...]
        o_ref[rows_of(blk_ref[grp, slot]), :] = (y * (1.0 - lambda_init)).astype(o_ref.dtype)
        return carry

    lax.fori_loop(0, n_slots, finish, 0)


def _balanced_causal_groups(nq):
    n_groups = 2 if nq % 4 == 0 else 1
    groups = [[] for _ in range(n_groups)]
    for p in range((nq + 1) // 2):
        for blk in sorted({p, nq - 1 - p}):
            groups[p % n_groups].append(blk)
    slots = [[s for s, blk in enumerate(g) for _ in range(blk + 1)] for g in groups]
    keys = [[j for blk in g for j in range(blk + 1)] for g in groups]
    assert len({len(s) for s in slots}) == 1 and len({len(g) for g in groups}) == 1
    return groups, slots, keys


def _diff_attention(proj, bias, lq1, lk1, lq2, lk2, g_subln, batch, seq, n_heads, col0, tile, lambda_init):
    nq = seq // tile
    hd2 = 2 * HEAD_DIM
    c0 = col0 // hd2
    groups, slots, keys = _balanced_causal_groups(nq)
    n_groups, n_slots, n_tiles = len(groups), len(groups[0]), len(slots[0])
    vec = lambda: pl.BlockSpec((1, HEAD_DIM), lambda h, b, g: (0, 0))
    smem = pl.BlockSpec(memory_space=pltpu.SMEM)
    blocks = 4 * _nbytes((seq, hd2), BF16) + _nbytes((3, tile, tile), F32)
    state_rows = 2 * n_slots * tile
    scratch = (_nbytes((state_rows, hd2), BF16) + 2 * _nbytes((2 * tile, tile), F32)
               + 2 * _nbytes((2 * tile, tile), BF16) + 2 * _nbytes((2 * tile, V7X_LANES), F32)
               + 2 * _nbytes((state_rows, V7X_LANES), F32) + _nbytes((state_rows, hd2), F32))
    return pl.pallas_call(
        functools.partial(_diff_kernel, tile=tile, scale=HEAD_DIM ** -0.5, lambda_init=lambda_init,
                          n_tiles=n_tiles, n_slots=n_slots),
        grid=(n_heads, batch, n_groups),
        in_specs=[
            smem, smem, smem,
            pl.BlockSpec((seq, hd2), lambda h, b, g: (b, c0 + h)),
            pl.BlockSpec((seq, hd2), lambda h, b, g: (b, c0 + n_heads + h)),
            pl.BlockSpec((seq, hd2), lambda h, b, g: (b, c0 + 2 * n_heads + h)),
            pl.BlockSpec((1, 3, tile, tile), lambda h, b, g: (h, 0, 0, 0)),
            vec(), vec(), vec(), vec(),
            pl.BlockSpec((1, hd2), lambda h, b, g: (0, 0)),
        ],
        out_specs=pl.BlockSpec((seq, hd2), lambda h, b, g: (b, h)),
        out_shape=jax.ShapeDtypeStruct((batch * seq, n_heads * hd2), BF16),
        scratch_shapes=[
            pltpu.VMEM((state_rows, hd2), BF16),
            pltpu.VMEM((2 * tile, tile), F32), pltpu.VMEM((2 * tile, tile), F32),
            pltpu.VMEM((2 * tile, tile), BF16), pltpu.VMEM((2 * tile, tile), BF16),
            pltpu.VMEM((2 * tile, V7X_LANES), F32), pltpu.VMEM((2 * tile, V7X_LANES), F32),
            pltpu.VMEM((state_rows, V7X_LANES), F32),
            pltpu.VMEM((state_rows, V7X_LANES), F32),
            pltpu.VMEM((state_rows, hd2), F32),
        ],
        compiler_params=pltpu.CompilerParams(
            dimension_semantics=("parallel", "parallel", "arbitrary"),
            vmem_limit_bytes=_vmem_limit(blocks, scratch, 12 * _nbytes((tile, tile), F32)),
        ),
        name="diff_attention",
    )(jnp.asarray(groups, jnp.int32), jnp.asarray(slots, jnp.int32), jnp.asarray(keys, jnp.int32),
      proj, proj, proj, bias, lq1, lk1, lq2, lk2, g_subln)


def _cross_kernel(q_ref, k_ref, v_ref, o_ref, *, n_heads, scale):
    dh = q_ref.shape[1] // n_heads
    for h in range(n_heads):
        cols = slice(h * dh, (h + 1) * dh)
        s = _qk(q_ref[:, cols], k_ref[:, cols]) * scale
        m = jnp.max(s, axis=1, keepdims=True)
        p = jnp.exp(s - m)
        p = p / jnp.sum(p, axis=1, keepdims=True)
        o_ref[:, cols] = jnp.dot(p.astype(BF16), v_ref[:, cols], preferred_element_type=F32).astype(o_ref.dtype)


def _cross_attention(q, k_mem, v_mem, batch, seq, n_mem, tile):
    d = q.shape[1]
    nq = seq // tile
    blocks = 2 * _nbytes((tile, d), BF16) + 2 * _nbytes((n_mem, d), BF16)
    return pl.pallas_call(
        functools.partial(_cross_kernel, n_heads=H_MEM, scale=(d // H_MEM) ** -0.5),
        grid=(batch, nq),
        in_specs=[
            pl.BlockSpec((tile, d), lambda b, i: (b * nq + i, 0)),
            pl.BlockSpec((n_mem, d), lambda b, i: (b, 0)),
            pl.BlockSpec((n_mem, d), lambda b, i: (b, 0)),
        ],
        out_specs=pl.BlockSpec((tile, d), lambda b, i: (b * nq + i, 0)),
        out_shape=jax.ShapeDtypeStruct((batch * seq, d), BF16),
        compiler_params=pltpu.CompilerParams(
            dimension_semantics=("parallel", "arbitrary"),
            vmem_limit_bytes=_vmem_limit(blocks, 0, 8 * _nbytes((tile, n_mem), F32) + _nbytes((tile, d), F32)),
        ),
        name="cross_attention",
    )(q, k_mem, v_mem)


FINAL_NORM_ROWS = 32


def _mlp_kernel(a_ref, ss_ref, hs_ref, wu_ref, wd_ref, gf_ref, o_ref, *, eps):
    f = pl.program_id(1)
    rows = hs_ref.shape[0]

    @pl.when(f == 0)
    def _():
        o_ref[...] = jnp.zeros(o_ref.shape, F32)

    r0 = pl.multiple_of(f * rows, rows)
    o_ref[pl.ds(r0, rows), :] += hs_ref[...]

    r = _row_scale(ss_ref, a_ref.shape[1], eps)
    u = jnp.dot(a_ref[...], wu_ref[...], preferred_element_type=F32)
    act = (jnp.square(jnp.maximum(u, 0.0)) * _lane_tile(r * r, u.shape[1] // V7X_LANES)).astype(BF16)
    o_ref[...] += jnp.dot(act, wd_ref[...], preferred_element_type=F32)

    @pl.when(f == pl.num_programs(1) - 1)
    def _():
        def norm_rows(c, carry):
            c0 = pl.multiple_of(c * FINAL_NORM_ROWS, FINAL_NORM_ROWS)
            y = o_ref[pl.ds(c0, FINAL_NORM_ROWS), :]
            ms = jnp.mean(y * y, axis=-1, keepdims=True)
            o_ref[pl.ds(c0, FINAL_NORM_ROWS), :] = y * lax.rsqrt(ms + eps) * gf_ref[...]
            return carry

        lax.fori_loop(0, o_ref.shape[0] // FINAL_NORM_ROWS, norm_rows, 0)


def _mlp(a, ss, h, w_up, w_down, g_final, *, eps=NORM_EPS, tm=512, tf=512):
    m, d = h.shape
    dff = w_up.shape[1]
    tm = _pick_tile(m, tm)
    tf = _pick_tile(dff, tf)
    nf = dff // tf
    rows = tm // nf
    assert rows * nf == tm and rows % 8 == 0 and tm % FINAL_NORM_ROWS == 0
    nss = ss.shape[1]
    blocks = (2 * _nbytes((d, tf), BF16) + _nbytes((tm, d), BF16) + _nbytes((tm, nss), F32)
              + _nbytes((rows, d), F32) + _nbytes((tm, d), F32))
    return pl.pallas_call(
        functools.partial(_mlp_kernel, eps=eps),
        grid=(m // tm, nf),
        in_specs=[
            pl.BlockSpec((tm, d), lambda i, f: (i, 0)),
            pl.BlockSpec((tm, nss), lambda i, f: (i, 0)),
            pl.BlockSpec((rows, d), lambda i, f: (i * nf + f, 0)),
            pl.BlockSpec((d, tf), lambda i, f: (0, f)),
            pl.BlockSpec((tf, d), lambda i, f: (f, 0)),
            pl.BlockSpec((1, d), lambda i, f: (0, 0)),
        ],
        out_specs=pl.BlockSpec((tm, d), lambda i, f: (i, 0)),
        out_shape=jax.ShapeDtypeStruct((m, d), F32),
        compiler_params=pltpu.CompilerParams(
            dimension_semantics=("parallel", "arbitrary"),
            vmem_limit_bytes=_vmem_limit(blocks, 0, 4 * _nbytes((tm, tf), F32)),
        ),
        name="mlp_final_norm",
    )(a, ss, h, w_up, w_down, g_final.reshape(1, d))


def _attn_tile(seq):
    return _pick_tile(seq, 512)


def _layer(h, mem, l, g_mix, w_in, b_forget, lambda_q1, lambda_k1, lambda_q2, lambda_k2, g_subln, rel_bias,
           w_out, g_cross, g_mem, wq_mem, wk_mem, wv_mem, wo_mem, g_mlp, w_up, w_down, g_final):
    batch, seq, d = h.shape
    n_mem = mem.shape[1]
    h_fox = b_forget.shape[-1]
    h_diff = rel_bias.shape[1]
    w_fox = h_fox * HEAD_DIM
    tile = _attn_tile(seq)
    x2 = h.reshape(batch * seq, d)

    def gained(g, w):
        return (g.astype(F32)[:, None] * w).astype(BF16)

    f0 = 3 * w_fox
    w_t = jnp.swapaxes(w_in[l], 0, 1).astype(BF16)
    n_gate = -(-h_fox // V7X_LANES) * V7X_LANES
    w_gate_t = jnp.pad(w_t[f0:f0 + h_fox], ((0, n_gate - h_fox), (0, 0)))
    b_gate = jnp.pad(b_forget[l].astype(F32), (0, n_gate - h_fox)).reshape(1, n_gate)
    skip = (f0, h_fox)
    if h_fox % BF16_ROWS_PER_TILE:
        w_t, skip = jnp.concatenate([w_t[:f0], w_t[f0 + h_fox:]], axis=0), (f0, 0)

    proj, f_logit = _in_proj(x2, g_mix[l], w_t, skip, w_gate_t)

    cx = _gate_cumsum(f_logit, b_gate, batch, seq, h_fox)
    fox = _fox_attention(proj, cx, batch, seq, h_fox, tile)

    lambda_init = 0.8 - 0.6 * math.exp(-0.3 * l)
    bias = _bias_tiles(rel_bias.astype(F32), tile)
    row = lambda v: v[l].astype(F32).reshape(1, -1)
    diff = _diff_attention(proj, bias, row(lambda_q1), row(lambda_k1), row(lambda_q2), row(lambda_k2),
                           row(g_subln), batch, seq, h_diff, 3 * w_fox, tile, lambda_init)

    w_o = w_out[l].astype(BF16)
    h1, h1b, ss1 = _matmul_resid((fox, 0), (diff, 0), w_fox, (w_o, 0), (w_o, 1), w_fox, x2, name="out_proj")

    q = _scaled_matmul(h1b, ss1, gained(g_cross[l], wq_mem[l]), name="cross_q_proj")
    mem2 = mem.reshape(batch * n_mem, d)
    k_mem = _norm_matmul(mem2, g_mem[l], wk_mem[l].astype(BF16), name="cross_k_proj")
    v_mem = _norm_matmul(mem2, g_mem[l], wv_mem[l].astype(BF16), name="cross_v_proj")
    o = _cross_attention(q, k_mem, v_mem, batch, seq, n_mem, tile)
    w_om = wo_mem[l].astype(BF16)
    half = d // 2
    h2, h2b, ss2 = _matmul_resid((o, 0), (o, 1), half, (w_om, 0), (w_om, 1), half, h1, name="cross_o_proj")

    out = _mlp(h2b, ss2, h2, gained(g_mlp[l], w_up[l]), w_down[l].astype(BF16), g_final)
    return out.reshape(batch, seq, d)


def kernel(x, mem, g_mix, w_in, b_forget, lambda_q1, lambda_k1, lambda_q2, lambda_k2, g_subln, rel_bias, w_out,
           g_cross, g_mem, wq_mem, wk_mem, wv_mem, wo_mem, g_mlp, w_up, w_down, g_final):
    depth = g_mix.shape[0]
    assert depth == 1, "the fused MLP epilogue applies the final norm; only depth 1 is supported"
    return _layer(x, mem, 0, g_mix, w_in, b_forget, lambda_q1, lambda_k1, lambda_q2, lambda_k2, g_subln,
                  rel_bias, w_out, g_cross, g_mem, wq_mem, wk_mem, wv_mem, wo_mem, g_mlp, w_up, w_down,
                  g_final)
```

```python
import functools
import math

import jax
import jax.numpy as jnp
from jax import lax
from jax.experimental import pallas as pl
from jax.experimental.pallas import tpu as pltpu

HEAD_DIM = 128
CHUNK = 64
N_BUCKETS = 32
MAX_DISTANCE = 128
H_MEM = 4
NORM_EPS = 1e-6
SUBLN_EPS = 1e-5
NEG_INF = -1e30
LOG2E = math.log2(math.e)

V7X_LANES = 128
F32_ROWS_PER_TILE = 8
BF16_ROWS_PER_TILE = 16
V7X_VMEM_BYTES = 64 * 1024 * 1024
V7X_VMEM_REQUEST_CAP = V7X_VMEM_BYTES - 8 * 1024 * 1024

F32 = jnp.float32
BF16 = jnp.bfloat16


def _vmem_limit(block_bytes, scratch_bytes=0, temp_bytes=0):
    need = 2 * block_bytes + scratch_bytes + temp_bytes + (4 << 20)
    return int(min(max(need, 16 << 20), V7X_VMEM_REQUEST_CAP))


def _nbytes(shape, dtype):
    return math.prod(shape) * jnp.dtype(dtype).itemsize


def _pick_tile(n, target):
    if n <= target:
        return n
    t = target
    while t >= V7X_LANES:
        if n % t == 0:
            return t
        t -= V7X_LANES
    return n


def _t5_thresholds():
    half = N_BUCKETS // 2
    max_exact = half // 2
    steps = half - max_exact
    ratio = MAX_DISTANCE // max_exact
    thr = []
    for k in range(1, steps):
        n = max_exact
        while n ** steps < (ratio ** k) * (max_exact ** steps):
            n += 1
        thr.append(n)
    return half, max_exact, tuple(thr)


def _norm_matmul_kernel(x_ref, g_ref, w_ref, o_ref, a_ref, *, eps):
    @pl.when(pl.program_id(1) == 0)
    def _():
        x = x_ref[...]
        ms = jnp.mean(x * x, axis=-1, keepdims=True)
        a_ref[...] = (x * lax.rsqrt(ms + eps) * g_ref[...]).astype(BF16)

    o_ref[...] = jnp.dot(a_ref[...], w_ref[...], preferred_element_type=F32).astype(o_ref.dtype)


def _norm_matmul(x, g, w, *, eps=NORM_EPS, tm=512, tn=1024, name):
    m, k = x.shape
    n = w.shape[1]
    tm = _pick_tile(m, tm)
    tn = _pick_tile(n, tn)
    blocks = _nbytes((k, tn), BF16) + _nbytes((tm, tn), BF16)
    return pl.pallas_call(
        functools.partial(_norm_matmul_kernel, eps=eps),
        grid=(m // tm, n // tn),
        in_specs=[
            pl.BlockSpec((tm, k), lambda i, j: (i, 0), pipeline_mode=pl.Buffered(1)),
            pl.BlockSpec((1, k), lambda i, j: (0, 0)),
            pl.BlockSpec((k, tn), lambda i, j: (0, j)),
        ],
        out_specs=pl.BlockSpec((tm, tn), lambda i, j: (i, j)),
        out_shape=jax.ShapeDtypeStruct((m, n), BF16),
        scratch_shapes=[pltpu.VMEM((tm, k), BF16)],
        compiler_params=pltpu.CompilerParams(
            dimension_semantics=("parallel", "arbitrary"),
            vmem_limit_bytes=_vmem_limit(
                blocks, _nbytes((tm, k), F32) + _nbytes((tm, k), BF16), 2 * _nbytes((tm, k), F32)),
        ),
        name=name,
    )(x, g.reshape(1, k), w)


def _row_scale(ss_ref, k, eps):
    ss = ss_ref[:, :V7X_LANES]
    for c in range(1, ss_ref.shape[1] // V7X_LANES):
        ss = ss + ss_ref[:, c * V7X_LANES:(c + 1) * V7X_LANES]
    return lax.rsqrt(ss * (1.0 / k) + eps)


def _in_proj_kernel(xs_ref, g_ref, w_ref, wf_ref, o_ref, f_ref, a0_ref, a1_ref, fs0_ref, fs1_ref, *, eps):
    t = pl.program_id(0)
    j = pl.program_id(1)
    rows = xs_ref.shape[0]

    def step(a_cur, fs_cur, a_nxt, fs_nxt):
        def norm_slice():
            x = xs_ref[...]
            ms = jnp.mean(x * x, axis=-1, keepdims=True)
            a = (x * lax.rsqrt(ms + eps) * g_ref[...]).astype(BF16)
            r0 = pl.multiple_of(j * rows, rows)
            a_nxt[pl.ds(r0, rows), :] = a
            fs_nxt[pl.ds(r0, rows), :] = _qk(a, wf_ref[...])

        @pl.when((t > 0) & (j == 0))
        def _():
            f_ref[...] = fs_cur[...]

        @pl.when(t == 0)
        def _():
            o_ref[...] = jnp.zeros(o_ref.shape, o_ref.dtype)
            f_ref[...] = jnp.zeros(f_ref.shape, f_ref.dtype)
            norm_slice()

        @pl.when(t > 0)
        def _():
            o_ref[...] = _qk(a_cur[...], w_ref[...]).astype(o_ref.dtype)
            norm_slice()

    @pl.when((t & 1) == 0)
    def _():
        step(a1_ref, fs1_ref, a0_ref, fs0_ref)

    @pl.when((t & 1) == 1)
    def _():
        step(a0_ref, fs0_ref, a1_ref, fs1_ref)


def _in_proj(x, g, w_t, skip, wf_t, *, eps=NORM_EPS, tm=1024, tn=768):
    m, k = x.shape
    skip0, n_skip = skip
    n = w_t.shape[0] - n_skip
    tm = _pick_tile(m, tm)
    tn = _pick_tile(math.gcd(n, skip0), tn)
    nj = n // tn
    n_rt = m // tm
    rows = tm // nj
    assert rows * nj == tm and rows % BF16_ROWS_PER_TILE == 0, (tm, nj)
    assert skip0 % tn == 0 and n_skip % BF16_ROWS_PER_TILE == 0, skip

    def w_rows(t, j):
        start = j * tn
        return pl.multiple_of(jnp.where(start < skip0, start, start + n_skip), BF16_ROWS_PER_TILE), 0
    nf = wf_t.shape[0]
    blocks = (_nbytes((k, tn), BF16) + _nbytes((tm, tn), BF16) + _nbytes((k, nf), BF16)
              + _nbytes((tm, nf), F32) + _nbytes((rows, k), F32))
    scratch = 2 * _nbytes((tm, k), BF16) + 2 * _nbytes((tm, nf), F32)
    return pl.pallas_call(
        functools.partial(_in_proj_kernel, eps=eps),
        grid=(n_rt + 1, nj),
        in_specs=[
            pl.BlockSpec((rows, k), lambda t, j: (jnp.minimum(t, n_rt - 1) * nj + j, 0)),
            pl.BlockSpec((1, k), lambda t, j: (0, 0)),
            pl.BlockSpec((pl.Element(tn), pl.Element(k)), w_rows),
            pl.BlockSpec((nf, k), lambda t, j: (0, 0)),
        ],
        out_specs=[
            pl.BlockSpec((tm, tn), lambda t, j: (jnp.where(t == 0, n_rt, t - 1), j)),
            pl.BlockSpec((tm, nf), lambda t, j: (jnp.where(t == 0, n_rt, t - 1), 0)),
        ],
        out_shape=[
            jax.ShapeDtypeStruct((m + tm, n), BF16),
            jax.ShapeDtypeStruct((m + tm, nf), F32),
        ],
        scratch_shapes=[pltpu.VMEM((tm, k), BF16), pltpu.VMEM((tm, k), BF16),
                        pltpu.VMEM((tm, nf), F32), pltpu.VMEM((tm, nf), F32)],
        compiler_params=pltpu.CompilerParams(
            dimension_semantics=("arbitrary", "arbitrary"),
            vmem_limit_bytes=_vmem_limit(blocks, scratch, 2 * _nbytes((tm, tn), F32)),
        ),
        name="in_proj",
    )(x, g.reshape(1, k), w_t, wf_t)


def _scaled_matmul_kernel(a_ref, ss_ref, w_ref, o_ref, *, eps):
    r = _row_scale(ss_ref, a_ref.shape[1], eps)
    acc = jnp.dot(a_ref[...], w_ref[...], preferred_element_type=F32)
    o_ref[...] = (_lane_tile(r, o_ref.shape[1] // V7X_LANES) * acc).astype(o_ref.dtype)


def _scaled_matmul(a, ss, w, *, eps=NORM_EPS, tm=1024, tn=1024, name):
    m, k = a.shape
    n = w.shape[1]
    tm = _pick_tile(m, tm)
    tn = _pick_tile(n, tn)
    nss = ss.shape[1]
    blocks = (_nbytes((tm, k), BF16) + _nbytes((tm, nss), F32) + _nbytes((k, tn), BF16)
              + _nbytes((tm, tn), BF16))
    return pl.pallas_call(
        functools.partial(_scaled_matmul_kernel, eps=eps),
        grid=(m // tm, n // tn),
        in_specs=[
            pl.BlockSpec((tm, k), lambda i, j: (i, 0)),
            pl.BlockSpec((tm, nss), lambda i, j: (i, 0)),
            pl.BlockSpec((k, tn), lambda i, j: (0, j)),
        ],
        out_specs=pl.BlockSpec((tm, tn), lambda i, j: (i, j)),
        out_shape=jax.ShapeDtypeStruct((m, n), BF16),
        compiler_params=pltpu.CompilerParams(
            dimension_semantics=("parallel", "arbitrary"),
            vmem_limit_bytes=_vmem_limit(blocks, 0, 2 * _nbytes((tm, tn), F32)),
        ),
        name=name,
    )(a, ss, w)


def _matmul_resid_kernel(a1_ref, a2_ref, w1_ref, w2_ref, r_ref, o_ref, ob_ref, ss_ref):
    acc = jnp.dot(a1_ref[...], w1_ref[...], preferred_element_type=F32)
    acc = acc + jnp.dot(a2_ref[...], w2_ref[...], preferred_element_type=F32)
    h = r_ref[...] + acc
    o_ref[...] = h
    ob_ref[...] = h.astype(BF16)
    ss_ref[...] = jnp.broadcast_to(jnp.sum(h * h, axis=1, keepdims=True), ss_ref.shape)


def _matmul_resid(a1, a2, a_blk, w1, w2, w_blk, resid, *, tm=1024, tn=512, name):
    (a1, a1c), (a2, a2c) = a1, a2
    (w1, w1r), (w2, w2r) = w1, w2
    m, n = resid.shape
    tm = _pick_tile(m, tm)
    tn = _pick_tile(n, tn)
    blocks = (2 * _nbytes((tm, a_blk), BF16) + 2 * _nbytes((w_blk, tn), BF16)
              + 2 * _nbytes((tm, tn), F32) + _nbytes((tm, tn), BF16) + _nbytes((tm, V7X_LANES), F32))
    return pl.pallas_call(
        _matmul_resid_kernel,
        grid=(m // tm, n // tn),
        in_specs=[
            pl.BlockSpec((tm, a_blk), lambda i, j: (i, a1c)),
            pl.BlockSpec((tm, a_blk), lambda i, j: (i, a2c)),
            pl.BlockSpec((w_blk, tn), lambda i, j: (w1r, j)),
            pl.BlockSpec((w_blk, tn), lambda i, j: (w2r, j)),
            pl.BlockSpec((tm, tn), lambda i, j: (i, j)),
        ],
        out_specs=[
            pl.BlockSpec((tm, tn), lambda i, j: (i, j)),
            pl.BlockSpec((tm, tn), lambda i, j: (i, j)),
            pl.BlockSpec((tm, V7X_LANES), lambda i, j: (i, j)),
        ],
        out_shape=[
            jax.ShapeDtypeStruct((m, n), F32),
            jax.ShapeDtypeStruct((m, n), BF16),
            jax.ShapeDtypeStruct((m, (n // tn) * V7X_LANES), F32),
        ],
        compiler_params=pltpu.CompilerParams(
            dimension_semantics=("parallel", "arbitrary"),
            vmem_limit_bytes=_vmem_limit(blocks, 0, 3 * _nbytes((tm, tn), F32)),
        ),
        name=name,
    )(a1, a2, w1, w2, resid)


def _gate_cumsum_kernel(f_ref, b_ref, o_ref, *, n_heads):
    z = f_ref[...] + b_ref[...]
    x = jnp.minimum(z, 0.0) - jnp.log1p(jnp.exp(-jnp.abs(z)))
    s_len = x.shape[0]
    row = lax.broadcasted_iota(jnp.int32, x.shape, 0)
    d = 1
    while d < s_len:
        x = x + jnp.where(row >= d, pltpu.roll(x, d, axis=0), 0.0)
        d *= 2
    c2 = x * (-LOG2E)
    hi = c2.astype(BF16)
    r1 = c2 - hi.astype(F32)
    mid = r1.astype(BF16)
    lo = (r1 - mid.astype(F32)).astype(BF16)
    lane = lax.broadcasted_iota(jnp.int32, x.shape, 1)
    mid_s = pltpu.roll(mid.astype(F32), n_heads, axis=1)
    lo_s = pltpu.roll(lo.astype(F32), 2 * n_heads, axis=1)
    out = jnp.where(lane < n_heads, hi.astype(F32),
                    jnp.where(lane < 2 * n_heads, mid_s, jnp.where(lane < 3 * n_heads, lo_s, 0.0)))
    o_ref[...] = out.astype(BF16)


def _gate_cumsum(f_logit, b_pad, batch, seq, n_heads):
    nf = f_logit.shape[1]
    assert 3 * n_heads <= nf
    return pl.pallas_call(
        functools.partial(_gate_cumsum_kernel, n_heads=n_heads),
        grid=(batch,),
        in_specs=[
            pl.BlockSpec((seq, nf), lambda b: (b, 0)),
            pl.BlockSpec((1, nf), lambda b: (0, 0)),
        ],
        out_specs=pl.BlockSpec((seq, nf), lambda b: (b, 0)),
        out_shape=jax.ShapeDtypeStruct((batch * seq, nf), BF16),
        compiler_params=pltpu.CompilerParams(
            dimension_semantics=("parallel",),
            vmem_limit_bytes=_vmem_limit(2 * _nbytes((seq, nf), F32), 0, 8 * _nbytes((seq, nf), F32)),
        ),
        name="gate_cumsum",
    )(f_logit, b_pad)


def _bias_tile_kernel(rb_ref, o_ref, *, tile, n_heads):
    h = pl.program_id(0)
    d = pl.program_id(1)
    half, max_exact, thresholds = _t5_thresholds()
    assert tile >= thresholds[-1], "tile 2 must lie wholly in the last (saturated) past bucket"

    @pl.when(d < 2)
    def _():
        t = lax.broadcasted_iota(jnp.int32, (tile, tile), 0)
        s = lax.broadcasted_iota(jnp.int32, (tile, tile), 1)
        rel = s - t - d * tile
        n = jnp.abs(rel)
        large = jnp.full((tile, tile), max_exact, jnp.int32)
        for thr in thresholds:
            large = large + (n >= thr).astype(jnp.int32)
        idx = jnp.where(n < max_exact, n, large) + jnp.where(rel > 0, half, 0)
        val = jnp.zeros((tile, tile), F32)
        for b in range(N_BUCKETS):
            val = jnp.where(idx == b, rb_ref[b * n_heads + h], val)
        shift = int(math.log2(CHUNK))
        allowed = (s >> shift) <= ((t >> shift) + d * tile)
        o_ref[0, 0] = jnp.where(allowed, val * LOG2E, NEG_INF)

    @pl.when(d == 2)
    def _():
        o_ref[0, 0] = jnp.full((tile, tile), rb_ref[(half - 1) * n_heads + h] * LOG2E, F32)


def _bias_tiles(rel_bias, tile):
    n_heads = rel_bias.shape[1]
    return pl.pallas_call(
        functools.partial(_bias_tile_kernel, tile=tile, n_heads=n_heads),
        grid=(n_heads, 3),
        in_specs=[pl.BlockSpec(memory_space=pltpu.SMEM)],
        out_specs=pl.BlockSpec((1, 1, tile, tile), lambda h, d: (h, d, 0, 0)),
        out_shape=jax.ShapeDtypeStruct((n_heads, 3, tile, tile), F32),
        compiler_params=pltpu.CompilerParams(
            dimension_semantics=("parallel", "parallel"),
            vmem_limit_bytes=_vmem_limit(_nbytes((tile, tile), F32), 0, 8 * _nbytes((tile, tile), F32)),
        ),
        name="t5_bias_tiles",
    )(rel_bias.reshape(-1))


def _lane_tile(x, reps):
    return x if reps == 1 else jnp.concatenate([x] * reps, axis=1)


def _qk(q, k):
    return lax.dot_general(q, k, (((1,), (1,)), ((), ())), preferred_element_type=F32)


def _pipelined_tiles(n_tiles, score, softmax, pv):
    if n_tiles == 0:
        return
    score(0, 0)
    if n_tiles == 1:
        softmax(0, 0)
        pv(0, 0)
        return
    score(1, 1)
    softmax(0, 0)
    n_pairs = (n_tiles - 2) // 2

    def pair(t, carry):
        n = 2 * t
        score(n + 2, 0)
        softmax(n + 1, 1)
        pv(n, 0)
        score(n + 3, 1)
        softmax(n + 2, 0)
        pv(n + 1, 1)
        return carry

    lax.fori_loop(0, n_pairs, pair, 0)
    last = n_tiles - 1
    if n_tiles % 2 == 0:
        softmax(last, 1)
        pv(last - 1, 0)
        pv(last, 1)
    else:
        score(last, 0)
        softmax(last - 1, 1)
        pv(last - 2, 0)
        softmax(last, 0)
        pv(last - 1, 1)
        pv(last, 0)


def _causal_tile_lists(nq):
    pairs = [(i, j) for i in range(nq) for j in range(i)]
    qi = jnp.asarray([p[0] for p in pairs] or [0], jnp.int32)
    kj = jnp.asarray([p[1] for p in pairs] or [0], jnp.int32)
    return len(pairs), qi, kj


def _fox_kernel(qi_ref, kj_ref, q_ref, k_ref, v_ref, cx_ref, o_ref, qx_ref, kx_ref, kt_ref, vx_ref,
                s0_ref, s1_ref, p0_ref, p1_ref, a0_ref, a1_ref, m_ref, acc_ref,
                *, tile, scale, n_heads, n_lower):
    s_bufs, p_bufs, a_bufs = (s0_ref, s1_ref), (p0_ref, p1_ref), (a0_ref, a1_ref)
    h = pl.program_id(1)
    hd = HEAD_DIM
    seq = q_ref.shape[0]
    nq = seq // tile

    lane = lax.broadcasted_iota(jnp.int32, (seq, cx_ref.shape[1]), 1)
    sel = (lane == h) | (lane == h + n_heads) | (lane == h + 2 * n_heads)
    qx_ref[:, :hd] = (q_ref[...].astype(F32) * (scale * LOG2E)).astype(BF16)
    qx_ref[:, hd:] = jnp.where(sel, 1.0, 0.0).astype(BF16)
    kx_ref[:, :hd] = k_ref[...]
    kx_ref[:, hd:] = cx_ref[...]
    vx_ref[:, :hd] = v_ref[...]
    vx_ref[:, hd:] = jnp.ones((seq, vx_ref.shape[1] - hd), BF16)
    m_ref[...] = jnp.full(m_ref.shape, NEG_INF, F32)
    acc_ref[...] = jnp.zeros(acc_ref.shape, F32)

    def rows_of(block):
        return pl.ds(pl.multiple_of(block * tile, tile), tile)

    def transpose_keys(kb, carry):
        kt_ref[:, rows_of(kb)] = kx_ref[rows_of(kb), :].astype(F32).T.astype(BF16)
        return carry

    lax.fori_loop(0, nq, transpose_keys, 0)

    def score_stage(qb, kb, slot):
        s_bufs[slot][...] = jnp.dot(qx_ref[rows_of(qb), :], kt_ref[:, rows_of(kb)], preferred_element_type=F32)

    def softmax_stage(qb, slot, masked):
        s = s_bufs[slot][...]
        if masked:
            row = lax.broadcasted_iota(jnp.int32, s.shape, 0)
            col = lax.broadcasted_iota(jnp.int32, s.shape, 1)
            s = jnp.where(col <= row, s, NEG_INF)
        m_prev = m_ref[rows_of(qb), :]
        m_new = jnp.maximum(m_prev, jnp.max(s, axis=1, keepdims=True))
        a_bufs[slot][...] = jnp.exp2(m_prev - m_new)
        p_bufs[slot][...] = jnp.exp2(s - _lane_tile(m_new, tile // V7X_LANES)).astype(BF16)
        m_ref[rows_of(qb), :] = m_new

    def pv_stage(qb, kb, slot):
        pv = jnp.dot(p_bufs[slot][...], vx_ref[rows_of(kb), :], preferred_element_type=F32)
        acc = acc_ref[rows_of(qb), :]
        acc_ref[rows_of(qb), :] = _lane_tile(a_bufs[slot][...], acc.shape[1] // V7X_LANES) * acc + pv

    _pipelined_tiles(
        n_lower,
        lambda n, slot: score_stage(qi_ref[n], kj_ref[n], slot),
        lambda n, slot: softmax_stage(qi_ref[n], slot, False),
        lambda n, slot: pv_stage(qi_ref[n], kj_ref[n], slot))
    _pipelined_tiles(
        nq,
        lambda n, slot: score_stage(n, n, slot),
        lambda n, slot: softmax_stage(n, slot, True),
        lambda n, slot: pv_stage(n, n, slot))

    def finish(qb, carry):
        acc = acc_ref[rows_of(qb), :]
        o_ref[rows_of(qb), :] = (acc[:, :hd] / acc[:, hd:2 * hd]).astype(o_ref.dtype)
        return carry

    lax.fori_loop(0, nq, finish, 0)


def _fox_attention(proj, cx, batch, seq, n_heads, tile):
    nq = seq // tile
    hd = HEAD_DIM
    nx = cx.shape[1]
    n_lower, qi, kj = _causal_tile_lists(nq)
    blocks = 4 * _nbytes((seq, hd), BF16) + _nbytes((seq, nx), BF16)
    scratch = (4 * _nbytes((seq, hd + nx), BF16) + 2 * _nbytes((tile, tile), F32)
               + 2 * _nbytes((tile, tile), BF16) + 2 * _nbytes((tile, V7X_LANES), F32)
               + _nbytes((seq, V7X_LANES), F32) + _nbytes((seq, 2 * hd), F32))
    smem = pl.BlockSpec(memory_space=pltpu.SMEM)
    return pl.pallas_call(
        functools.partial(_fox_kernel, tile=tile, scale=hd ** -0.5, n_heads=n_heads, n_lower=n_lower),
        grid=(batch, n_heads),
        in_specs=[
            smem, smem,
            pl.BlockSpec((seq, hd), lambda b, h: (b, h)),
            pl.BlockSpec((seq, hd), lambda b, h: (b, n_heads + h)),
            pl.BlockSpec((seq, hd), lambda b, h: (b, 2 * n_heads + h)),
            pl.BlockSpec((seq, nx), lambda b, h: (b, 0)),
        ],
        out_specs=pl.BlockSpec((seq, hd), lambda b, h: (b, h)),
        out_shape=jax.ShapeDtypeStruct((batch * seq, n_heads * hd), BF16),
        scratch_shapes=[
            pltpu.VMEM((seq, hd + nx), BF16),
            pltpu.VMEM((seq, hd + nx), BF16),
            pltpu.VMEM((hd + nx, seq), BF16),
            pltpu.VMEM((seq, 2 * hd), BF16),
            pltpu.VMEM((tile, tile), F32), pltpu.VMEM((tile, tile), F32),
            pltpu.VMEM((tile, tile), BF16), pltpu.VMEM((tile, tile), BF16),
            pltpu.VMEM((tile, V7X_LANES), F32), pltpu.VMEM((tile, V7X_LANES), F32),
            pltpu.VMEM((seq, V7X_LANES), F32),
            pltpu.VMEM((seq, 2 * hd), F32),
        ],
        compiler_params=pltpu.CompilerParams(
            dimension_semantics=("parallel", "arbitrary"),
            vmem_limit_bytes=_vmem_limit(blocks, scratch, 12 * _nbytes((tile, tile), F32)),
        ),
        name="fox_attention",
    )(qi, kj, proj, proj, proj, cx)


def _diff_kernel(blk_ref, qs_ref, kj_ref, q_ref, k_ref, v_ref, bias_ref, lq1_ref, lk1_ref, lq2_ref, lk2_ref,
                 g_ref, o_ref, qq_ref, s0_ref, s1_ref, p0_ref, p1_ref, a0_ref, a1_ref, m_ref, l_ref, acc_ref,
                 *, tile, scale, lambda_init, n_tiles, n_slots):
    s_bufs, p_bufs, a_bufs = (s0_ref, s1_ref), (p0_ref, p1_ref), (a0_ref, a1_ref)
    grp = pl.program_id(2)
    hd = HEAD_DIM
    two = 2 * tile

    def rows_of(block, size=tile):
        return pl.ds(pl.multiple_of(block * size, size), size)

    def stage_queries(slot, carry):
        qf = q_ref[rows_of(blk_ref[grp, slot]), :].astype(F32) * (scale * LOG2E)
        lane = lax.broadcasted_iota(jnp.int32, qf.shape, 1)
        base = pl.multiple_of(slot * two, two)
        qq_ref[pl.ds(base, tile), :] = jnp.where(lane < hd, qf, 0.0).astype(BF16)
        qq_ref[pl.ds(base + tile, tile), :] = jnp.where(lane >= hd, qf, 0.0).astype(BF16)
        return carry

    lax.fori_loop(0, n_slots, stage_queries, 0)
    m_ref[...] = jnp.full(m_ref.shape, NEG_INF, F32)
    l_ref[...] = jnp.zeros(l_ref.shape, F32)
    acc_ref[...] = jnp.zeros(acc_ref.shape, F32)

    def score_stage(n, slot):
        qs, kb = qs_ref[grp, n], kj_ref[grp, n]
        qk = _qk(qq_ref[rows_of(qs, two), :], k_ref[rows_of(kb), :])
        bias = bias_ref[0, jnp.minimum(blk_ref[grp, qs] - kb, 2)]
        s_bufs[slot][:tile, :] = qk[:tile] + bias
        s_bufs[slot][tile:, :] = qk[tile:] + bias

    def softmax_stage(n, slot):
        qs = qs_ref[grp, n]
        for c in range(2):
            rows = slice(c * tile, (c + 1) * tile)
            state = pl.ds(pl.multiple_of(qs * two + c * tile, tile), tile)
            s = s_bufs[slot][rows, :]
            m_prev = m_ref[state, :]
            m_new = jnp.maximum(m_prev, jnp.max(s, axis=1, keepdims=True))
            alpha = jnp.exp2(m_prev - m_new)
            p = jnp.exp2(s - _lane_tile(m_new, tile // V7X_LANES))
            l_ref[state, :] = alpha * l_ref[state, :] + jnp.sum(p, axis=1, keepdims=True)
            a_bufs[slot][rows, :] = alpha
            p_bufs[slot][rows, :] = p.astype(BF16)
            m_ref[state, :] = m_new

    def pv_stage(n, slot):
        state = rows_of(qs_ref[grp, n], two)
        pv = jnp.dot(p_bufs[slot][...], v_ref[rows_of(kj_ref[grp, n]), :], preferred_element_type=F32)
        acc = acc_ref[state, :]
        acc_ref[state, :] = _lane_tile(a_bufs[slot][...], acc.shape[1] // V7X_LANES) * acc + pv

    _pipelined_tiles(n_tiles, score_stage, softmax_stage, pv_stage)

    lam = (jnp.exp(jnp.sum(lq1_ref[...] * lk1_ref[...], axis=1, keepdims=True))
           - jnp.exp(jnp.sum(lq2_ref[...] * lk2_ref[...], axis=1, keepdims=True))
           + lambda_init)

    def finish(slot, carry):
        base = pl.multiple_of(slot * two, two)
        c1, c2 = pl.ds(base, tile), pl.ds(base + tile, tile)
        reps = acc_ref.shape[1] // V7X_LANES
        out = (acc_ref[c1, :] / _lane_tile(l_ref[c1, :], reps)
               - lam * (acc_ref[c2, :] / _lane_tile(l_ref[c2, :], reps)))
        ms = jnp.mean(out * out, axis=-1, keepdims=True)
        y = out * lax.rsqrt(ms + SUBLN_EPS) * g_ref[...]
        o_ref[rows_of(blk_ref[grp, slot]), :] = (y * (1.0 - lambda_init)).astype(o_ref.dtype)
        return carry

    lax.fori_loop(0, n_slots, finish, 0)


def _balanced_causal_groups(nq):
    n_groups = 2 if nq % 4 == 0 else 1
    groups = [[] for _ in range(n_groups)]
    for p in range((nq + 1) // 2):
        for blk in sorted({p, nq - 1 - p}):
            groups[p % n_groups].append(blk)
    slots = [[s for s, blk in enumerate(g) for _ in range(blk + 1)] for g in groups]
    keys = [[j for blk in g for j in range(blk + 1)] for g in groups]
    assert len({len(s) for s in slots}) == 1 and len({len(g) for g in groups}) == 1
    return groups, slots, keys


def _diff_attention(proj, bias, lq1, lk1, lq2, lk2, g_subln, batch, seq, n_heads, col0, tile, lambda_init):
    nq = seq // tile
    hd2 = 2 * HEAD_DIM
    c0 = col0 // hd2
    groups, slots, keys = _balanced_causal_groups(nq)
    n_groups, n_slots, n_tiles = len(groups), len(groups[0]), len(slots[0])
    vec = lambda: pl.BlockSpec((1, HEAD_DIM), lambda h, b, g: (0, 0))
    smem = pl.BlockSpec(memory_space=pltpu.SMEM)
    blocks = 4 * _nbytes((seq, hd2), BF16) + _nbytes((3, tile, tile), F32)
    state_rows = 2 * n_slots * tile
    scratch = (_nbytes((state_rows, hd2), BF16) + 2 * _nbytes((2 * tile, tile), F32)
               + 2 * _nbytes((2 * tile, tile), BF16) + 2 * _nbytes((2 * tile, V7X_LANES), F32)
               + 2 * _nbytes((state_rows, V7X_LANES), F32) + _nbytes((state_rows, hd2), F32))
    return pl.pallas_call(
        functools.partial(_diff_kernel, tile=tile, scale=HEAD_DIM ** -0.5, lambda_init=lambda_init,
                          n_tiles=n_tiles, n_slots=n_slots),
        grid=(n_heads, batch, n_groups),
        in_specs=[
            smem, smem, smem,
            pl.BlockSpec((seq, hd2), lambda h, b, g: (b, c0 + h)),
            pl.BlockSpec((seq, hd2), lambda h, b, g: (b, c0 + n_heads + h)),
            pl.BlockSpec((seq, hd2), lambda h, b, g: (b, c0 + 2 * n_heads + h)),
            pl.BlockSpec((1, 3, tile, tile), lambda h, b, g: (h, 0, 0, 0)),
            vec(), vec(), vec(), vec(),
            pl.BlockSpec((1, hd2), lambda h, b, g: (0, 0)),
        ],
        out_specs=pl.BlockSpec((seq, hd2), lambda h, b, g: (b, h)),
        out_shape=jax.ShapeDtypeStruct((batch * seq, n_heads * hd2), BF16),
        scratch_shapes=[
            pltpu.VMEM((state_rows, hd2), BF16),
            pltpu.VMEM((2 * tile, tile), F32), pltpu.VMEM((2 * tile, tile), F32),
            pltpu.VMEM((2 * tile, tile), BF16), pltpu.VMEM((2 * tile, tile), BF16),
            pltpu.VMEM((2 * tile, V7X_LANES), F32), pltpu.VMEM((2 * tile, V7X_LANES), F32),
            pltpu.VMEM((state_rows, V7X_LANES), F32),
            pltpu.VMEM((state_rows, V7X_LANES), F32),
            pltpu.VMEM((state_rows, hd2), F32),
        ],
        compiler_params=pltpu.CompilerParams(
            dimension_semantics=("parallel", "parallel", "arbitrary"),
            vmem_limit_bytes=_vmem_limit(blocks, scratch, 12 * _nbytes((tile, tile), F32)),
        ),
        name="diff_attention",
    )(jnp.asarray(groups, jnp.int32), jnp.asarray(slots, jnp.int32), jnp.asarray(keys, jnp.int32),
      proj, proj, proj, bias, lq1, lk1, lq2, lk2, g_subln)


def _cross_kernel(q_ref, k_ref, v_ref, o_ref, *, n_heads, scale):
    dh = q_ref.shape[1] // n_heads
    for h in range(n_heads):
        cols = slice(h * dh, (h + 1) * dh)
        s = _qk(q_ref[:, cols], k_ref[:, cols]) * scale
        m = jnp.max(s, axis=1, keepdims=True)
        p = jnp.exp(s - m)
        p = p / jnp.sum(p, axis=1, keepdims=True)
        o_ref[:, cols] = jnp.dot(p.astype(BF16), v_ref[:, cols], preferred_element_type=F32).astype(o_ref.dtype)


def _cross_attention(q, k_mem, v_mem, batch, seq, n_mem, tile):
    d = q.shape[1]
    nq = seq // tile
    blocks = 2 * _nbytes((tile, d), BF16) + 2 * _nbytes((n_mem, d), BF16)
    return pl.pallas_call(
        functools.partial(_cross_kernel, n_heads=H_MEM, scale=(d // H_MEM) ** -0.5),
        grid=(batch, nq),
        in_specs=[
            pl.BlockSpec((tile, d), lambda b, i: (b * nq + i, 0)),
            pl.BlockSpec((n_mem, d), lambda b, i: (b, 0)),
            pl.BlockSpec((n_mem, d), lambda b, i: (b, 0)),
        ],
        out_specs=pl.BlockSpec((tile, d), lambda b, i: (b * nq + i, 0)),
        out_shape=jax.ShapeDtypeStruct((batch * seq, d), BF16),
        compiler_params=pltpu.CompilerParams(
            dimension_semantics=("parallel", "arbitrary"),
            vmem_limit_bytes=_vmem_limit(blocks, 0, 8 * _nbytes((tile, n_mem), F32) + _nbytes((tile, d), F32)),
        ),
        name="cross_attention",
    )(q, k_mem, v_mem)


FINAL_NORM_ROWS = 32


def _mlp_kernel(a_ref, ss_ref, hs_ref, wu_ref, wd_ref, gf_ref, o_ref, *, eps):
    f = pl.program_id(1)
    rows = hs_ref.shape[0]

    @pl.when(f == 0)
    def _():
        o_ref[...] = jnp.zeros(o_ref.shape, F32)

    r0 = pl.multiple_of(f * rows, rows)
    o_ref[pl.ds(r0, rows), :] += hs_ref[...]

    r = _row_scale(ss_ref, a_ref.shape[1], eps)
    u = jnp.dot(a_ref[...], wu_ref[...], preferred_element_type=F32)
    act = (jnp.square(jnp.maximum(u, 0.0)) * _lane_tile(r * r, u.shape[1] // V7X_LANES)).astype(BF16)
    o_ref[...] += jnp.dot(act, wd_ref[...], preferred_element_type=F32)

    @pl.when(f == pl.num_programs(1) - 1)
    def _():
        def norm_rows(c, carry):
            c0 = pl.multiple_of(c * FINAL_NORM_ROWS, FINAL_NORM_ROWS)
            y = o_ref[pl.ds(c0, FINAL_NORM_ROWS), :]
            ms = jnp.mean(y * y, axis=-1, keepdims=True)
            o_ref[pl.ds(c0, FINAL_NORM_ROWS), :] = y * lax.rsqrt(ms + eps) * gf_ref[...]
            return carry

        lax.fori_loop(0, o_ref.shape[0] // FINAL_NORM_ROWS, norm_rows, 0)


def _mlp(a, ss, h, w_up, w_down, g_final, *, eps=NORM_EPS, tm=512, tf=512):
    m, d = h.shape
    dff = w_up.shape[1]
    tm = _pick_tile(m, tm)
    tf = _pick_tile(dff, tf)
    nf = dff // tf
    rows = tm // nf
    assert rows * nf == tm and rows % F32_ROWS_PER_TILE == 0 and tm % FINAL_NORM_ROWS == 0
    nss = ss.shape[1]
    blocks = (2 * _nbytes((d, tf), BF16) + _nbytes((tm, d), BF16) + _nbytes((tm, nss), F32)
              + _nbytes((rows, d), F32) + _nbytes((tm, d), F32))
    return pl.pallas_call(
        functools.partial(_mlp_kernel, eps=eps),
        grid=(m // tm, nf),
        in_specs=[
            pl.BlockSpec((tm, d), lambda i, f: (i, 0)),
            pl.BlockSpec((tm, nss), lambda i, f: (i, 0)),
            pl.BlockSpec((rows, d), lambda i, f: (i * nf + f, 0)),
            pl.BlockSpec((d, tf), lambda i, f: (0, f)),
            pl.BlockSpec((tf, d), lambda i, f: (f, 0)),
            pl.BlockSpec((1, d), lambda i, f: (0, 0)),
        ],
        out_specs=pl.BlockSpec((tm, d), lambda i, f: (i, 0)),
        out_shape=jax.ShapeDtypeStruct((m, d), F32),
        compiler_params=pltpu.CompilerParams(
            dimension_semantics=("parallel", "arbitrary"),
            vmem_limit_bytes=_vmem_limit(blocks, 0, 4 * _nbytes((tm, tf), F32)),
        ),
        name="mlp_final_norm",
    )(a, ss, h, w_up, w_down, g_final.reshape(1, d))


def _attn_tile(seq):
    return _pick_tile(seq, 512)


def _layer(h, mem, l, g_mix, w_in, b_forget, lambda_q1, lambda_k1, lambda_q2, lambda_k2, g_subln, rel_bias,
           w_out, g_cross, g_mem, wq_mem, wk_mem, wv_mem, wo_mem, g_mlp, w_up, w_down, g_final):
    batch, seq, d = h.shape
    n_mem = mem.shape[1]
    h_fox = b_forget.shape[-1]
    h_diff = rel_bias.shape[1]
    w_fox = h_fox * HEAD_DIM
    tile = _attn_tile(seq)
    x2 = h.reshape(batch * seq, d)

    def gained(g, w):
        return (g.astype(F32)[:, None] * w).astype(BF16)

    f0 = 3 * w_fox
    w_t = jnp.swapaxes(w_in[l], 0, 1).astype(BF16)
    n_gate = -(-h_fox // V7X_LANES) * V7X_LANES
    w_gate_t = jnp.pad(w_t[f0:f0 + h_fox], ((0, n_gate - h_fox), (0, 0)))
    b_gate = jnp.pad(b_forget[l].astype(F32), (0, n_gate - h_fox)).reshape(1, n_gate)
    skip = (f0, h_fox)
    if h_fox % BF16_ROWS_PER_TILE:
        w_t, skip = jnp.concatenate([w_t[:f0], w_t[f0 + h_fox:]], axis=0), (f0, 0)

    proj, f_logit = _in_proj(x2, g_mix[l], w_t, skip, w_gate_t)

    cx = _gate_cumsum(f_logit, b_gate, batch, seq, h_fox)
    fox = _fox_attention(proj, cx, batch, seq, h_fox, tile)

    lambda_init = 0.8 - 0.6 * math.exp(-0.3 * l)
    bias = _bias_tiles(rel_bias.astype(F32), tile)
    row = lambda v: v[l].astype(F32).reshape(1, -1)
    diff = _diff_attention(proj, bias, row(lambda_q1), row(lambda_k1), row(lambda_q2), row(lambda_k2),
                           row(g_subln), batch, seq, h_diff, 3 * w_fox, tile, lambda_init)

    w_o = w_out[l].astype(BF16)
    h1, h1b, ss1 = _matmul_resid((fox, 0), (diff, 0), w_fox, (w_o, 0), (w_o, 1), w_fox, x2, name="out_proj")

    q = _scaled_matmul(h1b, ss1, gained(g_cross[l], wq_mem[l]), name="cross_q_proj")
    mem2 = mem.reshape(batch * n_mem, d)
    k_mem = _norm_matmul(mem2, g_mem[l], wk_mem[l].astype(BF16), name="cross_k_proj")
    v_mem = _norm_matmul(mem2, g_mem[l], wv_mem[l].astype(BF16), name="cross_v_proj")
    o = _cross_attention(q, k_mem, v_mem, batch, seq, n_mem, tile)
    w_om = wo_mem[l].astype(BF16)
    half = d // 2
    h2, h2b, ss2 = _matmul_resid((o, 0), (o, 1), half, (w_om, 0), (w_om, 1), half, h1, name="cross_o_proj")

    out = _mlp(h2b, ss2, h2, gained(g_mlp[l], w_up[l]), w_down[l].astype(BF16), g_final)
    return out.reshape(batch, seq, d)


def kernel(x, mem, g_mix, w_in, b_forget, lambda_q1, lambda_k1, lambda_q2, lambda_k2, g_subln, rel_bias, w_out,
           g_cross, g_mem, wq_mem, wk_mem, wv_mem, wo_mem, g_mlp, w_up, w_down, g_final):
    depth = g_mix.shape[0]
    assert depth == 1, "the fused MLP epilogue applies the final norm; only depth 1 is supported"
    return _layer(x, mem, 0, g_mix, w_in, b_forget, lambda_q1, lambda_k1, lambda_q2, lambda_k2, g_subln,
                  rel_bias, w_out, g_cross, g_mem, wq_mem, wk_mem, wv_mem, wo_mem, g_mlp, w_up, w_down,
                  g_final)
```

```python
import functools
import math

import jax
import jax.numpy as jnp
from jax import lax
from jax.experimental import pallas as pl
from jax.experimental.pallas import tpu as pltpu

HEAD_DIM = 128
CHUNK = 64
N_BUCKETS = 32
MAX_DISTANCE = 128
H_MEM = 4
NORM_EPS = 1e-6
SUBLN_EPS = 1e-5
NEG_INF = -1e30
LOG2E = math.log2(math.e)

V7X_LANES = 128
F32_ROWS_PER_TILE = 8
BF16_ROWS_PER_TILE = 16
V7X_VMEM_BYTES = 64 * 1024 * 1024
V7X_VMEM_REQUEST_CAP = V7X_VMEM_BYTES - 8 * 1024 * 1024

F32 = jnp.float32
BF16 = jnp.bfloat16


def _vmem_limit(block_bytes, scratch_bytes=0, temp_bytes=0):
    need = 2 * block_bytes + scratch_bytes + temp_bytes + (4 << 20)
    return int(min(max(need, 16 << 20), V7X_VMEM_REQUEST_CAP))


def _nbytes(shape, dtype):
    return math.prod(shape) * jnp.dtype(dtype).itemsize


def _pick_tile(n, target):
    if n <= target:
        return n
    t = target
    while t >= V7X_LANES:
        if n % t == 0:
            return t
        t -= V7X_LANES
    return n


def _t5_thresholds():
    half = N_BUCKETS // 2
    max_exact = half // 2
    steps = half - max_exact
    ratio = MAX_DISTANCE // max_exact
    thr = []
    for k in range(1, steps):
        n = max_exact
        while n ** steps < (ratio ** k) * (max_exact ** steps):
            n += 1
        thr.append(n)
    return half, max_exact, tuple(thr)


def _norm_matmul_kernel(x_ref, g_ref, w_ref, o_ref, a_ref, *, eps):
    @pl.when(pl.program_id(1) == 0)
    def _():
        x = x_ref[...]
        ms = jnp.mean(x * x, axis=-1, keepdims=True)
        a_ref[...] = (x * lax.rsqrt(ms + eps) * g_ref[...]).astype(BF16)

    o_ref[...] = jnp.dot(a_ref[...], w_ref[...], preferred_element_type=F32).astype(o_ref.dtype)


def _norm_matmul(x, g, w, *, eps=NORM_EPS, tm=512, tn=1024, name):
    m, k = x.shape
    n = w.shape[1]
    tm = _pick_tile(m, tm)
    tn = _pick_tile(n, tn)
    blocks = _nbytes((k, tn), BF16) + _nbytes((tm, tn), BF16)
    return pl.pallas_call(
        functools.partial(_norm_matmul_kernel, eps=eps),
        grid=(m // tm, n // tn),
        in_specs=[
            pl.BlockSpec((tm, k), lambda i, j: (i, 0), pipeline_mode=pl.Buffered(1)),
            pl.BlockSpec((1, k), lambda i, j: (0, 0)),
            pl.BlockSpec((k, tn), lambda i, j: (0, j)),
        ],
        out_specs=pl.BlockSpec((tm, tn), lambda i, j: (i, j)),
        out_shape=jax.ShapeDtypeStruct((m, n), BF16),
        scratch_shapes=[pltpu.VMEM((tm, k), BF16)],
        compiler_params=pltpu.CompilerParams(
            dimension_semantics=("parallel", "arbitrary"),
            vmem_limit_bytes=_vmem_limit(
                blocks, _nbytes((tm, k), F32) + _nbytes((tm, k), BF16), 2 * _nbytes((tm, k), F32)),
        ),
        name=name,
    )(x, g.reshape(1, k), w)


def _row_scale(ss_ref, k, eps):
    ss = ss_ref[:, :V7X_LANES]
    for c in range(1, ss_ref.shape[1] // V7X_LANES):
        ss = ss + ss_ref[:, c * V7X_LANES:(c + 1) * V7X_LANES]
    return lax.rsqrt(ss * (1.0 / k) + eps)


def _in_proj_kernel(xs_ref, g_ref, w_ref, wf_ref, cw_ref, cg_ref, o_ref, f_ref, co_ref,
                    a0_ref, a1_ref, fs0_ref, fs1_ref, *, eps):
    t = pl.program_id(0)
    j = pl.program_id(1)
    rows = xs_ref.shape[0]

    def step(a_cur, fs_cur, a_nxt, fs_nxt):
        def norm_slice():
            x = xs_ref[...]
            ms = jnp.mean(x * x, axis=-1, keepdims=True)
            a = (x * lax.rsqrt(ms + eps) * g_ref[...]).astype(BF16)
            r0 = pl.multiple_of(j * rows, rows)
            a_nxt[pl.ds(r0, rows), :] = a
            fs_nxt[pl.ds(r0, rows), :] = _qk(a, wf_ref[...])
            co_ref[...] = (cg_ref[...] * cw_ref[0]).astype(BF16)

        @pl.when((t > 0) & (j == 0))
        def _():
            f_ref[...] = fs_cur[...]

        @pl.when(t == 0)
        def _():
            o_ref[...] = jnp.zeros(o_ref.shape, o_ref.dtype)
            f_ref[...] = jnp.zeros(f_ref.shape, f_ref.dtype)
            norm_slice()

        @pl.when(t > 0)
        def _():
            o_ref[...] = _qk(a_cur[...], w_ref[...]).astype(o_ref.dtype)
            norm_slice()

    @pl.when((t & 1) == 0)
    def _():
        step(a1_ref, fs1_ref, a0_ref, fs0_ref)

    @pl.when((t & 1) == 1)
    def _():
        step(a0_ref, fs0_ref, a1_ref, fs1_ref)


def _in_proj(x, g, w_t, skip, wf_t, cast_w, cast_layer, cast_gain, *, eps=NORM_EPS, tm=1024, tn=768):
    m, k = x.shape
    skip0, n_skip = skip
    n = w_t.shape[0] - n_skip
    tm = _pick_tile(m, tm)
    tn = _pick_tile(math.gcd(n, skip0), tn)
    nj = n // tn
    n_rt = m // tm
    rows = tm // nj
    assert rows * nj == tm and rows % BF16_ROWS_PER_TILE == 0, (tm, nj)
    assert skip0 % tn == 0 and n_skip % BF16_ROWS_PER_TILE == 0, skip

    def w_rows(t, j):
        start = j * tn
        return pl.multiple_of(jnp.where(start < skip0, start, start + n_skip), BF16_ROWS_PER_TILE), 0
    nf = wf_t.shape[0]
    n_steps = (n_rt + 1) * nj
    _, ck, cn = cast_w.shape
    n_cblk = max(c for c in range(1, n_steps + 1) if ck % c == 0 and (ck // c) % BF16_ROWS_PER_TILE == 0)
    crows = ck // n_cblk

    def cast_rows(t, j):
        return jnp.minimum(t * nj + j, n_cblk - 1)

    blocks = (_nbytes((k, tn), BF16) + _nbytes((tm, tn), BF16) + _nbytes((k, nf), BF16)
              + _nbytes((tm, nf), F32) + _nbytes((rows, k), F32)
              + _nbytes((crows, cn), F32) + _nbytes((crows, cn), BF16))
    scratch = 2 * _nbytes((tm, k), BF16) + 2 * _nbytes((tm, nf), F32)
    return pl.pallas_call(
        functools.partial(_in_proj_kernel, eps=eps),
        grid=(n_rt + 1, nj),
        in_specs=[
            pl.BlockSpec((rows, k), lambda t, j: (jnp.minimum(t, n_rt - 1) * nj + j, 0)),
            pl.BlockSpec((1, k), lambda t, j: (0, 0)),
            pl.BlockSpec((pl.Element(tn), pl.Element(k)), w_rows),
            pl.BlockSpec((nf, k), lambda t, j: (0, 0)),
            pl.BlockSpec((1, crows, cn), lambda t, j: (cast_layer, cast_rows(t, j), 0)),
            pl.BlockSpec((crows, 1), lambda t, j: (cast_rows(t, j), 0)),
        ],
        out_specs=[
            pl.BlockSpec((tm, tn), lambda t, j: (jnp.where(t == 0, n_rt, t - 1), j)),
            pl.BlockSpec((tm, nf), lambda t, j: (jnp.where(t == 0, n_rt, t - 1), 0)),
            pl.BlockSpec((crows, cn), lambda t, j: (cast_rows(t, j), 0)),
        ],
        out_shape=[
            jax.ShapeDtypeStruct((m + tm, n), BF16),
            jax.ShapeDtypeStruct((m + tm, nf), F32),
            jax.ShapeDtypeStruct((ck, cn), BF16),
        ],
        scratch_shapes=[pltpu.VMEM((tm, k), BF16), pltpu.VMEM((tm, k), BF16),
                        pltpu.VMEM((tm, nf), F32), pltpu.VMEM((tm, nf), F32)],
        compiler_params=pltpu.CompilerParams(
            dimension_semantics=("arbitrary", "arbitrary"),
            vmem_limit_bytes=_vmem_limit(blocks, scratch, 2 * _nbytes((tm, tn), F32)),
        ),
        name="in_proj",
    )(x, g.reshape(1, k), w_t, wf_t, cast_w, cast_gain.astype(F32).reshape(ck, 1))


def _scaled_matmul_kernel(a_ref, ss_ref, w_ref, o_ref, *, eps):
    r = _row_scale(ss_ref, a_ref.shape[1], eps)
    acc = jnp.dot(a_ref[...], w_ref[...], preferred_element_type=F32)
    o_ref[...] = (_lane_tile(r, o_ref.shape[1] // V7X_LANES) * acc).astype(o_ref.dtype)


def _scaled_matmul(a, ss, w, *, eps=NORM_EPS, tm=1024, tn=1024, name):
    m, k = a.shape
    n = w.shape[1]
    tm = _pick_tile(m, tm)
    tn = _pick_tile(n, tn)
    nss = ss.shape[1]
    blocks = (_nbytes((tm, k), BF16) + _nbytes((tm, nss), F32) + _nbytes((k, tn), BF16)
              + _nbytes((tm, tn), BF16))
    return pl.pallas_call(
        functools.partial(_scaled_matmul_kernel, eps=eps),
        grid=(m // tm, n // tn),
        in_specs=[
            pl.BlockSpec((tm, k), lambda i, j: (i, 0)),
            pl.BlockSpec((tm, nss), lambda i, j: (i, 0)),
            pl.BlockSpec((k, tn), lambda i, j: (0, j)),
        ],
        out_specs=pl.BlockSpec((tm, tn), lambda i, j: (i, j)),
        out_shape=jax.ShapeDtypeStruct((m, n), BF16),
        compiler_params=pltpu.CompilerParams(
            dimension_semantics=("parallel", "arbitrary"),
            vmem_limit_bytes=_vmem_limit(blocks, 0, 2 * _nbytes((tm, tn), F32)),
        ),
        name=name,
    )(a, ss, w)


def _matmul_resid_kernel(a1_ref, a2_ref, w1_ref, w2_ref, r_ref, o_ref, ob_ref, ss_ref):
    acc = jnp.dot(a1_ref[...], w1_ref[...], preferred_element_type=F32)
    acc = acc + jnp.dot(a2_ref[...], w2_ref[...], preferred_element_type=F32)
    h = r_ref[...] + acc
    o_ref[...] = h
    ob_ref[...] = h.astype(BF16)
    ss_ref[...] = jnp.broadcast_to(jnp.sum(h * h, axis=1, keepdims=True), ss_ref.shape)


def _matmul_resid(a1, a2, a_blk, w1, w2, w_blk, resid, *, tm=1024, tn=512, name):
    (a1, a1c), (a2, a2c) = a1, a2
    (w1, w1r), (w2, w2r) = w1, w2
    m, n = resid.shape
    tm = _pick_tile(m, tm)
    tn = _pick_tile(n, tn)
    blocks = (2 * _nbytes((tm, a_blk), BF16) + 2 * _nbytes((w_blk, tn), BF16)
              + 2 * _nbytes((tm, tn), F32) + _nbytes((tm, tn), BF16) + _nbytes((tm, V7X_LANES), F32))
    return pl.pallas_call(
        _matmul_resid_kernel,
        grid=(m // tm, n // tn),
        in_specs=[
            pl.BlockSpec((tm, a_blk), lambda i, j: (i, a1c)),
            pl.BlockSpec((tm, a_blk), lambda i, j: (i, a2c)),
            pl.BlockSpec((w_blk, tn), lambda i, j: (w1r, j)),
            pl.BlockSpec((w_blk, tn), lambda i, j: (w2r, j)),
            pl.BlockSpec((tm, tn), lambda i, j: (i, j)),
        ],
        out_specs=[
            pl.BlockSpec((tm, tn), lambda i, j: (i, j)),
            pl.BlockSpec((tm, tn), lambda i, j: (i, j)),
            pl.BlockSpec((tm, V7X_LANES), lambda i, j: (i, j)),
        ],
        out_shape=[
            jax.ShapeDtypeStruct((m, n), F32),
            jax.ShapeDtypeStruct((m, n), BF16),
            jax.ShapeDtypeStruct((m, (n // tn) * V7X_LANES), F32),
        ],
        compiler_params=pltpu.CompilerParams(
            dimension_semantics=("parallel", "arbitrary"),
            vmem_limit_bytes=_vmem_limit(blocks, 0, 3 * _nbytes((tm, tn), F32)),
        ),
        name=name,
    )(a1, a2, w1, w2, resid)


def _gate_cumsum_kernel(f_ref, b_ref, o_ref, *, n_heads):
    z = f_ref[...] + b_ref[...]
    x = jnp.minimum(z, 0.0) - jnp.log1p(jnp.exp(-jnp.abs(z)))
    s_len = x.shape[0]
    row = lax.broadcasted_iota(jnp.int32, x.shape, 0)
    d = 1
    while d < s_len:
        x = x + jnp.where(row >= d, pltpu.roll(x, d, axis=0), 0.0)
        d *= 2
    c2 = x * (-LOG2E)
    hi = c2.astype(BF16)
    r1 = c2 - hi.astype(F32)
    mid = r1.astype(BF16)
    lo = (r1 - mid.astype(F32)).astype(BF16)
    lane = lax.broadcasted_iota(jnp.int32, x.shape, 1)
    mid_s = pltpu.roll(mid.astype(F32), n_heads, axis=1)
    lo_s = pltpu.roll(lo.astype(F32), 2 * n_heads, axis=1)
    out = jnp.where(lane < n_heads, hi.astype(F32),
                    jnp.where(lane < 2 * n_heads, mid_s, jnp.where(lane < 3 * n_heads, lo_s, 0.0)))
    o_ref[...] = out.astype(BF16)


def _gate_cumsum(f_logit, b_pad, batch, seq, n_heads):
    nf = f_logit.shape[1]
    assert 3 * n_heads <= nf
    return pl.pallas_call(
        functools.partial(_gate_cumsum_kernel, n_heads=n_heads),
        grid=(batch,),
        in_specs=[
            pl.BlockSpec((seq, nf), lambda b: (b, 0)),
            pl.BlockSpec((1, nf), lambda b: (0, 0)),
        ],
        out_specs=pl.BlockSpec((seq, nf), lambda b: (b, 0)),
        out_shape=jax.ShapeDtypeStruct((batch * seq, nf), BF16),
        compiler_params=pltpu.CompilerParams(
            dimension_semantics=("parallel",),
            vmem_limit_bytes=_vmem_limit(2 * _nbytes((seq, nf), F32), 0, 8 * _nbytes((seq, nf), F32)),
        ),
        name="gate_cumsum",
    )(f_logit, b_pad)


def _bias_tile_kernel(rb_ref, o_ref, *, tile, n_heads):
    h = pl.program_id(0)
    d = pl.program_id(1)
    half, max_exact, thresholds = _t5_thresholds()
    assert tile >= thresholds[-1], "tile 2 must lie wholly in the last (saturated) past bucket"

    @pl.when(d < 2)
    def _():
        t = lax.broadcasted_iota(jnp.int32, (tile, tile), 0)
        s = lax.broadcasted_iota(jnp.int32, (tile, tile), 1)
        rel = s - t - d * tile
        n = jnp.abs(rel)
        large = jnp.full((tile, tile), max_exact, jnp.int32)
        for thr in thresholds:
            large = large + (n >= thr).astype(jnp.int32)
        idx = jnp.where(n < max_exact, n, large) + jnp.where(rel > 0, half, 0)
        val = jnp.zeros((tile, tile), F32)
        for b in range(N_BUCKETS):
            val = jnp.where(idx == b, rb_ref[b * n_heads + h], val)
        shift = int(math.log2(CHUNK))
        allowed = (s >> shift) <= ((t >> shift) + d * tile)
        o_ref[0, 0] = jnp.where(allowed, val * LOG2E, NEG_INF)

    @pl.when(d == 2)
    def _():
        o_ref[0, 0] = jnp.full((tile, tile), rb_ref[(half - 1) * n_heads + h] * LOG2E, F32)


def _bias_tiles(rel_bias, tile):
    n_heads = rel_bias.shape[1]
    return pl.pallas_call(
        functools.partial(_bias_tile_kernel, tile=tile, n_heads=n_heads),
        grid=(n_heads, 3),
        in_specs=[pl.BlockSpec(memory_space=pltpu.SMEM)],
        out_specs=pl.BlockSpec((1, 1, tile, tile), lambda h, d: (h, d, 0, 0)),
        out_shape=jax.ShapeDtypeStruct((n_heads, 3, tile, tile), F32),
        compiler_params=pltpu.CompilerParams(
            dimension_semantics=("parallel", "parallel"),
            vmem_limit_bytes=_vmem_limit(_nbytes((tile, tile), F32), 0, 8 * _nbytes((tile, tile), F32)),
        ),
        name="t5_bias_tiles",
    )(rel_bias.reshape(-1))


def _lane_tile(x, reps):
    return x if reps == 1 else jnp.concatenate([x] * reps, axis=1)


def _qk(q, k):
    return lax.dot_general(q, k, (((1,), (1,)), ((), ())), preferred_element_type=F32)


def _pipelined_tiles(n_tiles, score, softmax, pv):
    if n_tiles == 0:
        return
    score(0, 0)
    if n_tiles == 1:
        softmax(0, 0)
        pv(0, 0)
        return
    score(1, 1)
    softmax(0, 0)
    n_pairs = (n_tiles - 2) // 2

    def pair(t, carry):
        n = 2 * t
        score(n + 2, 0)
        softmax(n + 1, 1)
        pv(n, 0)
        score(n + 3, 1)
        softmax(n + 2, 0)
        pv(n + 1, 1)
        return carry

    lax.fori_loop(0, n_pairs, pair, 0)
    last = n_tiles - 1
    if n_tiles % 2 == 0:
        softmax(last, 1)
        pv(last - 1, 0)
        pv(last, 1)
    else:
        score(last, 0)
        softmax(last - 1, 1)
        pv(last - 2, 0)
        softmax(last, 0)
        pv(last - 1, 1)
        pv(last, 0)


def _causal_tile_lists(nq):
    pairs = [(i, j) for i in range(nq) for j in range(i)]
    qi = jnp.asarray([p[0] for p in pairs] or [0], jnp.int32)
    kj = jnp.asarray([p[1] for p in pairs] or [0], jnp.int32)
    return len(pairs), qi, kj


def _fox_kernel(qi_ref, kj_ref, q_ref, k_ref, v_ref, cx_ref, o_ref, qx_ref, kx_ref, kt_ref, vx_ref,
                s0_ref, s1_ref, p0_ref, p1_ref, a0_ref, a1_ref, m_ref, acc_ref,
                *, tile, scale, n_heads, n_lower):
    s_bufs, p_bufs, a_bufs = (s0_ref, s1_ref), (p0_ref, p1_ref), (a0_ref, a1_ref)
    h = pl.program_id(1)
    hd = HEAD_DIM
    seq = q_ref.shape[0]
    nq = seq // tile

    lane = lax.broadcasted_iota(jnp.int32, (seq, cx_ref.shape[1]), 1)
    sel = (lane == h) | (lane == h + n_heads) | (lane == h + 2 * n_heads)
    qx_ref[:, :hd] = (q_ref[...].astype(F32) * (scale * LOG2E)).astype(BF16)
    qx_ref[:, hd:] = jnp.where(sel, 1.0, 0.0).astype(BF16)
    kx_ref[:, :hd] = k_ref[...]
    kx_ref[:, hd:] = cx_ref[...]
    vx_ref[:, :hd] = v_ref[...]
    vx_ref[:, hd:] = jnp.ones((seq, vx_ref.shape[1] - hd), BF16)
    m_ref[...] = jnp.full(m_ref.shape, NEG_INF, F32)
    acc_ref[...] = jnp.zeros(acc_ref.shape, F32)

    def rows_of(block):
        return pl.ds(pl.multiple_of(block * tile, tile), tile)

    def transpose_keys(kb, carry):
        kt_ref[:, rows_of(kb)] = kx_ref[rows_of(kb), :].astype(F32).T.astype(BF16)
        return carry

    lax.fori_loop(0, nq, transpose_keys, 0)

    def score_stage(qb, kb, slot):
        s_bufs[slot][...] = jnp.dot(qx_ref[rows_of(qb), :], kt_ref[:, rows_of(kb)], preferred_element_type=F32)

    def softmax_stage(qb, slot, masked):
        s = s_bufs[slot][...]
        if masked:
            row = lax.broadcasted_iota(jnp.int32, s.shape, 0)
            col = lax.broadcasted_iota(jnp.int32, s.shape, 1)
            s = jnp.where(col <= row, s, NEG_INF)
        m_prev = m_ref[rows_of(qb), :]
        m_new = jnp.maximum(m_prev, jnp.max(s, axis=1, keepdims=True))
        a_bufs[slot][...] = jnp.exp2(m_prev - m_new)
        p_bufs[slot][...] = jnp.exp2(s - _lane_tile(m_new, tile // V7X_LANES)).astype(BF16)
        m_ref[rows_of(qb), :] = m_new

    def pv_stage(qb, kb, slot):
        pv = jnp.dot(p_bufs[slot][...], vx_ref[rows_of(kb), :], preferred_element_type=F32)
        acc = acc_ref[rows_of(qb), :]
        acc_ref[rows_of(qb), :] = _lane_tile(a_bufs[slot][...], acc.shape[1] // V7X_LANES) * acc + pv

    _pipelined_tiles(
        n_lower,
        lambda n, slot: score_stage(qi_ref[n], kj_ref[n], slot),
        lambda n, slot: softmax_stage(qi_ref[n], slot, False),
        lambda n, slot: pv_stage(qi_ref[n], kj_ref[n], slot))
    _pipelined_tiles(
        nq,
        lambda n, slot: score_stage(n, n, slot),
        lambda n, slot: softmax_stage(n, slot, True),
        lambda n, slot: pv_stage(n, n, slot))

    def finish(qb, carry):
        acc = acc_ref[rows_of(qb), :]
        o_ref[rows_of(qb), :] = (acc[:, :hd] / acc[:, hd:2 * hd]).astype(o_ref.dtype)
        return carry

    lax.fori_loop(0, nq, finish, 0)


def _fox_attention(proj, cx, batch, seq, n_heads, tile):
    nq = seq // tile
    hd = HEAD_DIM
    nx = cx.shape[1]
    n_lower, qi, kj = _causal_tile_lists(nq)
    blocks = 4 * _nbytes((seq, hd), BF16) + _nbytes((seq, nx), BF16)
    scratch = (4 * _nbytes((seq, hd + nx), BF16) + 2 * _nbytes((tile, tile), F32)
               + 2 * _nbytes((tile, tile), BF16) + 2 * _nbytes((tile, V7X_LANES), F32)
               + _nbytes((seq, V7X_LANES), F32) + _nbytes((seq, 2 * hd), F32))
    smem = pl.BlockSpec(memory_space=pltpu.SMEM)
    return pl.pallas_call(
        functools.partial(_fox_kernel, tile=tile, scale=hd ** -0.5, n_heads=n_heads, n_lower=n_lower),
        grid=(batch, n_heads),
        in_specs=[
            smem, smem,
            pl.BlockSpec((seq, hd), lambda b, h: (b, h)),
            pl.BlockSpec((seq, hd), lambda b, h: (b, n_heads + h)),
            pl.BlockSpec((seq, hd), lambda b, h: (b, 2 * n_heads + h)),
            pl.BlockSpec((seq, nx), lambda b, h: (b, 0)),
        ],
        out_specs=pl.BlockSpec((seq, hd), lambda b, h: (b, h)),
        out_shape=jax.ShapeDtypeStruct((batch * seq, n_heads * hd), BF16),
        scratch_shapes=[
            pltpu.VMEM((seq, hd + nx), BF16),
            pltpu.VMEM((seq, hd + nx), BF16),
            pltpu.VMEM((hd + nx, seq), BF16),
            pltpu.VMEM((seq, 2 * hd), BF16),
            pltpu.VMEM((tile, tile), F32), pltpu.VMEM((tile, tile), F32),
            pltpu.VMEM((tile, tile), BF16), pltpu.VMEM((tile, tile), BF16),
            pltpu.VMEM((tile, V7X_LANES), F32), pltpu.VMEM((tile, V7X_LANES), F32),
            pltpu.VMEM((seq, V7X_LANES), F32),
            pltpu.VMEM((seq, 2 * hd), F32),
        ],
        compiler_params=pltpu.CompilerParams(
            dimension_semantics=("parallel", "arbitrary"),
            vmem_limit_bytes=_vmem_limit(blocks, scratch, 12 * _nbytes((tile, tile), F32)),
        ),
        name="fox_attention",
    )(qi, kj, proj, proj, proj, cx)


def _diff_kernel(blk_ref, qs_ref, kj_ref, q_ref, k_ref, v_ref, bias_ref, lq1_ref, lk1_ref, lq2_ref, lk2_ref,
                 g_ref, o_ref, qq_ref, s0_ref, s1_ref, p0_ref, p1_ref, a0_ref, a1_ref, m_ref, l_ref, acc_ref,
                 *, tile, scale, lambda_init, n_tiles, n_slots):
    s_bufs, p_bufs, a_bufs = (s0_ref, s1_ref), (p0_ref, p1_ref), (a0_ref, a1_ref)
    grp = pl.program_id(2)
    hd = HEAD_DIM
    two = 2 * tile

    def rows_of(block, size=tile):
        return pl.ds(pl.multiple_of(block * size, size), size)

    def stage_queries(slot, carry):
        qf = q_ref[rows_of(blk_ref[grp, slot]), :].astype(F32) * (scale * LOG2E)
        lane = lax.broadcasted_iota(jnp.int32, qf.shape, 1)
        base = pl.multiple_of(slot * two, two)
        qq_ref[pl.ds(base, tile), :] = jnp.where(lane < hd, qf, 0.0).astype(BF16)
        qq_ref[pl.ds(base + tile, tile), :] = jnp.where(lane >= hd, qf, 0.0).astype(BF16)
        return carry

    lax.fori_loop(0, n_slots, stage_queries, 0)
    m_ref[...] = jnp.full(m_ref.shape, NEG_INF, F32)
    l_ref[...] = jnp.zeros(l_ref.shape, F32)
    acc_ref[...] = jnp.zeros(acc_ref.shape, F32)

    def score_stage(n, slot):
        qs, kb = qs_ref[grp, n], kj_ref[grp, n]
        qk = _qk(qq_ref[rows_of(qs, two), :], k_ref[rows_of(kb), :])
        bias = bias_ref[0, jnp.minimum(blk_ref[grp, qs] - kb, 2)]
        s_bufs[slot][:tile, :] = qk[:tile] + bias
        s_bufs[slot][tile:, :] = qk[tile:] + bias

    def softmax_stage(n, slot):
        qs = qs_ref[grp, n]
        for c in range(2):
            rows = slice(c * tile, (c + 1) * tile)
            state = pl.ds(pl.multiple_of(qs * two + c * tile, tile), tile)
            s = s_bufs[slot][rows, :]
            m_prev = m_ref[state, :]
            m_new = jnp.maximum(m_prev, jnp.max(s, axis=1, keepdims=True))
            alpha = jnp.exp2(m_prev - m_new)
            p = jnp.exp2(s - _lane_tile(m_new, tile // V7X_LANES))
            l_ref[state, :] = alpha * l_ref[state, :] + jnp.sum(p, axis=1, keepdims=True)
            a_bufs[slot][rows, :] = alpha
            p_bufs[slot][rows, :] = p.astype(BF16)
            m_ref[state, :] = m_new

    def pv_stage(n, slot):
        state = rows_of(qs_ref[grp, n], two)
        pv = jnp.dot(p_bufs[slot][...], v_ref[rows_of(kj_ref[grp, n]), :], preferred_element_type=F32)
        acc = acc_ref[state, :]
        acc_ref[state, :] = _lane_tile(a_bufs[slot][...], acc.shape[1] // V7X_LANES) * acc + pv

    _pipelined_tiles(n_tiles, score_stage, softmax_stage, pv_stage)

    lam = (jnp.exp(jnp.sum(lq1_ref[...] * lk1_ref[...], axis=1, keepdims=True))
           - jnp.exp(jnp.sum(lq2_ref[...] * lk2_ref[...], axis=1, keepdims=True))
           + lambda_init)

    def finish(slot, carry):
        base = pl.multiple_of(slot * two, two)
        c1, c2 = pl.ds(base, tile), pl.ds(base + tile, tile)
        reps = acc_ref.shape[1] // V7X_LANES
        out = (acc_ref[c1, :] / _lane_tile(l_ref[c1, :], reps)
               - lam * (acc_ref[c2, :] / _lane_tile(l_ref[c2, :], reps)))
        ms = jnp.mean(out * out, axis=-1, keepdims=True)
        y = out * lax.rsqrt(ms + SUBLN_EPS) * g_ref[...]
        o_ref[rows_of(blk_ref[grp, slot]), :] = (y * (1.0 - lambda_init)).astype(o_ref.dtype)
        return carry

    lax.fori_loop(0, n_slots, finish, 0)


def _balanced_causal_groups(nq):
    n_groups = 2 if nq % 4 == 0 else 1
    groups = [[] for _ in range(n_groups)]
    for p in range((nq + 1) // 2):
        for blk in sorted({p, nq - 1 - p}):
            groups[p % n_groups].append(blk)
    slots = [[s for s, blk in enumerate(g) for _ in range(blk + 1)] for g in groups]
    keys = [[j for blk in g for j in range(blk + 1)] for g in groups]
    assert len({len(s) for s in slots}) == 1 and len({len(g) for g in groups}) == 1
    return groups, slots, keys


def _diff_attention(proj, bias, lq1, lk1, lq2, lk2, g_subln, batch, seq, n_heads, col0, tile, lambda_init):
    nq = seq // tile
    hd2 = 2 * HEAD_DIM
    c0 = col0 // hd2
    groups, slots, keys = _balanced_causal_groups(nq)
    n_groups, n_slots, n_tiles = len(groups), len(groups[0]), len(slots[0])
    vec = lambda: pl.BlockSpec((1, HEAD_DIM), lambda h, b, g: (0, 0))
    smem = pl.BlockSpec(memory_space=pltpu.SMEM)
    blocks = 4 * _nbytes((seq, hd2), BF16) + _nbytes((3, tile, tile), F32)
    state_rows = 2 * n_slots * tile
    scratch = (_nbytes((state_rows, hd2), BF16) + 2 * _nbytes((2 * tile, tile), F32)
               + 2 * _nbytes((2 * tile, tile), BF16) + 2 * _nbytes((2 * tile, V7X_LANES), F32)
               + 2 * _nbytes((state_rows, V7X_LANES), F32) + _nbytes((state_rows, hd2), F32))
    return pl.pallas_call(
        functools.partial(_diff_kernel, tile=tile, scale=HEAD_DIM ** -0.5, lambda_init=lambda_init,
                          n_tiles=n_tiles, n_slots=n_slots),
        grid=(n_heads, batch, n_groups),
        in_specs=[
            smem, smem, smem,
            pl.BlockSpec((seq, hd2), lambda h, b, g: (b, c0 + h)),
            pl.BlockSpec((seq, hd2), lambda h, b, g: (b, c0 + n_heads + h)),
            pl.BlockSpec((seq, hd2), lambda h, b, g: (b, c0 + 2 * n_heads + h)),
            pl.BlockSpec((1, 3, tile, tile), lambda h, b, g: (h, 0, 0, 0)),
            vec(), vec(), vec(), vec(),
            pl.BlockSpec((1, hd2), lambda h, b, g: (0, 0)),
        ],
        out_specs=pl.BlockSpec((seq, hd2), lambda h, b, g: (b, h)),
        out_shape=jax.ShapeDtypeStruct((batch * seq, n_heads * hd2), BF16),
        scratch_shapes=[
            pltpu.VMEM((state_rows, hd2), BF16),
            pltpu.VMEM((2 * tile, tile), F32), pltpu.VMEM((2 * tile, tile), F32),
            pltpu.VMEM((2 * tile, tile), BF16), pltpu.VMEM((2 * tile, tile), BF16),
            pltpu.VMEM((2 * tile, V7X_LANES), F32), pltpu.VMEM((2 * tile, V7X_LANES), F32),
            pltpu.VMEM((state_rows, V7X_LANES), F32),
            pltpu.VMEM((state_rows, V7X_LANES), F32),
            pltpu.VMEM((state_rows, hd2), F32),
        ],
        compiler_params=pltpu.CompilerParams(
            dimension_semantics=("parallel", "parallel", "arbitrary"),
            vmem_limit_bytes=_vmem_limit(blocks, scratch, 12 * _nbytes((tile, tile), F32)),
        ),
        name="diff_attention",
    )(jnp.asarray(groups, jnp.int32), jnp.asarray(slots, jnp.int32), jnp.asarray(keys, jnp.int32),
      proj, proj, proj, bias, lq1, lk1, lq2, lk2, g_subln)


def _cross_kernel(q_ref, k_ref, v_ref, o_ref, *, n_heads, scale):
    dh = q_ref.shape[1] // n_heads
    for h in range(n_heads):
        cols = slice(h * dh, (h + 1) * dh)
        s = _qk(q_ref[:, cols], k_ref[:, cols]) * scale
        m = jnp.max(s, axis=1, keepdims=True)
        p = jnp.exp(s - m)
        p = p / jnp.sum(p, axis=1, keepdims=True)
        o_ref[:, cols] = jnp.dot(p.astype(BF16), v_ref[:, cols], preferred_element_type=F32).astype(o_ref.dtype)


def _cross_attention(q, k_mem, v_mem, batch, seq, n_mem, tile):
    d = q.shape[1]
    nq = seq // tile
    blocks = 2 * _nbytes((tile, d), BF16) + 2 * _nbytes((n_mem, d), BF16)
    return pl.pallas_call(
        functools.partial(_cross_kernel, n_heads=H_MEM, scale=(d // H_MEM) ** -0.5),
        grid=(batch, nq),
        in_specs=[
            pl.BlockSpec((tile, d), lambda b, i: (b * nq + i, 0)),
            pl.BlockSpec((n_mem, d), lambda b, i: (b, 0)),
            pl.BlockSpec((n_mem, d), lambda b, i: (b, 0)),
        ],
        out_specs=pl.BlockSpec((tile, d), lambda b, i: (b * nq + i, 0)),
        out_shape=jax.ShapeDtypeStruct((batch * seq, d), BF16),
        compiler_params=pltpu.CompilerParams(
            dimension_semantics=("parallel", "arbitrary"),
            vmem_limit_bytes=_vmem_limit(blocks, 0, 8 * _nbytes((tile, n_mem), F32) + _nbytes((tile, d), F32)),
        ),
        name="cross_attention",
    )(q, k_mem, v_mem)


FINAL_NORM_ROWS = 32


def _mlp_kernel(a_ref, ss_ref, hs_ref, wu_ref, wd_ref, gf_ref, o_ref, *, eps):
    f = pl.program_id(1)
    rows = hs_ref.shape[0]

    @pl.when(f == 0)
    def _():
        o_ref[...] = jnp.zeros(o_ref.shape, F32)

    r0 = pl.multiple_of(f * rows, rows)
    o_ref[pl.ds(r0, rows), :] += hs_ref[...]

    r = _row_scale(ss_ref, a_ref.shape[1], eps)
    u = jnp.dot(a_ref[...], wu_ref[...], preferred_element_type=F32)
    act = (jnp.square(jnp.maximum(u, 0.0)) * _lane_tile(r * r, u.shape[1] // V7X_LANES)).astype(BF16)
    o_ref[...] += jnp.dot(act, wd_ref[...], preferred_element_type=F32)

    @pl.when(f == pl.num_programs(1) - 1)
    def _():
        def norm_rows(c, carry):
            c0 = pl.multiple_of(c * FINAL_NORM_ROWS, FINAL_NORM_ROWS)
            y = o_ref[pl.ds(c0, FINAL_NORM_ROWS), :]
            ms = jnp.mean(y * y, axis=-1, keepdims=True)
            o_ref[pl.ds(c0, FINAL_NORM_ROWS), :] = y * lax.rsqrt(ms + eps) * gf_ref[...]
            return carry

        lax.fori_loop(0, o_ref.shape[0] // FINAL_NORM_ROWS, norm_rows, 0)


def _mlp(a, ss, h, w_up, w_down, g_final, *, eps=NORM_EPS, tm=512, tf=512):
    m, d = h.shape
    dff = w_up.shape[1]
    tm = _pick_tile(m, tm)
    tf = _pick_tile(dff, tf)
    nf = dff // tf
    rows = tm // nf
    assert rows * nf == tm and rows % F32_ROWS_PER_TILE == 0 and tm % FINAL_NORM_ROWS == 0
    nss = ss.shape[1]
    blocks = (2 * _nbytes((d, tf), BF16) + _nbytes((tm, d), BF16) + _nbytes((tm, nss), F32)
              + _nbytes((rows, d), F32) + _nbytes((tm, d), F32))
    return pl.pallas_call(
        functools.partial(_mlp_kernel, eps=eps),
        grid=(m // tm, nf),
        in_specs=[
            pl.BlockSpec((tm, d), lambda i, f: (i, 0)),
            pl.BlockSpec((tm, nss), lambda i, f: (i, 0)),
            pl.BlockSpec((rows, d), lambda i, f: (i * nf + f, 0)),
            pl.BlockSpec((d, tf), lambda i, f: (0, f)),
            pl.BlockSpec((tf, d), lambda i, f: (f, 0)),
            pl.BlockSpec((1, d), lambda i, f: (0, 0)),
        ],
        out_specs=pl.BlockSpec((tm, d), lambda i, f: (i, 0)),
        out_shape=jax.ShapeDtypeStruct((m, d), F32),
        compiler_params=pltpu.CompilerParams(
            dimension_semantics=("parallel", "arbitrary"),
            vmem_limit_bytes=_vmem_limit(blocks, 0, 4 * _nbytes((tm, tf), F32)),
        ),
        name="mlp_final_norm",
    )(a, ss, h, w_up, w_down, g_final.reshape(1, d))


def _attn_tile(seq):
    return _pick_tile(seq, 512)


def _layer(h, mem, l, g_mix, w_in, b_forget, lambda_q1, lambda_k1, lambda_q2, lambda_k2, g_subln, rel_bias,
           w_out, g_cross, g_mem, wq_mem, wk_mem, wv_mem, wo_mem, g_mlp, w_up, w_down, g_final):
    batch, seq, d = h.shape
    n_mem = mem.shape[1]
    h_fox = b_forget.shape[-1]
    h_diff = rel_bias.shape[1]
    w_fox = h_fox * HEAD_DIM
    tile = _attn_tile(seq)
    x2 = h.reshape(batch * seq, d)

    def gained(g, w):
        return (g.astype(F32)[:, None] * w).astype(BF16)

    f0 = 3 * w_fox
    w_t = jnp.swapaxes(w_in[l], 0, 1).astype(BF16)
    n_gate = -(-h_fox // V7X_LANES) * V7X_LANES
    w_gate_t = jnp.pad(w_t[f0:f0 + h_fox], ((0, n_gate - h_fox), (0, 0)))
    b_gate = jnp.pad(b_forget[l].astype(F32), (0, n_gate - h_fox)).reshape(1, n_gate)
    skip = (f0, h_fox)
    if h_fox % BF16_ROWS_PER_TILE:
        w_t, skip = jnp.concatenate([w_t[:f0], w_t[f0 + h_fox:]], axis=0), (f0, 0)

    proj, f_logit, w_up_b = _in_proj(x2, g_mix[l], w_t, skip, w_gate_t, w_up, l, g_mlp[l])

    cx = _gate_cumsum(f_logit, b_gate, batch, seq, h_fox)
    fox = _fox_attention(proj, cx, batch, seq, h_fox, tile)

    lambda_init = 0.8 - 0.6 * math.exp(-0.3 * l)
    bias = _bias_tiles(rel_bias.astype(F32), tile)
    row = lambda v: v[l].astype(F32).reshape(1, -1)
    diff = _diff_attention(proj, bias, row(lambda_q1), row(lambda_k1), row(lambda_q2), row(lambda_k2),
                           row(g_subln), batch, seq, h_diff, 3 * w_fox, tile, lambda_init)

    w_o = w_out[l].astype(BF16)
    h1, h1b, ss1 = _matmul_resid((fox, 0), (diff, 0), w_fox, (w_o, 0), (w_o, 1), w_fox, x2, name="out_proj")

    q = _scaled_matmul(h1b, ss1, gained(g_cross[l], wq_mem[l]), name="cross_q_proj")
    mem2 = mem.reshape(batch * n_mem, d)
    k_mem = _norm_matmul(mem2, g_mem[l], wk_mem[l].astype(BF16), name="cross_k_proj")
    v_mem = _norm_matmul(mem2, g_mem[l], wv_mem[l].astype(BF16), name="cross_v_proj")
    o = _cross_attention(q, k_mem, v_mem, batch, seq, n_mem, tile)
    w_om = wo_mem[l].astype(BF16)
    half = d // 2
    h2, h2b, ss2 = _matmul_resid((o, 0), (o, 1), half, (w_om, 0), (w_om, 1), half, h1, name="cross_o_proj")

    out = _mlp(h2b, ss2, h2, w_up_b, w_down[l].astype(BF16), g_final)
    return out.reshape(batch, seq, d)


def kernel(x, mem, g_mix, w_in, b_forget, lambda_q1, lambda_k1, lambda_q2, lambda_k2, g_subln, rel_bias, w_out,
           g_cross, g_mem, wq_mem, wk_mem, wv_mem, wo_mem, g_mlp, w_up, w_down, g_final):
    depth = g_mix.shape[0]
    assert depth == 1, "the fused MLP epilogue applies the final norm; only depth 1 is supported"
    return _layer(x, mem, 0, g_mix, w_in, b_forget, lambda_q1, lambda_k1, lambda_q2, lambda_k2, g_subln,
                  rel_bias, w_out, g_cross, g_mem, wq_mem, wk_mem, wv_mem, wo_mem, g_mlp, w_up, w_down,
                  g_final)
```

```python
import functools
import math

import jax
import jax.numpy as jnp
from jax import lax
from jax.experimental import pallas as pl
from jax.experimental.pallas import tpu as pltpu

HEAD_DIM = 128
CHUNK = 64
N_BUCKETS = 32
MAX_DISTANCE = 128
H_MEM = 4
NORM_EPS = 1e-6
SUBLN_EPS = 1e-5
NEG_INF = -1e30
LOG2E = math.log2(math.e)

V7X_LANES = 128
F32_ROWS_PER_TILE = 8
BF16_ROWS_PER_TILE = 16
V7X_VMEM_BYTES = 64 * 1024 * 1024
V7X_VMEM_REQUEST_CAP = V7X_VMEM_BYTES - 8 * 1024 * 1024

F32 = jnp.float32
BF16 = jnp.bfloat16


def _vmem_limit(block_bytes, scratch_bytes=0, temp_bytes=0):
    need = 2 * block_bytes + scratch_bytes + temp_bytes + (4 << 20)
    return int(min(max(need, 16 << 20), V7X_VMEM_REQUEST_CAP))


def _nbytes(shape, dtype):
    return math.prod(shape) * jnp.dtype(dtype).itemsize


def _pick_tile(n, target):
    if n <= target:
        return n
    t = target
    while t >= V7X_LANES:
        if n % t == 0:
            return t
        t -= V7X_LANES
    return n


def _t5_thresholds():
    half = N_BUCKETS // 2
    max_exact = half // 2
    steps = half - max_exact
    ratio = MAX_DISTANCE // max_exact
    thr = []
    for k in range(1, steps):
        n = max_exact
        while n ** steps < (ratio ** k) * (max_exact ** steps):
            n += 1
        thr.append(n)
    return half, max_exact, tuple(thr)


def _norm_matmul_kernel(x_ref, g_ref, w_ref, o_ref, a_ref, *, eps):
    @pl.when(pl.program_id(1) == 0)
    def _():
        x = x_ref[...]
        ms = jnp.mean(x * x, axis=-1, keepdims=True)
        a_ref[...] = (x * lax.rsqrt(ms + eps) * g_ref[...]).astype(BF16)

    o_ref[...] = jnp.dot(a_ref[...], w_ref[...], preferred_element_type=F32).astype(o_ref.dtype)


def _norm_matmul(x, g, w, *, eps=NORM_EPS, tm=512, tn=1024, name):
    m, k = x.shape
    n = w.shape[1]
    tm = _pick_tile(m, tm)
    tn = _pick_tile(n, tn)
    blocks = _nbytes((k, tn), BF16) + _nbytes((tm, tn), BF16)
    return pl.pallas_call(
        functools.partial(_norm_matmul_kernel, eps=eps),
        grid=(m // tm, n // tn),
        in_specs=[
            pl.BlockSpec((tm, k), lambda i, j: (i, 0), pipeline_mode=pl.Buffered(1)),
            pl.BlockSpec((1, k), lambda i, j: (0, 0)),
            pl.BlockSpec((k, tn), lambda i, j: (0, j)),
        ],
        out_specs=pl.BlockSpec((tm, tn), lambda i, j: (i, j)),
        out_shape=jax.ShapeDtypeStruct((m, n), BF16),
        scratch_shapes=[pltpu.VMEM((tm, k), BF16)],
        compiler_params=pltpu.CompilerParams(
            dimension_semantics=("parallel", "arbitrary"),
            vmem_limit_bytes=_vmem_limit(
                blocks, _nbytes((tm, k), F32) + _nbytes((tm, k), BF16), 2 * _nbytes((tm, k), F32)),
        ),
        name=name,
    )(x, g.reshape(1, k), w)


def _row_scale(ss_ref, k, eps):
    ss = ss_ref[:, :V7X_LANES]
    for c in range(1, ss_ref.shape[1] // V7X_LANES):
        ss = ss + ss_ref[:, c * V7X_LANES:(c + 1) * V7X_LANES]
    return lax.rsqrt(ss * (1.0 / k) + eps)


def _in_proj_kernel(xs_ref, g_ref, w_ref, wf_ref, cw_ref, cg_ref, dw_ref, o_ref, f_ref, co_ref, do_ref,
                    a0_ref, a1_ref, fs0_ref, fs1_ref, *, eps):
    t = pl.program_id(0)
    j = pl.program_id(1)
    rows = xs_ref.shape[0]

    def step(a_cur, fs_cur, a_nxt, fs_nxt):
        def norm_slice():
            x = xs_ref[...]
            ms = jnp.mean(x * x, axis=-1, keepdims=True)
            a = (x * lax.rsqrt(ms + eps) * g_ref[...]).astype(BF16)
            r0 = pl.multiple_of(j * rows, rows)
            a_nxt[pl.ds(r0, rows), :] = a
            fs_nxt[pl.ds(r0, rows), :] = _qk(a, wf_ref[...])
            co_ref[...] = (cg_ref[...] * cw_ref[0]).astype(BF16)
            do_ref[...] = dw_ref[0].astype(BF16)

        @pl.when((t > 0) & (j == 0))
        def _():
            f_ref[...] = fs_cur[...]

        @pl.when(t == 0)
        def _():
            o_ref[...] = jnp.zeros(o_ref.shape, o_ref.dtype)
            f_ref[...] = jnp.zeros(f_ref.shape, f_ref.dtype)
            norm_slice()

        @pl.when(t > 0)
        def _():
            o_ref[...] = _qk(a_cur[...], w_ref[...]).astype(o_ref.dtype)
            norm_slice()

    @pl.when((t & 1) == 0)
    def _():
        step(a1_ref, fs1_ref, a0_ref, fs0_ref)

    @pl.when((t & 1) == 1)
    def _():
        step(a0_ref, fs0_ref, a1_ref, fs1_ref)


def _in_proj(x, g, w_t, skip, wf_t, cast_w, cast_layer, cast_gain, plain_w, *, eps=NORM_EPS, tm=1024, tn=768):
    m, k = x.shape
    skip0, n_skip = skip
    n = w_t.shape[0] - n_skip
    tm = _pick_tile(m, tm)
    tn = _pick_tile(math.gcd(n, skip0), tn)
    nj = n // tn
    n_rt = m // tm
    rows = tm // nj
    assert rows * nj == tm and rows % BF16_ROWS_PER_TILE == 0, (tm, nj)
    assert skip0 % tn == 0 and n_skip % BF16_ROWS_PER_TILE == 0, skip

    def w_rows(t, j):
        start = j * tn
        return pl.multiple_of(jnp.where(start < skip0, start, start + n_skip), BF16_ROWS_PER_TILE), 0
    nf = wf_t.shape[0]
    n_steps = (n_rt + 1) * nj
    _, ck, cn = cast_w.shape
    n_cblk = max(c for c in range(1, n_steps + 1) if ck % c == 0 and (ck // c) % BF16_ROWS_PER_TILE == 0)
    crows = ck // n_cblk

    def cast_rows(t, j):
        return jnp.minimum(t * nj + j, n_cblk - 1)

    _, dk, dn = plain_w.shape
    assert dk % n_cblk == 0 and (dk // n_cblk) % BF16_ROWS_PER_TILE == 0
    drows = dk // n_cblk

    blocks = (_nbytes((k, tn), BF16) + _nbytes((tm, tn), BF16) + _nbytes((k, nf), BF16)
              + _nbytes((tm, nf), F32) + _nbytes((rows, k), F32)
              + _nbytes((crows, cn), F32) + _nbytes((crows, cn), BF16)
              + _nbytes((drows, dn), F32) + _nbytes((drows, dn), BF16))
    scratch = 2 * _nbytes((tm, k), BF16) + 2 * _nbytes((tm, nf), F32)
    return pl.pallas_call(
        functools.partial(_in_proj_kernel, eps=eps),
        grid=(n_rt + 1, nj),
        in_specs=[
            pl.BlockSpec((rows, k), lambda t, j: (jnp.minimum(t, n_rt - 1) * nj + j, 0)),
            pl.BlockSpec((1, k), lambda t, j: (0, 0)),
            pl.BlockSpec((pl.Element(tn), pl.Element(k)), w_rows),
            pl.BlockSpec((nf, k), lambda t, j: (0, 0)),
            pl.BlockSpec((1, crows, cn), lambda t, j: (cast_layer, cast_rows(t, j), 0)),
            pl.BlockSpec((crows, 1), lambda t, j: (cast_rows(t, j), 0)),
            pl.BlockSpec((1, drows, dn), lambda t, j: (cast_layer, cast_rows(t, j), 0)),
        ],
        out_specs=[
            pl.BlockSpec((tm, tn), lambda t, j: (jnp.where(t == 0, n_rt, t - 1), j)),
            pl.BlockSpec((tm, nf), lambda t, j: (jnp.where(t == 0, n_rt, t - 1), 0)),
            pl.BlockSpec((crows, cn), lambda t, j: (cast_rows(t, j), 0)),
            pl.BlockSpec((drows, dn), lambda t, j: (cast_rows(t, j), 0)),
        ],
        out_shape=[
            jax.ShapeDtypeStruct((m + tm, n), BF16),
            jax.ShapeDtypeStruct((m + tm, nf), F32),
            jax.ShapeDtypeStruct((ck, cn), BF16),
            jax.ShapeDtypeStruct((dk, dn), BF16),
        ],
        scratch_shapes=[pltpu.VMEM((tm, k), BF16), pltpu.VMEM((tm, k), BF16),
                        pltpu.VMEM((tm, nf), F32), pltpu.VMEM((tm, nf), F32)],
        compiler_params=pltpu.CompilerParams(
            dimension_semantics=("arbitrary", "arbitrary"),
            vmem_limit_bytes=_vmem_limit(blocks, scratch, 2 * _nbytes((tm, tn), F32)),
        ),
        name="in_proj",
    )(x, g.reshape(1, k), w_t, wf_t, cast_w, cast_gain.astype(F32).reshape(ck, 1), plain_w)


def _scaled_matmul_kernel(a_ref, ss_ref, w_ref, o_ref, *, eps):
    r = _row_scale(ss_ref, a_ref.shape[1], eps)
    acc = jnp.dot(a_ref[...], w_ref[...], preferred_element_type=F32)
    o_ref[...] = (_lane_tile(r, o_ref.shape[1] // V7X_LANES) * acc).astype(o_ref.dtype)


def _scaled_matmul(a, ss, w, *, eps=NORM_EPS, tm=1024, tn=1024, name):
    m, k = a.shape
    n = w.shape[1]
    tm = _pick_tile(m, tm)
    tn = _pick_tile(n, tn)
    nss = ss.shape[1]
    blocks = (_nbytes((tm, k), BF16) + _nbytes((tm, nss), F32) + _nbytes((k, tn), BF16)
              + _nbytes((tm, tn), BF16))
    return pl.pallas_call(
        functools.partial(_scaled_matmul_kernel, eps=eps),
        grid=(m // tm, n // tn),
        in_specs=[
            pl.BlockSpec((tm, k), lambda i, j: (i, 0)),
            pl.BlockSpec((tm, nss), lambda i, j: (i, 0)),
            pl.BlockSpec((k, tn), lambda i, j: (0, j)),
        ],
        out_specs=pl.BlockSpec((tm, tn), lambda i, j: (i, j)),
        out_shape=jax.ShapeDtypeStruct((m, n), BF16),
        compiler_params=pltpu.CompilerParams(
            dimension_semantics=("parallel", "arbitrary"),
            vmem_limit_bytes=_vmem_limit(blocks, 0, 2 * _nbytes((tm, tn), F32)),
        ),
        name=name,
    )(a, ss, w)


def _matmul_resid_kernel(a1_ref, a2_ref, w1_ref, w2_ref, r_ref, o_ref, ob_ref, ss_ref):
    acc = jnp.dot(a1_ref[...], w1_ref[...], preferred_element_type=F32)
    acc = acc + jnp.dot(a2_ref[...], w2_ref[...], preferred_element_type=F32)
    h = r_ref[...] + acc
    o_ref[...] = h
    ob_ref[...] = h.astype(BF16)
    ss_ref[...] = jnp.broadcast_to(jnp.sum(h * h, axis=1, keepdims=True), ss_ref.shape)


def _matmul_resid(a1, a2, a_blk, w1, w2, w_blk, resid, *, tm=1024, tn=512, name):
    (a1, a1c), (a2, a2c) = a1, a2
    (w1, w1r), (w2, w2r) = w1, w2
    m, n = resid.shape
    tm = _pick_tile(m, tm)
    tn = _pick_tile(n, tn)
    blocks = (2 * _nbytes((tm, a_blk), BF16) + 2 * _nbytes((w_blk, tn), BF16)
              + 2 * _nbytes((tm, tn), F32) + _nbytes((tm, tn), BF16) + _nbytes((tm, V7X_LANES), F32))
    return pl.pallas_call(
        _matmul_resid_kernel,
        grid=(m // tm, n // tn),
        in_specs=[
            pl.BlockSpec((tm, a_blk), lambda i, j: (i, a1c)),
            pl.BlockSpec((tm, a_blk), lambda i, j: (i, a2c)),
            pl.BlockSpec((w_blk, tn), lambda i, j: (w1r, j)),
            pl.BlockSpec((w_blk, tn), lambda i, j: (w2r, j)),
            pl.BlockSpec((tm, tn), lambda i, j: (i, j)),
        ],
        out_specs=[
            pl.BlockSpec((tm, tn), lambda i, j: (i, j)),
            pl.BlockSpec((tm, tn), lambda i, j: (i, j)),
            pl.BlockSpec((tm, V7X_LANES), lambda i, j: (i, j)),
        ],
        out_shape=[
            jax.ShapeDtypeStruct((m, n), F32),
            jax.ShapeDtypeStruct((m, n), BF16),
            jax.ShapeDtypeStruct((m, (n // tn) * V7X_LANES), F32),
        ],
        compiler_params=pltpu.CompilerParams(
            dimension_semantics=("parallel", "arbitrary"),
            vmem_limit_bytes=_vmem_limit(blocks, 0, 3 * _nbytes((tm, tn), F32)),
        ),
        name=name,
    )(a1, a2, w1, w2, resid)


def _gate_cumsum_kernel(f_ref, b_ref, o_ref, *, n_heads):
    z = f_ref[...] + b_ref[...]
    x = jnp.minimum(z, 0.0) - jnp.log1p(jnp.exp(-jnp.abs(z)))
    s_len = x.shape[0]
    row = lax.broadcasted_iota(jnp.int32, x.shape, 0)
    d = 1
    while d < s_len:
        x = x + jnp.where(row >= d, pltpu.roll(x, d, axis=0), 0.0)
        d *= 2
    c2 = x * (-LOG2E)
    hi = c2.astype(BF16)
    r1 = c2 - hi.astype(F32)
    mid = r1.astype(BF16)
    lo = (r1 - mid.astype(F32)).astype(BF16)
    lane = lax.broadcasted_iota(jnp.int32, x.shape, 1)
    mid_s = pltpu.roll(mid.astype(F32), n_heads, axis=1)
    lo_s = pltpu.roll(lo.astype(F32), 2 * n_heads, axis=1)
    out = jnp.where(lane < n_heads, hi.astype(F32),
                    jnp.where(lane < 2 * n_heads, mid_s, jnp.where(lane < 3 * n_heads, lo_s, 0.0)))
    o_ref[...] = out.astype(BF16)


def _gate_cumsum(f_logit, b_pad, batch, seq, n_heads):
    nf = f_logit.shape[1]
    assert 3 * n_heads <= nf
    return pl.pallas_call(
        functools.partial(_gate_cumsum_kernel, n_heads=n_heads),
        grid=(batch,),
        in_specs=[
            pl.BlockSpec((seq, nf), lambda b: (b, 0)),
            pl.BlockSpec((1, nf), lambda b: (0, 0)),
        ],
        out_specs=pl.BlockSpec((seq, nf), lambda b: (b, 0)),
        out_shape=jax.ShapeDtypeStruct((batch * seq, nf), BF16),
        compiler_params=pltpu.CompilerParams(
            dimension_semantics=("parallel",),
            vmem_limit_bytes=_vmem_limit(2 * _nbytes((seq, nf), F32), 0, 8 * _nbytes((seq, nf), F32)),
        ),
        name="gate_cumsum",
    )(f_logit, b_pad)


def _bias_tile_kernel(rb_ref, o_ref, *, tile, n_heads):
    h = pl.program_id(0)
    d = pl.program_id(1)
    half, max_exact, thresholds = _t5_thresholds()
    assert tile >= thresholds[-1], "tile 2 must lie wholly in the last (saturated) past bucket"

    @pl.when(d < 2)
    def _():
        t = lax.broadcasted_iota(jnp.int32, (tile, tile), 0)
        s = lax.broadcasted_iota(jnp.int32, (tile, tile), 1)
        rel = s - t - d * tile
        n = jnp.abs(rel)
        large = jnp.full((tile, tile), max_exact, jnp.int32)
        for thr in thresholds:
            large = large + (n >= thr).astype(jnp.int32)
        idx = jnp.where(n < max_exact, n, large) + jnp.where(rel > 0, half, 0)
        val = jnp.zeros((tile, tile), F32)
        for b in range(N_BUCKETS):
            val = jnp.where(idx == b, rb_ref[b * n_heads + h], val)
        shift = int(math.log2(CHUNK))
        allowed = (s >> shift) <= ((t >> shift) + d * tile)
        o_ref[0, 0] = jnp.where(allowed, val * LOG2E, NEG_INF)

    @pl.when(d == 2)
    def _():
        o_ref[0, 0] = jnp.full((tile, tile), rb_ref[(half - 1) * n_heads + h] * LOG2E, F32)


def _bias_tiles(rel_bias, tile):
    n_heads = rel_bias.shape[1]
    return pl.pallas_call(
        functools.partial(_bias_tile_kernel, tile=tile, n_heads=n_heads),
        grid=(n_heads, 3),
        in_specs=[pl.BlockSpec(memory_space=pltpu.SMEM)],
        out_specs=pl.BlockSpec((1, 1, tile, tile), lambda h, d: (h, d, 0, 0)),
        out_shape=jax.ShapeDtypeStruct((n_heads, 3, tile, tile), F32),
        compiler_params=pltpu.CompilerParams(
            dimension_semantics=("parallel", "parallel"),
            vmem_limit_bytes=_vmem_limit(_nbytes((tile, tile), F32), 0, 8 * _nbytes((tile, tile), F32)),
        ),
        name="t5_bias_tiles",
    )(rel_bias.reshape(-1))


def _lane_tile(x, reps):
    return x if reps == 1 else jnp.concatenate([x] * reps, axis=1)


def _qk(q, k):
    return lax.dot_general(q, k, (((1,), (1,)), ((), ())), preferred_element_type=F32)


def _pipelined_tiles(n_tiles, score, softmax, pv):
    if n_tiles == 0:
        return
    score(0, 0)
    if n_tiles == 1:
        softmax(0, 0)
        pv(0, 0)
        return
    score(1, 1)
    softmax(0, 0)
    n_pairs = (n_tiles - 2) // 2

    def pair(t, carry):
        n = 2 * t
        score(n + 2, 0)
        softmax(n + 1, 1)
        pv(n, 0)
        score(n + 3, 1)
        softmax(n + 2, 0)
        pv(n + 1, 1)
        return carry

    lax.fori_loop(0, n_pairs, pair, 0)
    last = n_tiles - 1
    if n_tiles % 2 == 0:
        softmax(last, 1)
        pv(last - 1, 0)
        pv(last, 1)
    else:
        score(last, 0)
        softmax(last - 1, 1)
        pv(last - 2, 0)
        softmax(last, 0)
        pv(last - 1, 1)
        pv(last, 0)


def _causal_tile_lists(nq):
    pairs = [(i, j) for i in range(nq) for j in range(i)]
    qi = jnp.asarray([p[0] for p in pairs] or [0], jnp.int32)
    kj = jnp.asarray([p[1] for p in pairs] or [0], jnp.int32)
    return len(pairs), qi, kj


def _fox_kernel(qi_ref, kj_ref, q_ref, k_ref, v_ref, cx_ref, o_ref, qx_ref, kx_ref, kt_ref, vx_ref,
                s0_ref, s1_ref, p0_ref, p1_ref, a0_ref, a1_ref, m_ref, acc_ref,
                *, tile, scale, n_heads, n_lower):
    s_bufs, p_bufs, a_bufs = (s0_ref, s1_ref), (p0_ref, p1_ref), (a0_ref, a1_ref)
    h = pl.program_id(1)
    hd = HEAD_DIM
    seq = q_ref.shape[0]
    nq = seq // tile

    lane = lax.broadcasted_iota(jnp.int32, (seq, cx_ref.shape[1]), 1)
    sel = (lane == h) | (lane == h + n_heads) | (lane == h + 2 * n_heads)
    qx_ref[:, :hd] = (q_ref[...].astype(F32) * (scale * LOG2E)).astype(BF16)
    qx_ref[:, hd:] = jnp.where(sel, 1.0, 0.0).astype(BF16)
    kx_ref[:, :hd] = k_ref[...]
    kx_ref[:, hd:] = cx_ref[...]
    vx_ref[:, :hd] = v_ref[...]
    vx_ref[:, hd:] = jnp.ones((seq, vx_ref.shape[1] - hd), BF16)
    m_ref[...] = jnp.full(m_ref.shape, NEG_INF, F32)
    acc_ref[...] = jnp.zeros(acc_ref.shape, F32)

    def rows_of(block):
        return pl.ds(pl.multiple_of(block * tile, tile), tile)

    def transpose_keys(kb, carry):
        kt_ref[:, rows_of(kb)] = kx_ref[rows_of(kb), :].astype(F32).T.astype(BF16)
        return carry

    lax.fori_loop(0, nq, transpose_keys, 0)

    def score_stage(qb, kb, slot):
        s_bufs[slot][...] = jnp.dot(qx_ref[rows_of(qb), :], kt_ref[:, rows_of(kb)], preferred_element_type=F32)

    def softmax_stage(qb, slot, masked):
        s = s_bufs[slot][...]
        if masked:
            row = lax.broadcasted_iota(jnp.int32, s.shape, 0)
            col = lax.broadcasted_iota(jnp.int32, s.shape, 1)
            s = jnp.where(col <= row, s, NEG_INF)
        m_prev = m_ref[rows_of(qb), :]
        m_new = jnp.maximum(m_prev, jnp.max(s, axis=1, keepdims=True))
        a_bufs[slot][...] = jnp.exp2(m_prev - m_new)
        p_bufs[slot][...] = jnp.exp2(s - _lane_tile(m_new, tile // V7X_LANES)).astype(BF16)
        m_ref[rows_of(qb), :] = m_new

    def pv_stage(qb, kb, slot):
        pv = jnp.dot(p_bufs[slot][...], vx_ref[rows_of(kb), :], preferred_element_type=F32)
        acc = acc_ref[rows_of(qb), :]
        acc_ref[rows_of(qb), :] = _lane_tile(a_bufs[slot][...], acc.shape[1] // V7X_LANES) * acc + pv

    _pipelined_tiles(
        n_lower,
        lambda n, slot: score_stage(qi_ref[n], kj_ref[n], slot),
        lambda n, slot: softmax_stage(qi_ref[n], slot, False),
        lambda n, slot: pv_stage(qi_ref[n], kj_ref[n], slot))
    _pipelined_tiles(
        nq,
        lambda n, slot: score_stage(n, n, slot),
        lambda n, slot: softmax_stage(n, slot, True),
        lambda n, slot: pv_stage(n, n, slot))

    def finish(qb, carry):
        acc = acc_ref[rows_of(qb), :]
        o_ref[rows_of(qb), :] = (acc[:, :hd] / acc[:, hd:2 * hd]).astype(o_ref.dtype)
        return carry

    lax.fori_loop(0, nq, finish, 0)


def _fox_attention(proj, cx, batch, seq, n_heads, tile):
    nq = seq // tile
    hd = HEAD_DIM
    nx = cx.shape[1]
    n_lower, qi, kj = _causal_tile_lists(nq)
    blocks = 4 * _nbytes((seq, hd), BF16) + _nbytes((seq, nx), BF16)
    scratch = (4 * _nbytes((seq, hd + nx), BF16) + 2 * _nbytes((tile, tile), F32)
               + 2 * _nbytes((tile, tile), BF16) + 2 * _nbytes((tile, V7X_LANES), F32)
               + _nbytes((seq, V7X_LANES), F32) + _nbytes((seq, 2 * hd), F32))
    smem = pl.BlockSpec(memory_space=pltpu.SMEM)
    return pl.pallas_call(
        functools.partial(_fox_kernel, tile=tile, scale=hd ** -0.5, n_heads=n_heads, n_lower=n_lower),
        grid=(batch, n_heads),
        in_specs=[
            smem, smem,
            pl.BlockSpec((seq, hd), lambda b, h: (b, h)),
            pl.BlockSpec((seq, hd), lambda b, h: (b, n_heads + h)),
            pl.BlockSpec((seq, hd), lambda b, h: (b, 2 * n_heads + h)),
            pl.BlockSpec((seq, nx), lambda b, h: (b, 0)),
        ],
        out_specs=pl.BlockSpec((seq, hd), lambda b, h: (b, h)),
        out_shape=jax.ShapeDtypeStruct((batch * seq, n_heads * hd), BF16),
        scratch_shapes=[
            pltpu.VMEM((seq, hd + nx), BF16),
            pltpu.VMEM((seq, hd + nx), BF16),
            pltpu.VMEM((hd + nx, seq), BF16),
            pltpu.VMEM((seq, 2 * hd), BF16),
            pltpu.VMEM((tile, tile), F32), pltpu.VMEM((tile, tile), F32),
            pltpu.VMEM((tile, tile), BF16), pltpu.VMEM((tile, tile), BF16),
            pltpu.VMEM((tile, V7X_LANES), F32), pltpu.VMEM((tile, V7X_LANES), F32),
            pltpu.VMEM((seq, V7X_LANES), F32),
            pltpu.VMEM((seq, 2 * hd), F32),
        ],
        compiler_params=pltpu.CompilerParams(
            dimension_semantics=("parallel", "arbitrary"),
            vmem_limit_bytes=_vmem_limit(blocks, scratch, 12 * _nbytes((tile, tile), F32)),
        ),
        name="fox_attention",
    )(qi, kj, proj, proj, proj, cx)


def _diff_kernel(blk_ref, qs_ref, kj_ref, q_ref, k_ref, v_ref, bias_ref, lq1_ref, lk1_ref, lq2_ref, lk2_ref,
                 g_ref, o_ref, qq_ref, s0_ref, s1_ref, p0_ref, p1_ref, a0_ref, a1_ref, m_ref, l_ref, acc_ref,
                 *, tile, scale, lambda_init, n_tiles, n_slots):
    s_bufs, p_bufs, a_bufs = (s0_ref, s1_ref), (p0_ref, p1_ref), (a0_ref, a1_ref)
    grp = pl.program_id(2)
    hd = HEAD_DIM
    two = 2 * tile

    def rows_of(block, size=tile):
        return pl.ds(pl.multiple_of(block * size, size), size)

    def stage_queries(slot, carry):
        qf = q_ref[rows_of(blk_ref[grp, slot]), :].astype(F32) * (scale * LOG2E)
        lane = lax.broadcasted_iota(jnp.int32, qf.shape, 1)
        base = pl.multiple_of(slot * two, two)
        qq_ref[pl.ds(base, tile), :] = jnp.where(lane < hd, qf, 0.0).astype(BF16)
        qq_ref[pl.ds(base + tile, tile), :] = jnp.where(lane >= hd, qf, 0.0).astype(BF16)
        return carry

    lax.fori_loop(0, n_slots, stage_queries, 0)
    m_ref[...] = jnp.full(m_ref.shape, NEG_INF, F32)
    l_ref[...] = jnp.zeros(l_ref.shape, F32)
    acc_ref[...] = jnp.zeros(acc_ref.shape, F32)

    def score_stage(n, slot):
        qs, kb = qs_ref[grp, n], kj_ref[grp, n]
        qk = _qk(qq_ref[rows_of(qs, two), :], k_ref[rows_of(kb), :])
        bias = bias_ref[0, jnp.minimum(blk_ref[grp, qs] - kb, 2)]
        s_bufs[slot][:tile, :] = qk[:tile] + bias
        s_bufs[slot][tile:, :] = qk[tile:] + bias

    def softmax_stage(n, slot):
        qs = qs_ref[grp, n]
        for c in range(2):
            rows = slice(c * tile, (c + 1) * tile)
            state = pl.ds(pl.multiple_of(qs * two + c * tile, tile), tile)
            s = s_bufs[slot][rows, :]
            m_prev = m_ref[state, :]
            m_new = jnp.maximum(m_prev, jnp.max(s, axis=1, keepdims=True))
            alpha = jnp.exp2(m_prev - m_new)
            p = jnp.exp2(s - _lane_tile(m_new, tile // V7X_LANES))
            l_ref[state, :] = alpha * l_ref[state, :] + jnp.sum(p, axis=1, keepdims=True)
            a_bufs[slot][rows, :] = alpha
            p_bufs[slot][rows, :] = p.astype(BF16)
            m_ref[state, :] = m_new

    def pv_stage(n, slot):
        state = rows_of(qs_ref[grp, n], two)
        pv = jnp.dot(p_bufs[slot][...], v_ref[rows_of(kj_ref[grp, n]), :], preferred_element_type=F32)
        acc = acc_ref[state, :]
        acc_ref[state, :] = _lane_tile(a_bufs[slot][...], acc.shape[1] // V7X_LANES) * acc + pv

    _pipelined_tiles(n_tiles, score_stage, softmax_stage, pv_stage)

    lam = (jnp.exp(jnp.sum(lq1_ref[...] * lk1_ref[...], axis=1, keepdims=True))
           - jnp.exp(jnp.sum(lq2_ref[...] * lk2_ref[...], axis=1, keepdims=True))
           + lambda_init)

    def finish(slot, carry):
        base = pl.multiple_of(slot * two, two)
        c1, c2 = pl.ds(base, tile), pl.ds(base + tile, tile)
        reps = acc_ref.shape[1] // V7X_LANES
        out = (acc_ref[c1, :] / _lane_tile(l_ref[c1, :], reps)
               - lam * (acc_ref[c2, :] / _lane_tile(l_ref[c2, :], reps)))
        ms = jnp.mean(out * out, axis=-1, keepdims=True)
        y = out * lax.rsqrt(ms + SUBLN_EPS) * g_ref[...]
        o_ref[rows_of(blk_ref[grp, slot]), :] = (y * (1.0 - lambda_init)).astype(o_ref.dtype)
        return carry

    lax.fori_loop(0, n_slots, finish, 0)


def _balanced_causal_groups(nq):
    n_groups = 2 if nq % 4 == 0 else 1
    groups = [[] for _ in range(n_groups)]
    for p in range((nq + 1) // 2):
        for blk in sorted({p, nq - 1 - p}):
            groups[p % n_groups].append(blk)
    slots = [[s for s, blk in enumerate(g) for _ in range(blk + 1)] for g in groups]
    keys = [[j for blk in g for j in range(blk + 1)] for g in groups]
    assert len({len(s) for s in slots}) == 1 and len({len(g) for g in groups}) == 1
    return groups, slots, keys


def _diff_attention(proj, bias, lq1, lk1, lq2, lk2, g_subln, batch, seq, n_heads, col0, tile, lambda_init):
    nq = seq // tile
    hd2 = 2 * HEAD_DIM
    c0 = col0 // hd2
    groups, slots, keys = _balanced_causal_groups(nq)
    n_groups, n_slots, n_tiles = len(groups), len(groups[0]), len(slots[0])
    vec = lambda: pl.BlockSpec((1, HEAD_DIM), lambda h, b, g: (0, 0))
    smem = pl.BlockSpec(memory_space=pltpu.SMEM)
    blocks = 4 * _nbytes((seq, hd2), BF16) + _nbytes((3, tile, tile), F32)
    state_rows = 2 * n_slots * tile
    scratch = (_nbytes((state_rows, hd2), BF16) + 2 * _nbytes((2 * tile, tile), F32)
               + 2 * _nbytes((2 * tile, tile), BF16) + 2 * _nbytes((2 * tile, V7X_LANES), F32)
               + 2 * _nbytes((state_rows, V7X_LANES), F32) + _nbytes((state_rows, hd2), F32))
    return pl.pallas_call(
        functools.partial(_diff_kernel, tile=tile, scale=HEAD_DIM ** -0.5, lambda_init=lambda_init,
                          n_tiles=n_tiles, n_slots=n_slots),
        grid=(n_heads, batch, n_groups),
        in_specs=[
            smem, smem, smem,
            pl.BlockSpec((seq, hd2), lambda h, b, g: (b, c0 + h)),
            pl.BlockSpec((seq, hd2), lambda h, b, g: (b, c0 + n_heads + h)),
            pl.BlockSpec((seq, hd2), lambda h, b, g: (b, c0 + 2 * n_heads + h)),
            pl.BlockSpec((1, 3, tile, tile), lambda h, b, g: (h, 0, 0, 0)),
            vec(), vec(), vec(), vec(),
            pl.BlockSpec((1, hd2), lambda h, b, g: (0, 0)),
        ],
        out_specs=pl.BlockSpec((seq, hd2), lambda h, b, g: (b, h)),
        out_shape=jax.ShapeDtypeStruct((batch * seq, n_heads * hd2), BF16),
        scratch_shapes=[
            pltpu.VMEM((state_rows, hd2), BF16),
            pltpu.VMEM((2 * tile, tile), F32), pltpu.VMEM((2 * tile, tile), F32),
            pltpu.VMEM((2 * tile, tile), BF16), pltpu.VMEM((2 * tile, tile), BF16),
            pltpu.VMEM((2 * tile, V7X_LANES), F32), pltpu.VMEM((2 * tile, V7X_LANES), F32),
            pltpu.VMEM((state_rows, V7X_LANES), F32),
            pltpu.VMEM((state_rows, V7X_LANES), F32),
            pltpu.VMEM((state_rows, hd2), F32),
        ],
        compiler_params=pltpu.CompilerParams(
            dimension_semantics=("parallel", "parallel", "arbitrary"),
            vmem_limit_bytes=_vmem_limit(blocks, scratch, 12 * _nbytes((tile, tile), F32)),
        ),
        name="diff_attention",
    )(jnp.asarray(groups, jnp.int32), jnp.asarray(slots, jnp.int32), jnp.asarray(keys, jnp.int32),
      proj, proj, proj, bias, lq1, lk1, lq2, lk2, g_subln)


def _cross_kernel(q_ref, k_ref, v_ref, o_ref, *, n_heads, scale):
    dh = q_ref.shape[1] // n_heads
    for h in range(n_heads):
        cols = slice(h * dh, (h + 1) * dh)
        s = _qk(q_ref[:, cols], k_ref[:, cols]) * scale
        m = jnp.max(s, axis=1, keepdims=True)
        p = jnp.exp(s - m)
        p = p / jnp.sum(p, axis=1, keepdims=True)
        o_ref[:, cols] = jnp.dot(p.astype(BF16), v_ref[:, cols], preferred_element_type=F32).astype(o_ref.dtype)


def _cross_attention(q, k_mem, v_mem, batch, seq, n_mem, tile):
    d = q.shape[1]
    nq = seq // tile
    blocks = 2 * _nbytes((tile, d), BF16) + 2 * _nbytes((n_mem, d), BF16)
    return pl.pallas_call(
        functools.partial(_cross_kernel, n_heads=H_MEM, scale=(d // H_MEM) ** -0.5),
        grid=(batch, nq),
        in_specs=[
            pl.BlockSpec((tile, d), lambda b, i: (b * nq + i, 0)),
            pl.BlockSpec((n_mem, d), lambda b, i: (b, 0)),
            pl.BlockSpec((n_mem, d), lambda b, i: (b, 0)),
        ],
        out_specs=pl.BlockSpec((tile, d), lambda b, i: (b * nq + i, 0)),
        out_shape=jax.ShapeDtypeStruct((batch * seq, d), BF16),
        compiler_params=pltpu.CompilerParams(
            dimension_semantics=("parallel", "arbitrary"),
            vmem_limit_bytes=_vmem_limit(blocks, 0, 8 * _nbytes((tile, n_mem), F32) + _nbytes((tile, d), F32)),
        ),
        name="cross_attention",
    )(q, k_mem, v_mem)


FINAL_NORM_ROWS = 32


def _mlp_kernel(a_ref, ss_ref, hs_ref, wu_ref, wd_ref, gf_ref, o_ref, *, eps):
    f = pl.program_id(1)
    rows = hs_ref.shape[0]

    @pl.when(f == 0)
    def _():
        o_ref[...] = jnp.zeros(o_ref.shape, F32)

    r0 = pl.multiple_of(f * rows, rows)
    o_ref[pl.ds(r0, rows), :] += hs_ref[...]

    r = _row_scale(ss_ref, a_ref.shape[1], eps)
    u = jnp.dot(a_ref[...], wu_ref[...], preferred_element_type=F32)
    act = (jnp.square(jnp.maximum(u, 0.0)) * _lane_tile(r * r, u.shape[1] // V7X_LANES)).astype(BF16)
    o_ref[...] += jnp.dot(act, wd_ref[...], preferred_element_type=F32)

    @pl.when(f == pl.num_programs(1) - 1)
    def _():
        def norm_rows(c, carry):
            c0 = pl.multiple_of(c * FINAL_NORM_ROWS, FINAL_NORM_ROWS)
            y = o_ref[pl.ds(c0, FINAL_NORM_ROWS), :]
            ms = jnp.mean(y * y, axis=-1, keepdims=True)
            o_ref[pl.ds(c0, FINAL_NORM_ROWS), :] = y * lax.rsqrt(ms + eps) * gf_ref[...]
            return carry

        lax.fori_loop(0, o_ref.shape[0] // FINAL_NORM_ROWS, norm_rows, 0)


def _mlp(a, ss, h, w_up, w_down, g_final, *, eps=NORM_EPS, tm=512, tf=512):
    m, d = h.shape
    dff = w_up.shape[1]
    tm = _pick_tile(m, tm)
    tf = _pick_tile(dff, tf)
    nf = dff // tf
    rows = tm // nf
    assert rows * nf == tm and rows % F32_ROWS_PER_TILE == 0 and tm % FINAL_NORM_ROWS == 0
    nss = ss.shape[1]
    blocks = (2 * _nbytes((d, tf), BF16) + _nbytes((tm, d), BF16) + _nbytes((tm, nss), F32)
              + _nbytes((rows, d), F32) + _nbytes((tm, d), F32))
    return pl.pallas_call(
        functools.partial(_mlp_kernel, eps=eps),
        grid=(m // tm, nf),
        in_specs=[
            pl.BlockSpec((tm, d), lambda i, f: (i, 0)),
            pl.BlockSpec((tm, nss), lambda i, f: (i, 0)),
            pl.BlockSpec((rows, d), lambda i, f: (i * nf + f, 0)),
            pl.BlockSpec((d, tf), lambda i, f: (0, f)),
            pl.BlockSpec((tf, d), lambda i, f: (f, 0)),
            pl.BlockSpec((1, d), lambda i, f: (0, 0)),
        ],
        out_specs=pl.BlockSpec((tm, d), lambda i, f: (i, 0)),
        out_shape=jax.ShapeDtypeStruct((m, d), F32),
        compiler_params=pltpu.CompilerParams(
            dimension_semantics=("parallel", "arbitrary"),
            vmem_limit_bytes=_vmem_limit(blocks, 0, 4 * _nbytes((tm, tf), F32)),
        ),
        name="mlp_final_norm",
    )(a, ss, h, w_up, w_down, g_final.reshape(1, d))


def _attn_tile(seq):
    return _pick_tile(seq, 512)


def _layer(h, mem, l, g_mix, w_in, b_forget, lambda_q1, lambda_k1, lambda_q2, lambda_k2, g_subln, rel_bias,
           w_out, g_cross, g_mem, wq_mem, wk_mem, wv_mem, wo_mem, g_mlp, w_up, w_down, g_final):
    batch, seq, d = h.shape
    n_mem = mem.shape[1]
    h_fox = b_forget.shape[-1]
    h_diff = rel_bias.shape[1]
    w_fox = h_fox * HEAD_DIM
    tile = _attn_tile(seq)
    x2 = h.reshape(batch * seq, d)

    def gained(g, w):
        return (g.astype(F32)[:, None] * w).astype(BF16)

    f0 = 3 * w_fox
    w_t = jnp.swapaxes(w_in[l], 0, 1).astype(BF16)
    n_gate = -(-h_fox // V7X_LANES) * V7X_LANES
    w_gate_t = jnp.pad(w_t[f0:f0 + h_fox], ((0, n_gate - h_fox), (0, 0)))
    b_gate = jnp.pad(b_forget[l].astype(F32), (0, n_gate - h_fox)).reshape(1, n_gate)
    skip = (f0, h_fox)
    if h_fox % BF16_ROWS_PER_TILE:
        w_t, skip = jnp.concatenate([w_t[:f0], w_t[f0 + h_fox:]], axis=0), (f0, 0)

    proj, f_logit, w_up_b, w_down_b = _in_proj(x2, g_mix[l], w_t, skip, w_gate_t, w_up, l, g_mlp[l], w_down)

    cx = _gate_cumsum(f_logit, b_gate, batch, seq, h_fox)
    fox = _fox_attention(proj, cx, batch, seq, h_fox, tile)

    lambda_init = 0.8 - 0.6 * math.exp(-0.3 * l)
    bias = _bias_tiles(rel_bias.astype(F32), tile)
    row = lambda v: v[l].astype(F32).reshape(1, -1)
    diff = _diff_attention(proj, bias, row(lambda_q1), row(lambda_k1), row(lambda_q2), row(lambda_k2),
                           row(g_subln), batch, seq, h_diff, 3 * w_fox, tile, lambda_init)

    w_o = w_out[l].astype(BF16)
    h1, h1b, ss1 = _matmul_resid((fox, 0), (diff, 0), w_fox, (w_o, 0), (w_o, 1), w_fox, x2, name="out_proj")

    q = _scaled_matmul(h1b, ss1, gained(g_cross[l], wq_mem[l]), name="cross_q_proj")
    mem2 = mem.reshape(batch * n_mem, d)
    k_mem = _norm_matmul(mem2, g_mem[l], wk_mem[l].astype(BF16), name="cross_k_proj")
    v_mem = _norm_matmul(mem2, g_mem[l], wv_mem[l].astype(BF16), name="cross_v_proj")
    o = _cross_attention(q, k_mem, v_mem, batch, seq, n_mem, tile)
    w_om = wo_mem[l].astype(BF16)
    half = d // 2
    h2, h2b, ss2 = _matmul_resid((o, 0), (o, 1), half, (w_om, 0), (w_om, 1), half, h1, name="cross_o_proj")

    out = _mlp(h2b, ss2, h2, w_up_b, w_down_b, g_final)
    return out.reshape(batch, seq, d)


def kernel(x, mem, g_mix, w_in, b_forget, lambda_q1, lambda_k1, lambda_q2, lambda_k2, g_subln, rel_bias, w_out,
           g_cross, g_mem, wq_mem, wk_mem, wv_mem, wo_mem, g_mlp, w_up, w_down, g_final):
    depth = g_mix.shape[0]
    assert depth == 1, "the fused MLP epilogue applies the final norm; only depth 1 is supported"
    return _layer(x, mem, 0, g_mix, w_in, b_forget, lambda_q1, lambda_k1, lambda_q2, lambda_k2, g_subln,
                  rel_bias, w_out, g_cross, g_mem, wq_mem, wk_mem, wv_mem, wo_mem, g_mlp, w_up, w_down,
                  g_final)
```
